```python
import math
import jax, jax.numpy as jnp
from jax import lax
import numpy as np

D_MODEL = 1024
BATCH = 4
SEQ = 8192
DEPTH = 2

GRID_W = 64
CTX_LEN = 256
HEAD_DIM = 64
N_HEADS = D_MODEL // HEAD_DIM
NA_HEADS = N_HEADS // 2
GQA_Q_HEADS = N_HEADS - NA_HEADS
GQA_KV_HEADS = GQA_Q_HEADS // 4
WIN_R = 8
WIN_C = 16
DIFF_HEADS = N_HEADS // 2
DIFF_V_DIM = 2 * HEAD_DIM
D_FF = -(-8 * D_MODEL // (3 * 256)) * 256
Q_BLOCK = 128
ROPE_THETA = 10000.0
EPS = 1e-6
N_MOD = 6

NA_WIDTH = NA_HEADS * HEAD_DIM
GQA_Q_WIDTH = GQA_Q_HEADS * HEAD_DIM
GQA_KV_WIDTH = GQA_KV_HEADS * HEAD_DIM
PAR_IN = 3 * NA_WIDTH + GQA_Q_WIDTH + 2 * GQA_KV_WIDTH
PAR_SPLITS = (NA_WIDTH, 2 * NA_WIDTH, 3 * NA_WIDTH, 3 * NA_WIDTH + GQA_Q_WIDTH,
              3 * NA_WIDTH + GQA_Q_WIDTH + GQA_KV_WIDTH)
PAR_OUT = (NA_HEADS + GQA_Q_HEADS) * HEAD_DIM
DIFF_IN = 3 * DIFF_HEADS * 2 * HEAD_DIM
DIFF_OUT = DIFF_HEADS * DIFF_V_DIM

kernel_name = "hybrid_natten_gqa_diffattn_dit"


def rms_norm(x, gain=None):
    xf = x.astype(jnp.float32)
    y = xf * lax.rsqrt(jnp.mean(xf * xf, axis=-1, keepdims=True) + EPS)
    if gain is not None:
        y = y * gain.astype(jnp.float32)
    return y.astype(x.dtype)


def ada_params(cond, w, b):
    m = jax.nn.silu(cond) @ w + b
    return jnp.split(m[..., None, :], N_MOD, axis=-1)


def modulate(h, shift, scale):
    return h * (1.0 + scale) + shift


def axial_angles(n_tokens):
    t = jnp.arange(n_tokens)
    row = (t // GRID_W).astype(jnp.float32)
    col = (t % GRID_W).astype(jnp.float32)
    n_freq = HEAD_DIM // 4
    inv = ROPE_THETA ** (-jnp.arange(n_freq, dtype=jnp.float32) / n_freq)
    ang = jnp.concatenate([row[:, None] * inv, col[:, None] * inv], axis=-1)
    return jnp.cos(ang), jnp.sin(ang)


def apply_rope(x, cos, sin):
    xf = x.astype(jnp.float32).reshape(x.shape[:-1] + (HEAD_DIM // 2, 2))
    x1, x2 = xf[..., 0], xf[..., 1]
    out = jnp.stack([x1 * cos - x2 * sin, x1 * sin + x2 * cos], axis=-1)
    return out.reshape(x.shape).astype(x.dtype)


def to_heads(t, n_heads):
    b, l, _ = t.shape
    return t.reshape(b, l, n_heads, -1).transpose(0, 2, 1, 3)


def merge_heads(o):
    b, h, l, d = o.shape
    return o.transpose(0, 2, 1, 3).reshape(b, l, h * d)


def grouped_attention(q, k, v):
    b, hq, lq, dh = q.shape
    hkv = k.shape[1]
    qg = q.reshape(b, hkv, hq // hkv, lq, dh)
    s = jnp.einsum('bhgqd,bhkd->bhgqk', qg, k).astype(jnp.float32) * (dh ** -0.5)
    p = jax.nn.softmax(s, axis=-1).astype(v.dtype)
    return jnp.einsum('bhgqk,bhkd->bhgqd', p, v).reshape(b, hq, lq, v.shape[-1])


def sweep_queries(attend, *qs):
    b, h, l, _ = qs[0].shape
    nb = l // Q_BLOCK
    blocks = tuple(q.reshape(b, h, nb, Q_BLOCK, q.shape[-1]).transpose(2, 0, 1, 3, 4) for q in qs)
    out = lax.map(lambda blk: attend(*blk), blocks)
    return out.transpose(1, 2, 0, 3, 4).reshape(b, h, l, out.shape[-1])


def neighbourhood_attention(q, k, v, kc, vc, rpb):
    b, h, l, dh = q.shape
    rows = l // GRID_W
    kr = min(WIN_R, rows)
    scale = dh ** -0.5
    qg = q.reshape(b, h, rows, GRID_W, dh)
    kg = k.reshape(b, h, rows, GRID_W, dh)
    vg = v.reshape(b, h, rows, GRID_W, dh)
    cols = jnp.arange(GRID_W)
    col_start = jnp.clip(cols - WIN_C // 2, 0, GRID_W - WIN_C)
    col_idx = col_start[:, None] + jnp.arange(WIN_C)[None, :]
    col_bias = rpb[:, :, col_idx - cols[:, None] + (WIN_C - 1)]
    n_win = kr * WIN_C

    def row_block(r):
        rs = jnp.clip(r - kr // 2, 0, rows - kr)
        q_r = lax.dynamic_index_in_dim(qg, r, axis=2, keepdims=False)
        k_rows = lax.dynamic_slice_in_dim(kg, rs, kr, axis=2)
        v_rows = lax.dynamic_slice_in_dim(vg, rs, kr, axis=2)
        k_win = k_rows[:, :, :, col_idx]
        v_win = v_rows[:, :, :, col_idx]
        row_off = rs + jnp.arange(kr) - r + (WIN_R - 1)
        bias = jnp.take(col_bias, row_off, axis=1).transpose(0, 2, 1, 3)
        s_win = jnp.einsum('bhqd,bhrqjd->bhqrj', q_r, k_win).astype(jnp.float32) * scale + bias[None].astype(jnp.float32)
        s_ctx = jnp.einsum('bhqd,bhkd->bhqk', q_r, kc).astype(jnp.float32) * scale
        p = jax.nn.softmax(jnp.concatenate([s_win.reshape(b, h, GRID_W, n_win), s_ctx], axis=-1), axis=-1)
        p = p.astype(v.dtype)
        p_win = p[..., :n_win].reshape(b, h, GRID_W, kr, WIN_C)
        p_ctx = p[..., n_win:]
        return (jnp.einsum('bhqrj,bhrqjd->bhqd', p_win, v_win)
                + jnp.einsum('bhqk,bhkd->bhqd', p_ctx, vc))

    out = lax.map(row_block, jnp.arange(rows))
    return out.transpose(1, 2, 0, 3, 4).reshape(b, h, l, dh)


def parallel_mixer(h, hc, w_in, w_out, rpb, q_gain, k_gain, cos, sin, need_ctx):
    def split(p):
        nq, nk, nv, gq, gk, gv = jnp.split(p, PAR_SPLITS, axis=-1)
        return (to_heads(nq, NA_HEADS), to_heads(nk, NA_HEADS), to_heads(nv, NA_HEADS),
                rms_norm(to_heads(gq, GQA_Q_HEADS), q_gain), rms_norm(to_heads(gk, GQA_KV_HEADS), k_gain),
                to_heads(gv, GQA_KV_HEADS))
    nq, nk, nv, gq, gk, gv = split(h @ w_in)
    cnq, cnk, cnv, cgq, cgk, cgv = split(hc @ w_in)
    gq = apply_rope(gq, cos, sin)
    gk = apply_rope(gk, cos, sin)
    out_na = neighbourhood_attention(nq, nk, nv, cnk, cnv, rpb)
    k_all = jnp.concatenate([cgk, gk], axis=2)
    v_all = jnp.concatenate([cgv, gv], axis=2)
    out_gqa = sweep_queries(lambda qb: grouped_attention(qb, k_all, v_all), gq)
    y = merge_heads(jnp.concatenate([out_na, out_gqa], axis=1)) @ w_out
    yc = None
    if need_ctx:
        yc_na = grouped_attention(cnq, cnk, cnv)
        yc_g = grouped_attention(cgq, cgk, cgv)
        yc = merge_heads(jnp.concatenate([yc_na, yc_g], axis=1)) @ w_out
    return y, yc


def diff_mixer(h, hc, w_in, w_out, lq1, lk1, lq2, lk2, subln_gain, lambda_init, cos, sin, need_ctx):
    def split(p):
        b, l, _ = p.shape
        q, k, v = jnp.split(p, 3, axis=-1)
        q = q.reshape(b, l, DIFF_HEADS, 2, HEAD_DIM).transpose(0, 2, 3, 1, 4)
        k = k.reshape(b, l, DIFF_HEADS, 2, HEAD_DIM).transpose(0, 2, 3, 1, 4)
        v = v.reshape(b, l, DIFF_HEADS, DIFF_V_DIM).transpose(0, 2, 1, 3)
        return q[:, :, 0], q[:, :, 1], k[:, :, 0], k[:, :, 1], v
    lam = (jnp.exp(jnp.sum(lq1.astype(jnp.float32) * lk1.astype(jnp.float32)))
           - jnp.exp(jnp.sum(lq2.astype(jnp.float32) * lk2.astype(jnp.float32))) + lambda_init)
    scale = HEAD_DIM ** -0.5

    def attend(a1, a2, k1, k2, v):
        s1 = jnp.einsum('bhqd,bhkd->bhqk', a1, k1).astype(jnp.float32) * scale
        s2 = jnp.einsum('bhqd,bhkd->bhqk', a2, k2).astype(jnp.float32) * scale
        p = jax.nn.softmax(s1, axis=-1) - lam * jax.nn.softmax(s2, axis=-1)
        return jnp.einsum('bhqk,bhkd->bhqd', p.astype(v.dtype), v)

    def finish(o):
        return merge_heads(rms_norm(o, subln_gain) * (1.0 - lambda_init)) @ w_out

    q1, q2, k1, k2, v = split(h @ w_in)
    cq1, cq2, ck1, ck2, cv = split(hc @ w_in)
    q1, q2, k1, k2 = (apply_rope(t, cos, sin) for t in (q1, q2, k1, k2))
    k1a = jnp.concatenate([ck1, k1], axis=2)
    k2a = jnp.concatenate([ck2, k2], axis=2)
    va = jnp.concatenate([cv, v], axis=2)
    y = finish(sweep_queries(lambda a1, a2: attend(a1, a2, k1a, k2a, va), q1, q2))
    yc = None
    if need_ctx:
        yc = finish(attend(cq1, cq2, ck1, ck2, cv))
    return y, yc


def swiglu(h, w_gate, w_up, w_down):
    return (jax.nn.silu(h @ w_gate) * (h @ w_up)) @ w_down


def setup_inputs(seed: int = 0) -> dict:
    key = jax.random.key(seed)
    ks = jax.random.split(key, 22)
    n_par = (DEPTH + 1) // 2
    n_diff = DEPTH // 2
    nrm = lambda k, shape: jax.random.normal(k, shape, jnp.float32)
    w = lambda k, shape, fan_in: nrm(k, shape) * fan_in ** -0.5
    gain = lambda k, shape: 1.0 + 0.01 * nrm(k, shape)
    return {
        "x": nrm(ks[0], (BATCH, SEQ, D_MODEL)),
        "c": nrm(ks[1], (BATCH, D_MODEL)),
        "ctx": nrm(ks[2], (BATCH, CTX_LEN, D_MODEL)),
        "c_ctx": nrm(ks[3], (D_MODEL,)),
        "ada_w": 0.5 * w(ks[4], (DEPTH, D_MODEL, N_MOD * D_MODEL), D_MODEL),
        "ada_b": 0.01 * nrm(ks[5], (DEPTH, N_MOD * D_MODEL)),
        "ffn_w_gate": w(ks[6], (DEPTH, D_MODEL, D_FF), D_MODEL),
        "ffn_w_up": w(ks[7], (DEPTH, D_MODEL, D_FF), D_MODEL),
        "ffn_w_down": w(ks[8], (DEPTH, D_FF, D_MODEL), D_FF),
        "par_w_in": w(ks[9], (n_par, D_MODEL, PAR_IN), D_MODEL),
        "par_w_out": w(ks[10], (n_par, PAR_OUT, D_MODEL), PAR_OUT),
        "na_rpb": 0.02 * nrm(ks[11], (n_par, NA_HEADS, 2 * WIN_R - 1, 2 * WIN_C - 1)),
        "gqa_q_gain": gain(ks[12], (n_par, HEAD_DIM)),
        "gqa_k_gain": gain(ks[13], (n_par, HEAD_DIM)),
        "diff_w_in": w(ks[14], (n_diff, D_MODEL, DIFF_IN), D_MODEL),
        "diff_w_out": w(ks[15], (n_diff, DIFF_OUT, D_MODEL), DIFF_OUT),
        "diff_lambda_q1": 0.1 * nrm(ks[16], (n_diff, HEAD_DIM)),
        "diff_lambda_k1": 0.1 * nrm(ks[17], (n_diff, HEAD_DIM)),
        "diff_lambda_q2": 0.1 * nrm(ks[18], (n_diff, HEAD_DIM)),
        "diff_lambda_k2": 0.1 * nrm(ks[19], (n_diff, HEAD_DIM)),
        "diff_subln_gain": gain(ks[20], (n_diff, DIFF_V_DIM)),
        "final_norm_gain": gain(ks[21], (D_MODEL,)),
    }


def reference(x, c, ctx, c_ctx, ada_w, ada_b, ffn_w_gate, ffn_w_up, ffn_w_down,
              par_w_in, par_w_out, na_rpb, gqa_q_gain, gqa_k_gain,
              diff_w_in, diff_w_out, diff_lambda_q1, diff_lambda_k1, diff_lambda_q2, diff_lambda_k2,
              diff_subln_gain, final_norm_gain):
    cos, sin = axial_angles(x.shape[1])
    xc = ctx
    for l in range(DEPTH):
        need_ctx = l < DEPTH - 1
        sh1, sc1, g1, sh2, sc2, g2 = ada_params(c, ada_w[l], ada_b[l])
        csh1, csc1, cg1, csh2, csc2, cg2 = ada_params(c_ctx, ada_w[l], ada_b[l])
        h = modulate(rms_norm(x), sh1, sc1)
        hc = modulate(rms_norm(xc), csh1, csc1)
        i = l // 2
        if l % 2 == 0:
            y, yc = parallel_mixer(h, hc, par_w_in[i], par_w_out[i], na_rpb[i], gqa_q_gain[i], gqa_k_gain[i],
                                   cos, sin, need_ctx)
        else:
            lambda_init = 0.8 - 0.6 * math.exp(-0.3 * l)
            y, yc = diff_mixer(h, hc, diff_w_in[i], diff_w_out[i], diff_lambda_q1[i], diff_lambda_k1[i],
                               diff_lambda_q2[i], diff_lambda_k2[i], diff_subln_gain[i], lambda_init,
                               cos, sin, need_ctx)
        x = x + g1 * y
        x = x + g2 * swiglu(modulate(rms_norm(x), sh2, sc2), ffn_w_gate[l], ffn_w_up[l], ffn_w_down[l])
        if need_ctx:
            xc = xc + cg1 * yc
            xc = xc + cg2 * swiglu(modulate(rms_norm(xc), csh2, csc2), ffn_w_gate[l], ffn_w_up[l], ffn_w_down[l])
    return rms_norm(x, final_norm_gain)
```

```python
import functools

import numpy as np
import jax
import jax.numpy as jnp
from jax import lax
from jax.experimental import pallas as pl
from jax.experimental.pallas import tpu as pltpu

F32 = jnp.float32
BF16 = jnp.bfloat16

GRID_W = 64
HEAD_DIM = 64
WIN_R = 8
WIN_C = 16
N_MOD = 6
ROPE_THETA = 10000.0
EPS = 1e-6

LANE = 128
TM = 256
NA_ROWS_PER_TILE = TM // GRID_W
NA_KEY_ROWS = 12
NA_WIN = NA_KEY_ROWS * GRID_W
NEG = -1e30
VMEM_LIMIT = 56 * 1024 * 1024

_DEINT = np.concatenate([np.arange(0, HEAD_DIM, 2), np.arange(1, HEAD_DIM, 2)])
_GQA_HEAD_ORDER = (0, 4, 1, 5, 2, 6, 3, 7)


def _dot(a, b):
    return jnp.dot(a, b, preferred_element_type=F32)


def _dot_nt(a, b):
    return lax.dot_general(a, b, (((1,), (1,)), ((), ())), preferred_element_type=F32)


def _params(n_grid):
    return pltpu.CompilerParams(dimension_semantics=("arbitrary",) * n_grid, vmem_limit_bytes=VMEM_LIMIT)


def _const_spec(shape):
    return pl.BlockSpec(shape, lambda *_: (0,) * len(shape), pipeline_mode=pl.Buffered(1))


def _split_bf16(a):
    hi = a.astype(BF16)
    return hi, (a - hi.astype(F32)).astype(BF16)


def _ada_kernel(cond_ref, w_ref, b_ref, o_ref):
    c = cond_ref[...]
    a_hi, a_lo = _split_bf16(c * (1.0 / (1.0 + jnp.exp(-c))))
    w_hi, w_lo = _split_bf16(w_ref[0])
    o_ref[0] = _dot(a_hi, w_hi) + _dot(a_lo, w_hi) + _dot(a_hi, w_lo) + b_ref[0]


def _ada(cond, ada_w, ada_b):
    depth, d, n = ada_w.shape
    tn = 1024
    return pl.pallas_call(
        _ada_kernel,
        grid=(depth, n // tn),
        in_specs=[pl.BlockSpec(cond.shape, lambda l, j: (0, 0)),
                  pl.BlockSpec((1, d, tn), lambda l, j: (l, 0, j)),
                  pl.BlockSpec((1, 1, tn), lambda l, j: (l, 0, j))],
        out_specs=pl.BlockSpec((1, cond.shape[0], tn), lambda l, j: (l, 0, j)),
        out_shape=jax.ShapeDtypeStruct((depth, cond.shape[0], n), F32),
        compiler_params=_params(2),
        name="ada_modulation",
    )(cond, ada_w, ada_b.reshape(depth, 1, n))


def _modulated_norm(x, shift, scale):
    ms = jnp.mean(x * x, axis=-1, keepdims=True)
    return (x * lax.rsqrt(ms + EPS)) * (1.0 + scale) + shift


def _group_mean_sq(x, bd):
    hi, lo = _split_bf16(x * x)
    return _dot(hi, bd) + _dot(lo, bd)


def _rope(x, c, sa, sb):
    return x * c + pltpu.roll(x, 32, 1) * sa + pltpu.roll(x, 96, 1) * sb


def _ones_column(rows):
    return (lax.broadcasted_iota(jnp.int32, (rows, LANE), 1) == 0).astype(BF16)


def _store_values(v_ref, p, col0, n_blocks):
    ones = _ones_column(p.shape[0])
    for j in range(n_blocks):
        v_ref[0, :, 2 * LANE * j:2 * LANE * j + LANE] = p[:, col0 + LANE * j:col0 + LANE * (j + 1)].astype(BF16)
        v_ref[0, :, 2 * LANE * j + LANE:2 * LANE * (j + 1)] = ones


def _proj_par_kernel(x_ref, mod_ref, w_ref, c_ref, sa_ref, sb_ref, gain_ref, bd_ref,
                     naq_ref, nak_ref, nav_ref, gq_ref, gk_ref, gv_ref):
    mod = mod_ref[0, 0]
    h = _modulated_norm(x_ref[0], mod[0:1], mod[1:2]).astype(BF16)
    p = _dot(h, w_ref[...])
    naq_ref[0] = p[:, 0:512].astype(BF16)
    nak_ref[0] = p[:, 512:1024].astype(BF16)
    _store_values(nav_ref, p, 1024, 4)
    c, sa, sb, bd = c_ref[...], sa_ref[...], sb_ref[...], bd_ref[...]

    def qk_norm_rope(g, gain):
        g = g * lax.rsqrt(_group_mean_sq(g, bd) + EPS) * gain
        return _rope(g, c, sa, sb).astype(BF16)

    for j in range(4):
        gq_ref[0, :, LANE * j:LANE * (j + 1)] = qk_norm_rope(p[:, 1536 + LANE * j:1536 + LANE * (j + 1)], gain_ref[0:1, :])
    gk_ref[0] = qk_norm_rope(p[:, 2048:2176], gain_ref[1:2, :])
    _store_values(gv_ref, p, 2176, 1)


def _proj_diff_kernel(x_ref, mod_ref, w_ref, c_ref, sa_ref, sb_ref, dq_ref, dk_ref, dv_ref):
    mod = mod_ref[0, 0]
    h = _modulated_norm(x_ref[0], mod[0:1], mod[1:2]).astype(BF16)
    p = _dot(h, w_ref[...])
    c, sa, sb = c_ref[...], sa_ref[...], sb_ref[...]
    for j in range(8):
        dq_ref[0, :, LANE * j:LANE * (j + 1)] = _rope(p[:, LANE * j:LANE * (j + 1)], c, sa, sb).astype(BF16)
        dk_ref[0, :, LANE * j:LANE * (j + 1)] = _rope(p[:, 1024 + LANE * j:1024 + LANE * (j + 1)], c, sa, sb).astype(BF16)
    _store_values(dv_ref, p, 2048, 8)


def _token_spec(width, tile_off=0):
    return pl.BlockSpec((1, TM, width), lambda b, t: (b, t + tile_off, 0))


def _mod_spec(d, latent_only=False):
    if latent_only:
        return pl.BlockSpec((1, 1, N_MOD, d), lambda b, t: (b, 1, 0, 0))
    return pl.BlockSpec((1, 1, N_MOD, d), lambda b, t: (b, jnp.minimum(t, 1), 0, 0))


def _rope_spec():
    return pl.BlockSpec((TM, LANE), lambda b, t: (t, 0))


def _proj_par(xa, mods, w, rope, gains, bd):
    b, t_len, d = xa.shape
    widths = (512, 512, 1024, 512, 128, 256)
    return pl.pallas_call(
        _proj_par_kernel,
        grid=(b, t_len // TM),
        in_specs=[_token_spec(d), _mod_spec(d), _const_spec(w.shape), _rope_spec(), _rope_spec(), _rope_spec(),
                  _const_spec(gains.shape), _const_spec(bd.shape)],
        out_specs=[_token_spec(n) for n in widths],
        out_shape=[jax.ShapeDtypeStruct((b, t_len, n), BF16) for n in widths],
        compiler_params=_params(2),
        name="proj_parallel_mixer",
    )(xa, mods, w, *rope, gains, bd)


def _proj_diff(xa, mods, w, rope):
    b, t_len, d = xa.shape
    widths = (1024, 1024, 2048)
    return pl.pallas_call(
        _proj_diff_kernel,
        grid=(b, t_len // TM),
        in_specs=[_token_spec(d), _mod_spec(d), _const_spec(w.shape), _rope_spec(), _rope_spec(), _rope_spec()],
        out_specs=[_token_spec(n) for n in widths],
        out_shape=[jax.ShapeDtypeStruct((b, t_len, n), BF16) for n in widths],
        compiler_params=_params(2),
        name="proj_diff_mixer",
    )(xa, mods, w, *rope)


def _split_heads(q):
    lane = lax.broadcasted_iota(jnp.int32, q.shape, 1)
    zero = jnp.zeros_like(q)
    return jnp.concatenate([jnp.where(lane < HEAD_DIM, q, zero), jnp.where(lane >= HEAD_DIM, q, zero)], axis=0)


def _flash_chunk(q2, k, v, m_ref, acc_ref):
    s = _dot_nt(q2, k)
    m_prev = m_ref[...]
    m_new = jnp.maximum(m_prev, jnp.max(s, axis=-1, keepdims=True))
    p = jnp.exp(s - m_new)
    acc_ref[...] = jnp.exp(m_prev - m_new) * acc_ref[...] + _dot(p.astype(BF16), v)
    m_ref[...] = m_new


def _normalize(acc):
    return acc[:, 0:LANE] / acc[:, LANE:LANE + 1]


def _merge_heads(o):
    lane = lax.broadcasted_iota(jnp.int32, (TM, LANE), 1)
    return jnp.where(lane < HEAD_DIM, o[0:TM], o[TM:2 * TM])


def _init_flash(m_ref, acc_ref):
    m_ref[...] = jnp.full(m_ref.shape, NEG, F32)
    acc_ref[...] = jnp.zeros(acc_ref.shape, F32)


def _flash_scratch():
    return [pltpu.VMEM((2 * TM, 1), F32), pltpu.VMEM((2 * TM, 2 * LANE), F32)]


def _key_chunk(t_len):
    return next(tk for tk in (768, 640, 512, 384, 256) if t_len % tk == 0)


def _gqa_kernel(q_ref, k_ref, v_ref, o_ref, m_ref, acc_ref, *, tk, n_ctx):
    t = pl.program_id(1)
    q2 = _split_heads(q_ref[0])
    _init_flash(m_ref, acc_ref)

    @pl.when(t == 0)
    def _():
        _flash_chunk(q2, k_ref[0, 0:n_ctx, :], v_ref[0, 0:n_ctx, :], m_ref, acc_ref)

    @pl.when(t > 0)
    def _():
        def body(i, carry):
            off = pl.multiple_of(i * tk, tk)
            _flash_chunk(q2, k_ref[0, pl.ds(off, tk), :], v_ref[0, pl.ds(off, tk), :], m_ref, acc_ref)
            return carry
        lax.fori_loop(0, k_ref.shape[1] // tk, body, 0)

    o_ref[0] = _merge_heads(_normalize(acc_ref[...])).astype(BF16)


def _gqa(gq, gk, gv, n_ctx, tk):
    b, t_len, qw = gq.shape
    return pl.pallas_call(
        functools.partial(_gqa_kernel, tk=tk, n_ctx=n_ctx),
        grid=(b, t_len // TM, qw // LANE),
        in_specs=[pl.BlockSpec((1, TM, LANE), lambda b, t, j: (b, t, j)),
                  pl.BlockSpec((1, t_len, LANE), lambda b, t, j: (b, 0, 0)),
                  pl.BlockSpec((1, t_len, 2 * LANE), lambda b, t, j: (b, 0, 0))],
        out_specs=pl.BlockSpec((1, TM, LANE), lambda b, t, j: (b, t, j)),
        out_shape=jax.ShapeDtypeStruct((b, t_len, qw), BF16),
        scratch_shapes=_flash_scratch(),
        compiler_params=_params(3),
        name="gqa_attention",
    )(gq, gk, gv)


def _na_kernel(q_ref, k_ref, v_ref, bias_ref, o_ref, m_ref, acc_ref, *, n_ctx, rows):
    t = pl.program_id(2)
    q2 = _split_heads(q_ref[0])
    _init_flash(m_ref, acc_ref)
    _flash_chunk(q2, k_ref[0, 0:n_ctx, :], v_ref[0, 0:n_ctx, :], m_ref, acc_ref)

    @pl.when(t > 0)
    def _():
        first_row = NA_ROWS_PER_TILE * (t - 1)
        start = jnp.clip(first_row - WIN_R // 2, 0, rows - NA_KEY_ROWS)
        off = pl.multiple_of(n_ctx + GRID_W * start, GRID_W)
        s = _dot_nt(q2, k_ref[0, pl.ds(off, NA_WIN), :]) + bias_ref[:, 0].reshape(2 * TM, NA_WIN)
        m_prev = m_ref[...]
        m_new = jnp.maximum(m_prev, jnp.max(s, axis=-1, keepdims=True))
        p = jnp.exp(s - m_new)
        acc_ref[...] = jnp.exp(m_prev - m_new) * acc_ref[...] + _dot(p.astype(BF16), v_ref[0, pl.ds(off, NA_WIN), :])

    o_ref[0] = _merge_heads(_normalize(acc_ref[...])).astype(BF16)


def _na(naq, nak, nav, bias, n_ctx):
    b, t_len, qw = naq.shape
    n_tiles = t_len // TM
    rows = (t_len - n_ctx) // GRID_W

    def bias_index(b, j, t):
        return (j, jnp.where(t <= 1, 0, jnp.where(t == n_tiles - 1, 2, 1)), 0, 0)

    return pl.pallas_call(
        functools.partial(_na_kernel, n_ctx=n_ctx, rows=rows),
        grid=(b, qw // LANE, n_tiles),
        in_specs=[pl.BlockSpec((1, TM, LANE), lambda b, j, t: (b, t, j)),
                  pl.BlockSpec((1, t_len, LANE), lambda b, j, t: (b, 0, j)),
                  pl.BlockSpec((1, t_len, 2 * LANE), lambda b, j, t: (b, 0, j)),
                  pl.BlockSpec((2, 1, TM, NA_WIN), bias_index)],
        out_specs=pl.BlockSpec((1, TM, LANE), lambda b, j, t: (b, t, j)),
        out_shape=jax.ShapeDtypeStruct((b, t_len, qw), BF16),
        scratch_shapes=_flash_scratch(),
        compiler_params=_params(3),
        name="neighbourhood_attention",
    )(naq, nak, nav, bias)


def _na_bias_table(rpb, rows):
    g_of_pattern = np.array([0, 2, rows // NA_ROWS_PER_TILE - 1])
    a = np.arange(NA_ROWS_PER_TILE)
    r = NA_ROWS_PER_TILE * g_of_pattern[:, None] + a[None, :]
    start = np.clip(NA_ROWS_PER_TILE * g_of_pattern - WIN_R // 2, 0, rows - NA_KEY_ROWS)
    rs = np.clip(r - WIN_R // 2, 0, rows - WIN_R)
    key_row = start[:, None] + np.arange(NA_KEY_ROWS)[None, :]
    row_ok = (key_row[:, None, :] >= rs[:, :, None]) & (key_row[:, None, :] < rs[:, :, None] + WIN_R)
    row_off = np.clip(key_row[:, None, :] - r[:, :, None] + (WIN_R - 1), 0, 2 * WIN_R - 2)
    cols = np.arange(GRID_W)
    col_start = np.clip(cols - WIN_C // 2, 0, GRID_W - WIN_C)
    col_ok = (cols[None, :] >= col_start[:, None]) & (cols[None, :] < col_start[:, None] + WIN_C)
    col_off = np.clip(cols[None, :] - cols[:, None] + (WIN_C - 1), 0, 2 * WIN_C - 2)
    ok = row_ok[:, :, None, :, None] & col_ok[None, None, :, None, :]
    ro = np.broadcast_to(row_off[:, :, None, :, None], ok.shape)
    co = np.broadcast_to(col_off[None, None, :, None, :], ok.shape)
    vals = rpb.astype(F32)[:, ro, co]
    table = jnp.where(ok[None], vals, NEG)
    return table.reshape(rpb.shape[0], 3, TM, NA_WIN)


def _diff_kernel(lam_ref, gain_ref, q_ref, k_ref, v_ref, o_ref, m_ref, acc_ref, *, tk, lambda_init):
    q2 = _split_heads(q_ref[0])
    _init_flash(m_ref, acc_ref)

    def body(i, carry):
        off = pl.multiple_of(i * tk, tk)
        _flash_chunk(q2, k_ref[0, pl.ds(off, tk), :], v_ref[0, pl.ds(off, tk), :], m_ref, acc_ref)
        return carry
    lax.fori_loop(0, k_ref.shape[1] // tk, body, 0)

    o = _normalize(acc_ref[...])
    lp = lam_ref[...]
    lam = (jnp.exp(jnp.sum(lp[0:1] * lp[1:2], axis=-1, keepdims=True))
           - jnp.exp(jnp.sum(lp[2:3] * lp[3:4], axis=-1, keepdims=True)) + lambda_init)
    d = o[0:TM] - lam * o[TM:2 * TM]
    y = d * lax.rsqrt(jnp.mean(d * d, axis=-1, keepdims=True) + EPS) * gain_ref[...]
    o_ref[0] = (y * (1.0 - lambda_init)).astype(BF16)


def _diff(lam_params, subln_gain, dq, dk, dv, n_ctx, tk, lambda_init):
    b, t_len, qw = dq.shape
    n_heads = qw // LANE
    ctx_tiles = n_ctx // TM
    return pl.pallas_call(
        functools.partial(_diff_kernel, tk=tk, lambda_init=lambda_init),
        grid=(b, n_heads, t_len // TM - ctx_tiles),
        in_specs=[pl.BlockSpec(lam_params.shape, lambda b, h, t: (0, 0)),
                  pl.BlockSpec(subln_gain.shape, lambda b, h, t: (0, 0)),
                  pl.BlockSpec((1, TM, LANE), lambda b, h, t: (b, t + ctx_tiles, h)),
                  pl.BlockSpec((1, t_len, LANE), lambda b, h, t: (b, 0, h)),
                  pl.BlockSpec((1, t_len, 2 * LANE), lambda b, h, t: (b, 0, h))],
        out_specs=pl.BlockSpec((1, TM, LANE), lambda b, h, t: (b, t, h)),
        out_shape=jax.ShapeDtypeStruct((b, t_len - n_ctx, qw), BF16),
        scratch_shapes=_flash_scratch(),
        compiler_params=_params(3),
        name="diff_attention",
    )(lam_params, subln_gain, dq, dk, dv)


def _ffn_kernel(*refs, n_att, ff_chunks, final):
    x_ref, mod_ref = refs[0], refs[1]
    att_refs = refs[2:2 + n_att]
    wo_refs = refs[2 + n_att:2 + 2 * n_att]
    wg_ref, wu_ref, wd_ref = refs[2 + 2 * n_att:5 + 2 * n_att]
    o_ref = refs[-1]
    mod = mod_ref[0, 0]
    y = _dot(att_refs[0][0], wo_refs[0][...])
    for a_ref, w_ref in zip(att_refs[1:], wo_refs[1:]):
        y = y + _dot(a_ref[0], w_ref[...])
    x1 = x_ref[0] + mod[2:3] * y
    h = _modulated_norm(x1, mod[3:4], mod[4:5]).astype(BF16)
    d_ff = wg_ref.shape[1]
    cw = d_ff // ff_chunks
    down = None
    for i in range(ff_chunks):
        g = _dot(h, wg_ref[:, cw * i:cw * (i + 1)])
        u = _dot(h, wu_ref[:, cw * i:cw * (i + 1)])
        a = (g * (1.0 / (1.0 + jnp.exp(-g))) * u).astype(BF16)
        part = _dot(a, wd_ref[cw * i:cw * (i + 1), :])
        down = part if down is None else down + part
    x2 = x1 + mod[5:6] * down
    if final:
        gain_ref = refs[5 + 2 * n_att]
        x2 = x2 * lax.rsqrt(jnp.mean(x2 * x2, axis=-1, keepdims=True) + EPS) * gain_ref[...]
    o_ref[0] = x2


def _ffn(xa, mods, atts, wos, wg, wu, wd, final_gain=None):
    b, t_len, d = xa.shape
    n_rows = atts[0].shape[1]
    tile_off = (t_len - n_rows) // TM
    final = final_gain is not None
    in_specs = ([_token_spec(d, tile_off), _mod_spec(d, latent_only=tile_off > 0)]
                + [_token_spec(a.shape[2]) for a in atts]
                + [_const_spec(w.shape) for w in wos]
                + [_const_spec(wg.shape), _const_spec(wu.shape), _const_spec(wd.shape)])
    args = [xa, mods, *atts, *wos, wg, wu, wd]
    if final:
        in_specs.append(_const_spec(final_gain.shape))
        args.append(final_gain)
    return pl.pallas_call(
        functools.partial(_ffn_kernel, n_att=len(atts), ff_chunks=2, final=final),
        grid=(b, n_rows // TM),
        in_specs=in_specs,
        out_specs=_token_spec(d),
        out_shape=jax.ShapeDtypeStruct((b, n_rows, d), F32),
        compiler_params=_params(2),
        name="outproj_ffn_final" if final else "outproj_ffn",
    )(*args)


def _head_cols(base, heads):
    return np.concatenate([base + HEAD_DIM * h + _DEINT for h in heads])


def _rope_tables(seq, n_ctx):
    t = jnp.arange(seq)
    row = (t // GRID_W).astype(F32)
    col = (t % GRID_W).astype(F32)
    n_freq = HEAD_DIM // 4
    inv = ROPE_THETA ** (-jnp.arange(n_freq, dtype=F32) / n_freq)
    ang = jnp.concatenate([row[:, None] * inv, col[:, None] * inv], axis=-1)
    cos, sin = jnp.cos(ang), jnp.sin(ang)
    zero = jnp.zeros_like(sin)

    def table(first_half, second_half, ctx_value):
        lat = jnp.tile(jnp.concatenate([first_half, second_half], axis=-1), (1, LANE // HEAD_DIM))
        return jnp.concatenate([jnp.full((n_ctx, LANE), ctx_value, F32), lat], axis=0)

    return table(cos, cos, 1.0), table(zero, sin, 0.0), table(-sin, zero, 0.0)


def kernel(x, c, ctx, c_ctx, ada_w, ada_b, ffn_w_gate, ffn_w_up, ffn_w_down, par_w_in, par_w_out, na_rpb,
           gqa_q_gain, gqa_k_gain, diff_w_in, diff_w_out, diff_lambda_q1, diff_lambda_k1, diff_lambda_q2,
           diff_lambda_k2, diff_subln_gain, final_norm_gain):
    b, seq, d = x.shape
    n_ctx = ctx.shape[1]
    assert n_ctx == TM and seq % TM == 0 and d % LANE == 0 and b < 8 and ada_w.shape[0] == 2
    rows = seq // GRID_W
    scale = HEAD_DIM ** -0.5

    xa = jnp.concatenate([ctx, x], axis=1)
    cond = jnp.zeros((8, d), F32).at[:b].set(c).at[b].set(c_ctx)
    mods_all = _ada(cond, ada_w, ada_b).reshape(2, 8, N_MOD, d)

    def mods_of(layer):
        m = mods_all[layer]
        return jnp.stack([jnp.broadcast_to(m[b], (b, N_MOD, d)), m[:b]], axis=1)

    rope = _rope_tables(seq, n_ctx)

    cols0 = np.concatenate([np.arange(0, 1536), _head_cols(1536, _GQA_HEAD_ORDER), _head_cols(2048, (0, 1)),
                            np.arange(2176, 2304)])
    col_scale0 = np.ones((2304,), np.float32)
    col_scale0[0:512] = scale
    w_in0 = (par_w_in[0][:, cols0] * col_scale0).astype(BF16)
    gains = jnp.zeros((8, LANE), F32)
    gains = gains.at[0].set(jnp.tile(gqa_q_gain[0][_DEINT] * scale, 2)).at[1].set(jnp.tile(gqa_k_gain[0][_DEINT], 2))
    block_mean = jnp.asarray(np.kron(np.eye(LANE // HEAD_DIM), np.full((HEAD_DIM, HEAD_DIM), 1.0 / HEAD_DIM)), BF16)
    naq, nak, nav, gq, gk, gv = _proj_par(xa, mods_of(0), w_in0, rope, gains, block_mean)
    att_na = _na(naq, nak, nav, _na_bias_table(na_rpb[0], rows), n_ctx)
    att_g = _gqa(gq, gk, gv, n_ctx, tk=_key_chunk(n_ctx + seq))
    wo_na = par_w_out[0][0:512].astype(BF16)
    wo_g = par_w_out[0][512 + np.concatenate([HEAD_DIM * h + np.arange(HEAD_DIM) for h in _GQA_HEAD_ORDER])].astype(BF16)
    xa = _ffn(xa, mods_of(0), [att_na, att_g], [wo_na, wo_g],
              ffn_w_gate[0].astype(BF16), ffn_w_up[0].astype(BF16), ffn_w_down[0].astype(BF16))

    cols1 = np.concatenate([_head_cols(0, range(16)), _head_cols(1024, range(16)), np.arange(2048, 3072)])
    col_scale1 = np.ones((3072,), np.float32)
    col_scale1[0:1024] = scale
    w_in1 = (diff_w_in[0][:, cols1] * col_scale1).astype(BF16)
    dq, dk, dv = _proj_diff(xa, mods_of(1), w_in1, rope)
    lambda_init = 0.8 - 0.6 * float(np.exp(-0.3 * 1))
    lam_params = jnp.stack([diff_lambda_q1[0], diff_lambda_k1[0], diff_lambda_q2[0], diff_lambda_k2[0]]).astype(F32)
    att_d = _diff(lam_params, diff_subln_gain[0].reshape(1, -1).astype(F32), dq, dk, dv, n_ctx,
                  tk=_key_chunk(n_ctx + seq),
                  lambda_init=lambda_init)
    return _ffn(xa, mods_of(1), [att_d], [diff_w_out[0].astype(BF16)],
                ffn_w_gate[1].astype(BF16), ffn_w_up[1].astype(BF16), ffn_w_down[1].astype(BF16),
                final_gain=final_norm_gain.reshape(1, -1).astype(F32))
```

```python
import functools

import numpy as np
import jax
import jax.numpy as jnp
from jax import lax
from jax.experimental import pallas as pl
from jax.experimental.pallas import tpu as pltpu

F32 = jnp.float32
BF16 = jnp.bfloat16

GRID_W = 64
HEAD_DIM = 64
WIN_R = 8
WIN_C = 16
N_MOD = 6
ROPE_THETA = 10000.0
EPS = 1e-6

LANE = 128
TM = 256
NA_ROWS_PER_TILE = TM // GRID_W
NA_KEY_ROWS = 12
NA_WIN = NA_KEY_ROWS * GRID_W
ROW_BLOCK = 128
DIFF_HEADS_PER_STEP = 4
NEG = -1e30
VMEM_LIMIT = 56 * 1024 * 1024

_DEINT = np.concatenate([np.arange(0, HEAD_DIM, 2), np.arange(1, HEAD_DIM, 2)])
_GQA_HEAD_ORDER = (0, 4, 1, 5, 2, 6, 3, 7)


def _dot(a, b):
    return jnp.dot(a, b, preferred_element_type=F32)


def _dot_nt(a, b):
    return lax.dot_general(a, b, (((1,), (1,)), ((), ())), preferred_element_type=F32)


def _params(n_grid):
    return pltpu.CompilerParams(dimension_semantics=("arbitrary",) * n_grid, vmem_limit_bytes=VMEM_LIMIT)


def _const_spec(shape):
    return pl.BlockSpec(shape, lambda *_: (0,) * len(shape), pipeline_mode=pl.Buffered(1))


def _split_bf16(a):
    hi = a.astype(BF16)
    return hi, (a - hi.astype(F32)).astype(BF16)


def _ada_kernel(cond_ref, w_ref, b_ref, o_ref):
    c = cond_ref[...]
    a_hi, a_lo = _split_bf16(c * (1.0 / (1.0 + jnp.exp(-c))))
    w_hi, w_lo = _split_bf16(w_ref[0])
    o_ref[0] = _dot(a_hi, w_hi) + _dot(a_lo, w_hi) + _dot(a_hi, w_lo) + b_ref[0]


def _ada(cond, ada_w, ada_b):
    depth, d, n = ada_w.shape
    tn = 1024
    return pl.pallas_call(
        _ada_kernel,
        grid=(depth, n // tn),
        in_specs=[pl.BlockSpec(cond.shape, lambda l, j: (0, 0)),
                  pl.BlockSpec((1, d, tn), lambda l, j: (l, 0, j)),
                  pl.BlockSpec((1, 1, tn), lambda l, j: (l, 0, j))],
        out_specs=pl.BlockSpec((1, cond.shape[0], tn), lambda l, j: (l, 0, j)),
        out_shape=jax.ShapeDtypeStruct((depth, cond.shape[0], n), F32),
        compiler_params=_params(2),
        name="ada_modulation",
    )(cond, ada_w, ada_b.reshape(depth, 1, n))


def _modulated_norm(x, shift, scale):
    ms = jnp.mean(x * x, axis=-1, keepdims=True)
    return (x * lax.rsqrt(ms + EPS)) * (1.0 + scale) + shift


def _group_mean_sq(x, bd):
    hi, lo = _split_bf16(x * x)
    return _dot(hi, bd) + _dot(lo, bd)


def _rope(x, c, sa, sb):
    return x * c + pltpu.roll(x, 32, 1) * sa + pltpu.roll(x, 96, 1) * sb


def _ones_column(rows):
    return (lax.broadcasted_iota(jnp.int32, (rows, LANE), 1) == 0).astype(BF16)


def _store_values(v_ref, p, col0, n_blocks):
    ones = _ones_column(p.shape[0])
    for j in range(n_blocks):
        v_ref[0, :, 2 * LANE * j:2 * LANE * j + LANE] = p[:, col0 + LANE * j:col0 + LANE * (j + 1)].astype(BF16)
        v_ref[0, :, 2 * LANE * j + LANE:2 * LANE * (j + 1)] = ones


def _proj_par_kernel(x_ref, mod_ref, w_ref, c_ref, sa_ref, sb_ref, gain_ref, bd_ref,
                     naq_ref, nak_ref, nav_ref, gq_ref, gk_ref, gv_ref):
    mod = mod_ref[0, 0]
    h = _modulated_norm(x_ref[0], mod[0:1], mod[1:2]).astype(BF16)
    p = _dot(h, w_ref[...])
    naq_ref[0] = p[:, 0:512].astype(BF16)
    nak_ref[0] = p[:, 512:1024].astype(BF16)
    _store_values(nav_ref, p, 1024, 4)
    c, sa, sb, bd = c_ref[...], sa_ref[...], sb_ref[...], bd_ref[...]

    def qk_norm_rope(g, gain):
        g = g * lax.rsqrt(_group_mean_sq(g, bd) + EPS) * gain
        return _rope(g, c, sa, sb).astype(BF16)

    for j in range(4):
        gq_ref[0, :, LANE * j:LANE * (j + 1)] = qk_norm_rope(p[:, 1536 + LANE * j:1536 + LANE * (j + 1)], gain_ref[0:1, :])
    gk_ref[0] = qk_norm_rope(p[:, 2048:2176], gain_ref[1:2, :])
    _store_values(gv_ref, p, 2176, 1)


def _proj_diff_kernel(x_ref, mod_ref, w_ref, c_ref, sa_ref, sb_ref, dq_ref, dk_ref, dv_ref):
    mod = mod_ref[0, 0]
    h = _modulated_norm(x_ref[0], mod[0:1], mod[1:2]).astype(BF16)
    p = _dot(h, w_ref[...])
    c, sa, sb = c_ref[...], sa_ref[...], sb_ref[...]
    for j in range(8):
        dq_ref[0, :, LANE * j:LANE * (j + 1)] = _rope(p[:, LANE * j:LANE * (j + 1)], c, sa, sb).astype(BF16)
        dk_ref[0, :, LANE * j:LANE * (j + 1)] = _rope(p[:, 1024 + LANE * j:1024 + LANE * (j + 1)], c, sa, sb).astype(BF16)
    _store_values(dv_ref, p, 2048, 8)


def _token_spec(width, tile_off=0):
    return pl.BlockSpec((1, TM, width), lambda b, t: (b, t + tile_off, 0))


def _mod_spec(d, latent_only=False):
    if latent_only:
        return pl.BlockSpec((1, 1, N_MOD, d), lambda b, t: (b, 1, 0, 0))
    return pl.BlockSpec((1, 1, N_MOD, d), lambda b, t: (b, jnp.minimum(t, 1), 0, 0))


def _rope_spec():
    return pl.BlockSpec((TM, LANE), lambda b, t: (t, 0))


def _proj_par(xa, mods, w, rope, gains, bd):
    b, t_len, d = xa.shape
    widths = (512, 512, 1024, 512, 128, 256)
    return pl.pallas_call(
        _proj_par_kernel,
        grid=(b, t_len // TM),
        in_specs=[_token_spec(d), _mod_spec(d), _const_spec(w.shape), _rope_spec(), _rope_spec(), _rope_spec(),
                  _const_spec(gains.shape), _const_spec(bd.shape)],
        out_specs=[_token_spec(n) for n in widths],
        out_shape=[jax.ShapeDtypeStruct((b, t_len, n), BF16) for n in widths],
        compiler_params=_params(2),
        name="proj_parallel_mixer",
    )(xa, mods, w, *rope, gains, bd)


def _proj_diff(xa, mods, w, rope):
    b, t_len, d = xa.shape
    widths = (1024, 1024, 2048)
    return pl.pallas_call(
        _proj_diff_kernel,
        grid=(b, t_len // TM),
        in_specs=[_token_spec(d), _mod_spec(d), _const_spec(w.shape), _rope_spec(), _rope_spec(), _rope_spec()],
        out_specs=[_token_spec(n) for n in widths],
        out_shape=[jax.ShapeDtypeStruct((b, t_len, n), BF16) for n in widths],
        compiler_params=_params(2),
        name="proj_diff_mixer",
    )(xa, mods, w, *rope)


def _split_heads(q):
    lane = lax.broadcasted_iota(jnp.int32, q.shape, 1)
    zero = jnp.zeros_like(q)
    return jnp.concatenate([jnp.where(lane < HEAD_DIM, q, zero), jnp.where(lane >= HEAD_DIM, q, zero)], axis=0)


def _flash_chunk(q2_ref, k, v, m_ref, acc_ref, row0, bias_of=None):
    for r in range(2 * TM // ROW_BLOCK):
        rows = slice(row0 + ROW_BLOCK * r, row0 + ROW_BLOCK * (r + 1))
        s = _dot_nt(q2_ref[rows], k)
        if bias_of is not None:
            s = s + bias_of(ROW_BLOCK * r, ROW_BLOCK)
        m_prev = m_ref[rows]
        m_new = jnp.maximum(m_prev, jnp.max(s, axis=-1, keepdims=True))
        p = jnp.exp(s - m_new)
        acc_ref[rows] = jnp.exp(m_prev - m_new) * acc_ref[rows] + _dot(p.astype(BF16), v)
        m_ref[rows] = m_new


def _normalize(acc):
    return acc[:, 0:LANE] / acc[:, LANE:LANE + 1]


def _merge_heads(o):
    lane = lax.broadcasted_iota(jnp.int32, (TM, LANE), 1)
    return jnp.where(lane < HEAD_DIM, o[0:TM], o[TM:2 * TM])


def _init_flash(q_ref, q2_ref, m_ref, acc_ref):
    for j in range(q_ref.shape[2] // LANE):
        q2_ref[2 * TM * j:2 * TM * (j + 1)] = _split_heads(q_ref[0, :, LANE * j:LANE * (j + 1)])
    m_ref[...] = jnp.full(m_ref.shape, NEG, F32)
    acc_ref[...] = jnp.zeros(acc_ref.shape, F32)


def _flash_scratch(n_blocks):
    rows = 2 * TM * n_blocks
    return [pltpu.VMEM((rows, LANE), BF16), pltpu.VMEM((rows, 1), F32), pltpu.VMEM((rows, 2 * LANE), F32)]


def _key_chunk(t_len):
    return next(tk for tk in (768, 640, 512, 384, 256) if t_len % tk == 0)


def _gqa_kernel(q_ref, k_ref, v_ref, o_ref, q2_ref, m_ref, acc_ref, *, tk, n_ctx):
    t = pl.program_id(1)
    n_blocks = q_ref.shape[2] // LANE
    _init_flash(q_ref, q2_ref, m_ref, acc_ref)

    def update(k, v):
        for j in range(n_blocks):
            _flash_chunk(q2_ref, k, v, m_ref, acc_ref, 2 * TM * j)

    @pl.when(t == 0)
    def _():
        update(k_ref[0, 0:n_ctx, :], v_ref[0, 0:n_ctx, :])

    @pl.when(t > 0)
    def _():
        def body(i, carry):
            off = pl.multiple_of(i * tk, tk)
            update(k_ref[0, pl.ds(off, tk), :], v_ref[0, pl.ds(off, tk), :])
            return carry
        lax.fori_loop(0, k_ref.shape[1] // tk, body, 0)

    for j in range(n_blocks):
        o = _normalize(acc_ref[2 * TM * j:2 * TM * (j + 1)])
        o_ref[0, :, LANE * j:LANE * (j + 1)] = _merge_heads(o).astype(BF16)


def _gqa(gq, gk, gv, n_ctx, tk):
    b, t_len, qw = gq.shape
    return pl.pallas_call(
        functools.partial(_gqa_kernel, tk=tk, n_ctx=n_ctx),
        grid=(b, t_len // TM),
        in_specs=[pl.BlockSpec((1, TM, qw), lambda b, t: (b, t, 0)),
                  pl.BlockSpec((1, t_len, LANE), lambda b, t: (b, 0, 0)),
                  pl.BlockSpec((1, t_len, 2 * LANE), lambda b, t: (b, 0, 0))],
        out_specs=pl.BlockSpec((1, TM, qw), lambda b, t: (b, t, 0)),
        out_shape=jax.ShapeDtypeStruct((b, t_len, qw), BF16),
        scratch_shapes=_flash_scratch(qw // LANE),
        compiler_params=_params(2),
        name="gqa_attention",
    )(gq, gk, gv)


def _na_kernel(q_ref, k_ref, v_ref, bias_ref, o_ref, q2_ref, m_ref, acc_ref, *, n_ctx, rows):
    t = pl.program_id(2)
    _init_flash(q_ref, q2_ref, m_ref, acc_ref)
    _flash_chunk(q2_ref, k_ref[0, 0:n_ctx, :], v_ref[0, 0:n_ctx, :], m_ref, acc_ref, 0)

    def bias_of(first_row, n_rows):
        head, row = divmod(first_row, TM)
        return bias_ref[head, 0, row:row + n_rows, :]

    @pl.when(t > 0)
    def _():
        first_row = NA_ROWS_PER_TILE * (t - 1)
        start = jnp.clip(first_row - WIN_R // 2, 0, rows - NA_KEY_ROWS)
        off = pl.multiple_of(n_ctx + GRID_W * start, GRID_W)
        _flash_chunk(q2_ref, k_ref[0, pl.ds(off, NA_WIN), :], v_ref[0, pl.ds(off, NA_WIN), :], m_ref, acc_ref, 0, bias_of)

    o_ref[0] = _merge_heads(_normalize(acc_ref[...])).astype(BF16)


def _na(naq, nak, nav, bias, n_ctx):
    b, t_len, qw = naq.shape
    n_tiles = t_len // TM
    rows = (t_len - n_ctx) // GRID_W

    def bias_index(b, j, t):
        return (j, jnp.where(t <= 1, 0, jnp.where(t == n_tiles - 1, 2, 1)), 0, 0)

    return pl.pallas_call(
        functools.partial(_na_kernel, n_ctx=n_ctx, rows=rows),
        grid=(b, qw // LANE, n_tiles),
        in_specs=[pl.BlockSpec((1, TM, LANE), lambda b, j, t: (b, t, j)),
                  pl.BlockSpec((1, t_len, LANE), lambda b, j, t: (b, 0, j)),
                  pl.BlockSpec((1, t_len, 2 * LANE), lambda b, j, t: (b, 0, j)),
                  pl.BlockSpec((2, 1, TM, NA_WIN), bias_index)],
        out_specs=pl.BlockSpec((1, TM, LANE), lambda b, j, t: (b, t, j)),
        out_shape=jax.ShapeDtypeStruct((b, t_len, qw), BF16),
        scratch_shapes=_flash_scratch(1),
        compiler_params=_params(3),
        name="neighbourhood_attention",
    )(naq, nak, nav, bias)


def _na_bias_table(rpb, rows):
    g_of_pattern = np.array([0, 2, rows // NA_ROWS_PER_TILE - 1])
    a = np.arange(NA_ROWS_PER_TILE)
    r = NA_ROWS_PER_TILE * g_of_pattern[:, None] + a[None, :]
    start = np.clip(NA_ROWS_PER_TILE * g_of_pattern - WIN_R // 2, 0, rows - NA_KEY_ROWS)
    rs = np.clip(r - WIN_R // 2, 0, rows - WIN_R)
    key_row = start[:, None] + np.arange(NA_KEY_ROWS)[None, :]
    row_ok = (key_row[:, None, :] >= rs[:, :, None]) & (key_row[:, None, :] < rs[:, :, None] + WIN_R)
    row_off = np.clip(key_row[:, None, :] - r[:, :, None] + (WIN_R - 1), 0, 2 * WIN_R - 2)
    cols = np.arange(GRID_W)
    col_start = np.clip(cols - WIN_C // 2, 0, GRID_W - WIN_C)
    col_ok = (cols[None, :] >= col_start[:, None]) & (cols[None, :] < col_start[:, None] + WIN_C)
    col_off = np.clip(cols[None, :] - cols[:, None] + (WIN_C - 1), 0, 2 * WIN_C - 2)
    ok = row_ok[:, :, None, :, None] & col_ok[None, None, :, None, :]
    bias_rows = rpb.astype(F32)[:, row_off]
    pick_col = (col_off[:, :, None] == np.arange(2 * WIN_C - 1)).astype(np.float32)
    vals = jnp.einsum('hpakb,cjb->hpackj', bias_rows, pick_col, precision=lax.Precision.HIGHEST)
    table = jnp.where(ok[None], vals, NEG)
    return table.reshape(rpb.shape[0], 3, TM, NA_WIN)


def _diff_kernel(lam_ref, gain_ref, q_ref, k_ref, v_ref, o_ref, q2_ref, m_ref, acc_ref, *, tk, lambda_init):
    n_heads = q_ref.shape[2] // LANE
    _init_flash(q_ref, q2_ref, m_ref, acc_ref)

    def body(i, carry):
        off = pl.multiple_of(i * tk, tk)
        for j in range(n_heads):
            _flash_chunk(q2_ref, k_ref[0, pl.ds(off, tk), LANE * j:LANE * (j + 1)],
                         v_ref[0, pl.ds(off, tk), 2 * LANE * j:2 * LANE * (j + 1)], m_ref, acc_ref, 2 * TM * j)
        return carry
    lax.fori_loop(0, k_ref.shape[1] // tk, body, 0)

    lp = lam_ref[...]
    lam = (jnp.exp(jnp.sum(lp[0:1] * lp[1:2], axis=-1, keepdims=True))
           - jnp.exp(jnp.sum(lp[2:3] * lp[3:4], axis=-1, keepdims=True)) + lambda_init)
    for j in range(n_heads):
        o = _normalize(acc_ref[2 * TM * j:2 * TM * (j + 1)])
        d = o[0:TM] - lam * o[TM:2 * TM]
        y = d * lax.rsqrt(jnp.mean(d * d, axis=-1, keepdims=True) + EPS) * gain_ref[...]
        o_ref[0, :, LANE * j:LANE * (j + 1)] = (y * (1.0 - lambda_init)).astype(BF16)


def _diff(lam_params, subln_gain, dq, dk, dv, n_ctx, tk, lambda_init):
    b, t_len, qw = dq.shape
    n_groups = qw // (LANE * DIFF_HEADS_PER_STEP)
    ctx_tiles = n_ctx // TM
    hw = LANE * DIFF_HEADS_PER_STEP
    return pl.pallas_call(
        functools.partial(_diff_kernel, tk=tk, lambda_init=lambda_init),
        grid=(b, n_groups, t_len // TM - ctx_tiles),
        in_specs=[pl.BlockSpec(lam_params.shape, lambda b, h, t: (0, 0)),
                  pl.BlockSpec(subln_gain.shape, lambda b, h, t: (0, 0)),
                  pl.BlockSpec((1, TM, hw), lambda b, h, t: (b, t + ctx_tiles, h)),
                  pl.BlockSpec((1, t_len, hw), lambda b, h, t: (b, 0, h), pipeline_mode=pl.Buffered(1)),
                  pl.BlockSpec((1, t_len, 2 * hw), lambda b, h, t: (b, 0, h), pipeline_mode=pl.Buffered(1))],
        out_specs=pl.BlockSpec((1, TM, hw), lambda b, h, t: (b, t, h)),
        out_shape=jax.ShapeDtypeStruct((b, t_len - n_ctx, qw), BF16),
        scratch_shapes=_flash_scratch(DIFF_HEADS_PER_STEP),
        compiler_params=_params(3),
        name="diff_attention",
    )(lam_params, subln_gain, dq, dk, dv)


def _ffn_kernel(*refs, n_att, ff_chunks, final):
    x_ref, mod_ref = refs[0], refs[1]
    att_refs = refs[2:2 + n_att]
    wo_refs = refs[2 + n_att:2 + 2 * n_att]
    wg_ref, wu_ref, wd_ref = refs[2 + 2 * n_att:5 + 2 * n_att]
    o_ref = refs[-1]
    mod = mod_ref[0, 0]
    y = _dot(att_refs[0][0], wo_refs[0][...])
    for a_ref, w_ref in zip(att_refs[1:], wo_refs[1:]):
        y = y + _dot(a_ref[0], w_ref[...])
    x1 = x_ref[0] + mod[2:3] * y
    h = _modulated_norm(x1, mod[3:4], mod[4:5]).astype(BF16)
    d_ff = wg_ref.shape[1]
    cw = d_ff // ff_chunks
    down = None
    for i in range(ff_chunks):
        g = _dot(h, wg_ref[:, cw * i:cw * (i + 1)])
        u = _dot(h, wu_ref[:, cw * i:cw * (i + 1)])
        a = (g * (1.0 / (1.0 + jnp.exp(-g))) * u).astype(BF16)
        part = _dot(a, wd_ref[cw * i:cw * (i + 1), :])
        down = part if down is None else down + part
    x2 = x1 + mod[5:6] * down
    if final:
        gain_ref = refs[5 + 2 * n_att]
        x2 = x2 * lax.rsqrt(jnp.mean(x2 * x2, axis=-1, keepdims=True) + EPS) * gain_ref[...]
    o_ref[0] = x2


def _ffn(xa, mods, atts, wos, wg, wu, wd, final_gain=None):
    b, t_len, d = xa.shape
    n_rows = atts[0].shape[1]
    tile_off = (t_len - n_rows) // TM
    final = final_gain is not None
    in_specs = ([_token_spec(d, tile_off), _mod_spec(d, latent_only=tile_off > 0)]
                + [_token_spec(a.shape[2]) for a in atts]
                + [_const_spec(w.shape) for w in wos]
                + [_const_spec(wg.shape), _const_spec(wu.shape), _const_spec(wd.shape)])
    args = [xa, mods, *atts, *wos, wg, wu, wd]
    if final:
        in_specs.append(_const_spec(final_gain.shape))
        args.append(final_gain)
    return pl.pallas_call(
        functools.partial(_ffn_kernel, n_att=len(atts), ff_chunks=2, final=final),
        grid=(b, n_rows // TM),
        in_specs=in_specs,
        out_specs=_token_spec(d),
        out_shape=jax.ShapeDtypeStruct((b, n_rows, d), F32),
        compiler_params=_params(2),
        name="outproj_ffn_final" if final else "outproj_ffn",
    )(*args)


def _head_cols(base, heads):
    return np.concatenate([base + HEAD_DIM * h + _DEINT for h in heads])


def _rope_tables(seq, n_ctx):
    t = jnp.arange(seq)
    row = (t // GRID_W).astype(F32)
    col = (t % GRID_W).astype(F32)
    n_freq = HEAD_DIM // 4
    inv = ROPE_THETA ** (-jnp.arange(n_freq, dtype=F32) / n_freq)
    ang = jnp.concatenate([row[:, None] * inv, col[:, None] * inv], axis=-1)
    cos, sin = jnp.cos(ang), jnp.sin(ang)
    zero = jnp.zeros_like(sin)

    def table(first_half, second_half, ctx_value):
        lat = jnp.tile(jnp.concatenate([first_half, second_half], axis=-1), (1, LANE // HEAD_DIM))
        return jnp.concatenate([jnp.full((n_ctx, LANE), ctx_value, F32), lat], axis=0)

    return table(cos, cos, 1.0), table(zero, sin, 0.0), table(-sin, zero, 0.0)


def kernel(x, c, ctx, c_ctx, ada_w, ada_b, ffn_w_gate, ffn_w_up, ffn_w_down, par_w_in, par_w_out, na_rpb,
           gqa_q_gain, gqa_k_gain, diff_w_in, diff_w_out, diff_lambda_q1, diff_lambda_k1, diff_lambda_q2,
           diff_lambda_k2, diff_subln_gain, final_norm_gain):
    b, seq, d = x.shape
    n_ctx = ctx.shape[1]
    assert n_ctx == TM and seq % TM == 0 and d % LANE == 0 and b < 8 and ada_w.shape[0] == 2
    rows = seq // GRID_W
    scale = HEAD_DIM ** -0.5

    xa = jnp.concatenate([ctx, x], axis=1)
    cond = jnp.zeros((8, d), F32).at[:b].set(c).at[b].set(c_ctx)
    mods_all = _ada(cond, ada_w, ada_b).reshape(2, 8, N_MOD, d)

    def mods_of(layer):
        m = mods_all[layer]
        return jnp.stack([jnp.broadcast_to(m[b], (b, N_MOD, d)), m[:b]], axis=1)

    rope = _rope_tables(seq, n_ctx)

    cols0 = np.concatenate([np.arange(0, 1536), _head_cols(1536, _GQA_HEAD_ORDER), _head_cols(2048, (0, 1)),
                            np.arange(2176, 2304)])
    col_scale0 = np.ones((2304,), np.float32)
    col_scale0[0:512] = scale
    w_in0 = (par_w_in[0][:, cols0] * col_scale0).astype(BF16)
    gains = jnp.zeros((8, LANE), F32)
    gains = gains.at[0].set(jnp.tile(gqa_q_gain[0][_DEINT] * scale, 2)).at[1].set(jnp.tile(gqa_k_gain[0][_DEINT], 2))
    block_mean = jnp.asarray(np.kron(np.eye(LANE // HEAD_DIM), np.full((HEAD_DIM, HEAD_DIM), 1.0 / HEAD_DIM)), BF16)
    naq, nak, nav, gq, gk, gv = _proj_par(xa, mods_of(0), w_in0, rope, gains, block_mean)
    att_na = _na(naq, nak, nav, _na_bias_table(na_rpb[0], rows), n_ctx)
    att_g = _gqa(gq, gk, gv, n_ctx, tk=_key_chunk(n_ctx + seq))
    wo_na = par_w_out[0][0:512].astype(BF16)
    wo_g = par_w_out[0][512 + np.concatenate([HEAD_DIM * h + np.arange(HEAD_DIM) for h in _GQA_HEAD_ORDER])].astype(BF16)
    xa = _ffn(xa, mods_of(0), [att_na, att_g], [wo_na, wo_g],
              ffn_w_gate[0].astype(BF16), ffn_w_up[0].astype(BF16), ffn_w_down[0].astype(BF16))

    cols1 = np.concatenate([_head_cols(0, range(16)), _head_cols(1024, range(16)), np.arange(2048, 3072)])
    col_scale1 = np.ones((3072,), np.float32)
    col_scale1[0:1024] = scale
    w_in1 = (diff_w_in[0][:, cols1] * col_scale1).astype(BF16)
    dq, dk, dv = _proj_diff(xa, mods_of(1), w_in1, rope)
    lambda_init = 0.8 - 0.6 * float(np.exp(-0.3 * 1))
    lam_params = jnp.stack([diff_lambda_q1[0], diff_lambda_k1[0], diff_lambda_q2[0], diff_lambda_k2[0]]).astype(F32)
    att_d = _diff(lam_params, diff_subln_gain[0].reshape(1, -1).astype(F32), dq, dk, dv, n_ctx,
                  tk=_key_chunk(n_ctx + seq),
                  lambda_init=lambda_init)
    return _ffn(xa, mods_of(1), [att_d], [diff_w_out[0].astype(BF16)],
                ffn_w_gate[1].astype(BF16), ffn_w_up[1].astype(BF16), ffn_w_down[1].astype(BF16),
                final_gain=final_norm_gain.reshape(1, -1).astype(F32))
```

```python
import functools

import numpy as np
import jax
import jax.numpy as jnp
from jax import lax
from jax.experimental import pallas as pl
from jax.experimental.pallas import tpu as pltpu

F32 = jnp.float32
BF16 = jnp.bfloat16

GRID_W = 64
HEAD_DIM = 64
WIN_R = 8
WIN_C = 16
N_MOD = 6
ROPE_THETA = 10000.0
EPS = 1e-6

LANE = 128
TM = 256
NA_ROWS_PER_TILE = TM // GRID_W
NA_KEY_ROWS = 12
NA_WIN = NA_KEY_ROWS * GRID_W
ROW_BLOCK = 128
DIFF_HEADS_PER_STEP = 4
SOFTMAX_ROWS = 32
LOG2_E = 1.4426950408889634
NEG = -1e30
VMEM_LIMIT = 56 * 1024 * 1024

_DEINT = np.concatenate([np.arange(0, HEAD_DIM, 2), np.arange(1, HEAD_DIM, 2)])
_GQA_HEAD_ORDER = (0, 4, 1, 5, 2, 6, 3, 7)


def _dot(a, b):
    return jnp.dot(a, b, preferred_element_type=F32)


def _dot_nt(a, b):
    return lax.dot_general(a, b, (((1,), (1,)), ((), ())), preferred_element_type=F32)


def _params(n_grid):
    return pltpu.CompilerParams(dimension_semantics=("arbitrary",) * n_grid, vmem_limit_bytes=VMEM_LIMIT)


def _const_spec(shape):
    return pl.BlockSpec(shape, lambda *_: (0,) * len(shape), pipeline_mode=pl.Buffered(1))


def _split_bf16(a):
    hi = a.astype(BF16)
    return hi, (a - hi.astype(F32)).astype(BF16)


def _ada_kernel(cond_ref, w_ref, b_ref, o_ref):
    c = cond_ref[...]
    a_hi, a_lo = _split_bf16(c * (1.0 / (1.0 + jnp.exp(-c))))
    w_hi, w_lo = _split_bf16(w_ref[0])
    o_ref[0] = _dot(a_hi, w_hi) + _dot(a_lo, w_hi) + _dot(a_hi, w_lo) + b_ref[0]


def _ada(cond, ada_w, ada_b):
    depth, d, n = ada_w.shape
    tn = 1024
    return pl.pallas_call(
        _ada_kernel,
        grid=(depth, n // tn),
        in_specs=[pl.BlockSpec(cond.shape, lambda l, j: (0, 0)),
                  pl.BlockSpec((1, d, tn), lambda l, j: (l, 0, j)),
                  pl.BlockSpec((1, 1, tn), lambda l, j: (l, 0, j))],
        out_specs=pl.BlockSpec((1, cond.shape[0], tn), lambda l, j: (l, 0, j)),
        out_shape=jax.ShapeDtypeStruct((depth, cond.shape[0], n), F32),
        compiler_params=_params(2),
        name="ada_modulation",
    )(cond, ada_w, ada_b.reshape(depth, 1, n))


def _modulated_norm(x, shift, scale):
    ms = jnp.mean(x * x, axis=-1, keepdims=True)
    return (x * lax.rsqrt(ms + EPS)) * (1.0 + scale) + shift


def _group_mean_sq(x, bd):
    hi, lo = _split_bf16(x * x)
    return _dot(hi, bd) + _dot(lo, bd)


def _rope(x, c, sa, sb):
    return x * c + pltpu.roll(x, 32, 1) * sa + pltpu.roll(x, 96, 1) * sb


def _ones_column(rows):
    return (lax.broadcasted_iota(jnp.int32, (rows, LANE), 1) == 0).astype(BF16)


def _store_values(v_ref, p, col0, n_blocks):
    ones = _ones_column(p.shape[0])
    for j in range(n_blocks):
        v_ref[0, :, 2 * LANE * j:2 * LANE * j + LANE] = p[:, col0 + LANE * j:col0 + LANE * (j + 1)].astype(BF16)
        v_ref[0, :, 2 * LANE * j + LANE:2 * LANE * (j + 1)] = ones


def _proj_par_kernel(x_ref, mod_ref, w_ref, c_ref, sa_ref, sb_ref, gain_ref, bd_ref,
                     naq_ref, nak_ref, nav_ref, gq_ref, gk_ref, gv_ref):
    mod = mod_ref[0, 0]
    h = _modulated_norm(x_ref[0], mod[0:1], mod[1:2]).astype(BF16)
    p = _dot(h, w_ref[...])
    naq_ref[0] = p[:, 0:512].astype(BF16)
    nak_ref[0] = p[:, 512:1024].astype(BF16)
    _store_values(nav_ref, p, 1024, 4)
    c, sa, sb, bd = c_ref[...], sa_ref[...], sb_ref[...], bd_ref[...]

    def qk_norm_rope(g, gain):
        g = g * lax.rsqrt(_group_mean_sq(g, bd) + EPS) * gain
        return _rope(g, c, sa, sb).astype(BF16)

    for j in range(4):
        gq_ref[0, :, LANE * j:LANE * (j + 1)] = qk_norm_rope(p[:, 1536 + LANE * j:1536 + LANE * (j + 1)], gain_ref[0:1, :])
    gk_ref[0] = qk_norm_rope(p[:, 2048:2176], gain_ref[1:2, :])
    _store_values(gv_ref, p, 2176, 1)


def _proj_diff_kernel(x_ref, mod_ref, w_ref, c_ref, sa_ref, sb_ref, dq_ref, dk_ref, dv_ref):
    mod = mod_ref[0, 0]
    h = _modulated_norm(x_ref[0], mod[0:1], mod[1:2]).astype(BF16)
    p = _dot(h, w_ref[...])
    c, sa, sb = c_ref[...], sa_ref[...], sb_ref[...]
    for j in range(8):
        dq_ref[0, :, LANE * j:LANE * (j + 1)] = _rope(p[:, LANE * j:LANE * (j + 1)], c, sa, sb).astype(BF16)
        dk_ref[0, :, LANE * j:LANE * (j + 1)] = _rope(p[:, 1024 + LANE * j:1024 + LANE * (j + 1)], c, sa, sb).astype(BF16)
    _store_values(dv_ref, p, 2048, 8)


def _token_spec(width, tile_off=0):
    return pl.BlockSpec((1, TM, width), lambda b, t: (b, t + tile_off, 0))


def _mod_spec(d, latent_only=False):
    if latent_only:
        return pl.BlockSpec((1, 1, N_MOD, d), lambda b, t: (b, 1, 0, 0))
    return pl.BlockSpec((1, 1, N_MOD, d), lambda b, t: (b, jnp.minimum(t, 1), 0, 0))


def _rope_spec():
    return pl.BlockSpec((TM, LANE), lambda b, t: (t, 0))


def _proj_par(xa, mods, w, rope, gains, bd):
    b, t_len, d = xa.shape
    widths = (512, 512, 1024, 512, 128, 256)
    return pl.pallas_call(
        _proj_par_kernel,
        grid=(b, t_len // TM),
        in_specs=[_token_spec(d), _mod_spec(d), _const_spec(w.shape), _rope_spec(), _rope_spec(), _rope_spec(),
                  _const_spec(gains.shape), _const_spec(bd.shape)],
        out_specs=[_token_spec(n) for n in widths],
        out_shape=[jax.ShapeDtypeStruct((b, t_len, n), BF16) for n in widths],
        compiler_params=_params(2),
        name="proj_parallel_mixer",
    )(xa, mods, w, *rope, gains, bd)


def _proj_diff(xa, mods, w, rope):
    b, t_len, d = xa.shape
    widths = (1024, 1024, 2048)
    return pl.pallas_call(
        _proj_diff_kernel,
        grid=(b, t_len // TM),
        in_specs=[_token_spec(d), _mod_spec(d), _const_spec(w.shape), _rope_spec(), _rope_spec(), _rope_spec()],
        out_specs=[_token_spec(n) for n in widths],
        out_shape=[jax.ShapeDtypeStruct((b, t_len, n), BF16) for n in widths],
        compiler_params=_params(2),
        name="proj_diff_mixer",
    )(xa, mods, w, *rope)


def _split_heads(q):
    lane = lax.broadcasted_iota(jnp.int32, q.shape, 1)
    zero = jnp.zeros_like(q)
    return jnp.concatenate([jnp.where(lane < HEAD_DIM, q, zero), jnp.where(lane >= HEAD_DIM, q, zero)], axis=0)


def _softmax_update(s, m_prev):
    m_new = jnp.maximum(m_prev, jnp.max(s, axis=-1, keepdims=True))
    p = jnp.exp2(s - m_new)
    return p, jnp.exp2(m_prev - m_new), m_new


def _flash_chunk(q2_ref, k, v, m_ref, acc_ref, row0, bias_of=None):
    for r in range(2 * TM // ROW_BLOCK):
        rows = slice(row0 + ROW_BLOCK * r, row0 + ROW_BLOCK * (r + 1))
        s = _dot_nt(q2_ref[rows], k)
        if bias_of is not None:
            s = s + bias_of(ROW_BLOCK * r, ROW_BLOCK)
        p, alpha, m_new = _softmax_update(s, m_ref[rows])
        acc_ref[rows] = alpha * acc_ref[rows] + _dot(p.astype(BF16), v)
        m_ref[rows] = m_new


def _flash_pipeline(q2_ref, k_of, v_of, n_chunks, n_blocks, m_ref, acc_ref, s_refs, p_refs, alpha_refs):
    assert n_blocks % 2 == 0
    r2 = 2 * TM

    def score(i, j, slot):
        s_refs[slot][...] = _dot_nt(q2_ref[r2 * j:r2 * (j + 1)], k_of(i, j))

    def softmax(j, slot):
        for r in range(r2 // SOFTMAX_ROWS):
            rows = slice(SOFTMAX_ROWS * r, SOFTMAX_ROWS * (r + 1))
            state_rows = slice(r2 * j + SOFTMAX_ROWS * r, r2 * j + SOFTMAX_ROWS * (r + 1))
            p, alpha, m_new = _softmax_update(s_refs[slot][rows], m_ref[state_rows])
            p_refs[slot][rows] = p.astype(BF16)
            alpha_refs[slot][rows] = alpha
            m_ref[state_rows] = m_new

    def accumulate(i, j, slot):
        blk = slice(r2 * j, r2 * (j + 1))
        acc_ref[blk] = alpha_refs[slot][...] * acc_ref[blk] + _dot(p_refs[slot][...], v_of(i, j))

    last_slot = (n_blocks - 1) % 2
    p_refs[last_slot][...] = jnp.zeros(p_refs[last_slot].shape, BF16)
    alpha_refs[last_slot][...] = jnp.ones(alpha_refs[last_slot].shape, F32)
    score(0, 0, 0)

    def body(i, carry):
        for j in range(n_blocks):
            if j + 1 < n_blocks:
                score(i, j + 1, (j + 1) % 2)
            else:
                score(jnp.minimum(i + 1, n_chunks - 1), 0, 0)
            softmax(j, j % 2)
            if j > 0:
                accumulate(i, j - 1, (j - 1) % 2)
            else:
                accumulate(jnp.maximum(i - 1, 0), n_blocks - 1, last_slot)
        return carry

    lax.fori_loop(0, n_chunks, body, 0)
    accumulate(n_chunks - 1, n_blocks - 1, last_slot)


def _pipeline_scratch(tk):
    rows = 2 * TM
    return [pltpu.VMEM((rows, tk), F32)] * 2 + [pltpu.VMEM((rows, tk), BF16)] * 2 + [pltpu.VMEM((rows, 1), F32)] * 2


def _normalize(acc):
    return acc[:, 0:LANE] / acc[:, LANE:LANE + 1]


def _merge_heads(o):
    lane = lax.broadcasted_iota(jnp.int32, (TM, LANE), 1)
    return jnp.where(lane < HEAD_DIM, o[0:TM], o[TM:2 * TM])


def _init_flash(q_ref, q2_ref, m_ref, acc_ref):
    for j in range(q_ref.shape[2] // LANE):
        q2_ref[2 * TM * j:2 * TM * (j + 1)] = _split_heads(q_ref[0, :, LANE * j:LANE * (j + 1)])
    m_ref[...] = jnp.full(m_ref.shape, NEG, F32)
    acc_ref[...] = jnp.zeros(acc_ref.shape, F32)


def _flash_scratch(n_blocks):
    rows = 2 * TM * n_blocks
    return [pltpu.VMEM((rows, LANE), BF16), pltpu.VMEM((rows, 1), F32), pltpu.VMEM((rows, 2 * LANE), F32)]


def _key_chunk(t_len):
    return next(tk for tk in (768, 640, 512, 384, 256) if t_len % tk == 0)


def _gqa_kernel(q_ref, k_ref, v_ref, o_ref, q2_ref, m_ref, acc_ref, *pipe_refs, tk, n_ctx):
    t = pl.program_id(1)
    n_blocks = q_ref.shape[2] // LANE
    _init_flash(q_ref, q2_ref, m_ref, acc_ref)

    @pl.when(t == 0)
    def _():
        for j in range(n_blocks):
            _flash_chunk(q2_ref, k_ref[0, 0:n_ctx, :], v_ref[0, 0:n_ctx, :], m_ref, acc_ref, 2 * TM * j)

    @pl.when(t > 0)
    def _():
        def chunk_of(ref):
            return lambda i, j: ref[0, pl.ds(pl.multiple_of(i * tk, tk), tk), :]
        _flash_pipeline(q2_ref, chunk_of(k_ref), chunk_of(v_ref), k_ref.shape[1] // tk, n_blocks, m_ref, acc_ref,
                        pipe_refs[0:2], pipe_refs[2:4], pipe_refs[4:6])

    for j in range(n_blocks):
        o = _normalize(acc_ref[2 * TM * j:2 * TM * (j + 1)])
        o_ref[0, :, LANE * j:LANE * (j + 1)] = _merge_heads(o).astype(BF16)


def _gqa(gq, gk, gv, n_ctx, tk):
    b, t_len, qw = gq.shape
    return pl.pallas_call(
        functools.partial(_gqa_kernel, tk=tk, n_ctx=n_ctx),
        grid=(b, t_len // TM),
        in_specs=[pl.BlockSpec((1, TM, qw), lambda b, t: (b, t, 0)),
                  pl.BlockSpec((1, t_len, LANE), lambda b, t: (b, 0, 0)),
                  pl.BlockSpec((1, t_len, 2 * LANE), lambda b, t: (b, 0, 0))],
        out_specs=pl.BlockSpec((1, TM, qw), lambda b, t: (b, t, 0)),
        out_shape=jax.ShapeDtypeStruct((b, t_len, qw), BF16),
        scratch_shapes=_flash_scratch(qw // LANE) + _pipeline_scratch(tk),
        compiler_params=_params(2),
        name="gqa_attention",
    )(gq, gk, gv)


def _na_kernel(q_ref, k_ref, v_ref, bias_ref, o_ref, q2_ref, m_ref, acc_ref, *pipe_refs, n_ctx, rows):
    t = pl.program_id(1)
    n_blocks = q_ref.shape[2] // LANE
    r2 = 2 * TM
    s_refs, p_refs = pipe_refs[0:2], pipe_refs[2:4]
    _init_flash(q_ref, q2_ref, m_ref, acc_ref)

    @pl.when(t == 0)
    def _():
        for j in range(n_blocks):
            _flash_chunk(q2_ref, k_ref[0, 0:n_ctx, LANE * j:LANE * (j + 1)],
                         v_ref[0, 0:n_ctx, 2 * LANE * j:2 * LANE * (j + 1)], m_ref, acc_ref, r2 * j)
            o_ref[0, :, LANE * j:LANE * (j + 1)] = _merge_heads(_normalize(acc_ref[r2 * j:r2 * (j + 1)])).astype(BF16)

    @pl.when(t > 0)
    def _():
        first_row = NA_ROWS_PER_TILE * (t - 1)
        start = jnp.clip(first_row - WIN_R // 2, 0, rows - NA_KEY_ROWS)
        off = pl.multiple_of(n_ctx + GRID_W * start, GRID_W)

        def score(j, slot):
            q2 = q2_ref[r2 * j:r2 * (j + 1)]
            s_refs[slot][:, 0:n_ctx] = _dot_nt(q2, k_ref[0, 0:n_ctx, LANE * j:LANE * (j + 1)])
            s_refs[slot][:, n_ctx:] = _dot_nt(q2, k_ref[0, pl.ds(off, NA_WIN), LANE * j:LANE * (j + 1)])

        def softmax(j, slot):
            for r in range(r2 // SOFTMAX_ROWS):
                rows_r = slice(SOFTMAX_ROWS * r, SOFTMAX_ROWS * (r + 1))
                head, row = divmod(SOFTMAX_ROWS * r, TM)
                s_ctx = s_refs[slot][rows_r, 0:n_ctx]
                s_win = s_refs[slot][rows_r, n_ctx:] + bias_ref[2 * j + head, 0, row:row + SOFTMAX_ROWS, :]
                m = jnp.maximum(jnp.max(s_ctx, axis=-1, keepdims=True), jnp.max(s_win, axis=-1, keepdims=True))
                p_refs[slot][rows_r, 0:n_ctx] = jnp.exp2(s_ctx - m).astype(BF16)
                p_refs[slot][rows_r, n_ctx:] = jnp.exp2(s_win - m).astype(BF16)

        def output(j, slot):
            acc = (_dot(p_refs[slot][:, 0:n_ctx], v_ref[0, 0:n_ctx, 2 * LANE * j:2 * LANE * (j + 1)])
                   + _dot(p_refs[slot][:, n_ctx:], v_ref[0, pl.ds(off, NA_WIN), 2 * LANE * j:2 * LANE * (j + 1)]))
            o_ref[0, :, LANE * j:LANE * (j + 1)] = _merge_heads(_normalize(acc)).astype(BF16)

        score(0, 0)
        for j in range(n_blocks):
            if j + 1 < n_blocks:
                score(j + 1, (j + 1) % 2)
            softmax(j, j % 2)
            if j > 0:
                output(j - 1, (j - 1) % 2)
        output(n_blocks - 1, (n_blocks - 1) % 2)


def _na(naq, nak, nav, bias, n_ctx):
    b, t_len, qw = naq.shape
    n_tiles = t_len // TM
    rows = (t_len - n_ctx) // GRID_W
    n_keys = n_ctx + NA_WIN

    def bias_index(b, t):
        return (0, jnp.where(t <= 1, 0, jnp.where(t == n_tiles - 1, 2, 1)), 0, 0)

    def resident(width):
        return pl.BlockSpec((1, t_len, width), lambda b, t: (b, 0, 0), pipeline_mode=pl.Buffered(1))

    return pl.pallas_call(
        functools.partial(_na_kernel, n_ctx=n_ctx, rows=rows),
        grid=(b, n_tiles),
        in_specs=[pl.BlockSpec((1, TM, qw), lambda b, t: (b, t, 0)), resident(qw), resident(2 * qw),
                  pl.BlockSpec((bias.shape[0], 1, TM, NA_WIN), bias_index, pipeline_mode=pl.Buffered(1))],
        out_specs=pl.BlockSpec((1, TM, qw), lambda b, t: (b, t, 0)),
        out_shape=jax.ShapeDtypeStruct((b, t_len, qw), BF16),
        scratch_shapes=(_flash_scratch(qw // LANE)
                        + [pltpu.VMEM((2 * TM, n_keys), F32)] * 2 + [pltpu.VMEM((2 * TM, n_keys), BF16)] * 2),
        compiler_params=_params(2),
        name="neighbourhood_attention",
    )(naq, nak, nav, bias)


def _na_bias_table(rpb, rows):
    g_of_pattern = np.array([0, 2, rows // NA_ROWS_PER_TILE - 1])
    a = np.arange(NA_ROWS_PER_TILE)
    r = NA_ROWS_PER_TILE * g_of_pattern[:, None] + a[None, :]
    start = np.clip(NA_ROWS_PER_TILE * g_of_pattern - WIN_R // 2, 0, rows - NA_KEY_ROWS)
    rs = np.clip(r - WIN_R // 2, 0, rows - WIN_R)
    key_row = start[:, None] + np.arange(NA_KEY_ROWS)[None, :]
    row_ok = (key_row[:, None, :] >= rs[:, :, None]) & (key_row[:, None, :] < rs[:, :, None] + WIN_R)
    row_off = np.clip(key_row[:, None, :] - r[:, :, None] + (WIN_R - 1), 0, 2 * WIN_R - 2)
    cols = np.arange(GRID_W)
    col_start = np.clip(cols - WIN_C // 2, 0, GRID_W - WIN_C)
    col_ok = (cols[None, :] >= col_start[:, None]) & (cols[None, :] < col_start[:, None] + WIN_C)
    col_off = np.clip(cols[None, :] - cols[:, None] + (WIN_C - 1), 0, 2 * WIN_C - 2)
    ok = row_ok[:, :, None, :, None] & col_ok[None, None, :, None, :]
    bias_rows = rpb.astype(F32)[:, row_off]
    pick_col = (col_off[:, :, None] == np.arange(2 * WIN_C - 1)).astype(np.float32)
    vals = jnp.einsum('hpakb,cjb->hpackj', bias_rows, pick_col, precision=lax.Precision.HIGHEST)
    table = jnp.where(ok[None], vals * LOG2_E, NEG)
    return table.reshape(rpb.shape[0], 3, TM, NA_WIN)


def _diff_kernel(lam_ref, gain_ref, q_ref, k_ref, v_ref, o_ref, q2_ref, m_ref, acc_ref, *pipe_refs, tk, lambda_init):
    n_heads = q_ref.shape[2] // LANE
    _init_flash(q_ref, q2_ref, m_ref, acc_ref)

    def chunk_of(ref, width):
        return lambda i, j: ref[0, pl.ds(pl.multiple_of(i * tk, tk), tk), width * j:width * (j + 1)]
    _flash_pipeline(q2_ref, chunk_of(k_ref, LANE), chunk_of(v_ref, 2 * LANE), k_ref.shape[1] // tk, n_heads,
                    m_ref, acc_ref, pipe_refs[0:2], pipe_refs[2:4], pipe_refs[4:6])

    lp = lam_ref[...]
    lam = (jnp.exp(jnp.sum(lp[0:1] * lp[1:2], axis=-1, keepdims=True))
           - jnp.exp(jnp.sum(lp[2:3] * lp[3:4], axis=-1, keepdims=True)) + lambda_init)
    for j in range(n_heads):
        o = _normalize(acc_ref[2 * TM * j:2 * TM * (j + 1)])
        d = o[0:TM] - lam * o[TM:2 * TM]
        y = d * lax.rsqrt(jnp.mean(d * d, axis=-1, keepdims=True) + EPS) * gain_ref[...]
        o_ref[0, :, LANE * j:LANE * (j + 1)] = (y * (1.0 - lambda_init)).astype(BF16)


def _diff(lam_params, subln_gain, dq, dk, dv, n_ctx, tk, lambda_init):
    b, t_len, qw = dq.shape
    n_groups = qw // (LANE * DIFF_HEADS_PER_STEP)
    ctx_tiles = n_ctx // TM
    hw = LANE * DIFF_HEADS_PER_STEP
    return pl.pallas_call(
        functools.partial(_diff_kernel, tk=tk, lambda_init=lambda_init),
        grid=(b, n_groups, t_len // TM - ctx_tiles),
        in_specs=[pl.BlockSpec(lam_params.shape, lambda b, h, t: (0, 0)),
                  pl.BlockSpec(subln_gain.shape, lambda b, h, t: (0, 0)),
                  pl.BlockSpec((1, TM, hw), lambda b, h, t: (b, t + ctx_tiles, h)),
                  pl.BlockSpec((1, t_len, hw), lambda b, h, t: (b, 0, h), pipeline_mode=pl.Buffered(1)),
                  pl.BlockSpec((1, t_len, 2 * hw), lambda b, h, t: (b, 0, h), pipeline_mode=pl.Buffered(1))],
        out_specs=pl.BlockSpec((1, TM, hw), lambda b, h, t: (b, t, h)),
        out_shape=jax.ShapeDtypeStruct((b, t_len - n_ctx, qw), BF16),
        scratch_shapes=_flash_scratch(DIFF_HEADS_PER_STEP) + _pipeline_scratch(tk),
        compiler_params=_params(3),
        name="diff_attention",
    )(lam_params, subln_gain, dq, dk, dv)


def _ffn_kernel(*refs, n_att, ff_chunks, final):
    x_ref, mod_ref = refs[0], refs[1]
    att_refs = refs[2:2 + n_att]
    wo_refs = refs[2 + n_att:2 + 2 * n_att]
    wg_ref, wu_ref, wd_ref = refs[2 + 2 * n_att:5 + 2 * n_att]
    o_ref = refs[-1]
    mod = mod_ref[0, 0]
    y = _dot(att_refs[0][0], wo_refs[0][...])
    for a_ref, w_ref in zip(att_refs[1:], wo_refs[1:]):
        y = y + _dot(a_ref[0], w_ref[...])
    x1 = x_ref[0] + mod[2:3] * y
    h = _modulated_norm(x1, mod[3:4], mod[4:5]).astype(BF16)
    d_ff = wg_ref.shape[1]
    cw = d_ff // ff_chunks
    down = None
    for i in range(ff_chunks):
        g = _dot(h, wg_ref[:, cw * i:cw * (i + 1)])
        u = _dot(h, wu_ref[:, cw * i:cw * (i + 1)])
        a = (g * (1.0 / (1.0 + jnp.exp(-g))) * u).astype(BF16)
        part = _dot(a, wd_ref[cw * i:cw * (i + 1), :])
        down = part if down is None else down + part
    x2 = x1 + mod[5:6] * down
    if final:
        gain_ref = refs[5 + 2 * n_att]
        x2 = x2 * lax.rsqrt(jnp.mean(x2 * x2, axis=-1, keepdims=True) + EPS) * gain_ref[...]
    o_ref[0] = x2


def _ffn(xa, mods, atts, wos, wg, wu, wd, final_gain=None):
    b, t_len, d = xa.shape
    n_rows = atts[0].shape[1]
    tile_off = (t_len - n_rows) // TM
    final = final_gain is not None
    in_specs = ([_token_spec(d, tile_off), _mod_spec(d, latent_only=tile_off > 0)]
                + [_token_spec(a.shape[2]) for a in atts]
                + [_const_spec(w.shape) for w in wos]
                + [_const_spec(wg.shape), _const_spec(wu.shape), _const_spec(wd.shape)])
    args = [xa, mods, *atts, *wos, wg, wu, wd]
    if final:
        in_specs.append(_const_spec(final_gain.shape))
        args.append(final_gain)
    return pl.pallas_call(
        functools.partial(_ffn_kernel, n_att=len(atts), ff_chunks=2, final=final),
        grid=(b, n_rows // TM),
        in_specs=in_specs,
        out_specs=_token_spec(d),
        out_shape=jax.ShapeDtypeStruct((b, n_rows, d), F32),
        compiler_params=_params(2),
        name="outproj_ffn_final" if final else "outproj_ffn",
    )(*args)


def _head_cols(base, heads):
    return np.concatenate([base + HEAD_DIM * h + _DEINT for h in heads])


def _rope_tables(seq, n_ctx):
    t = jnp.arange(seq)
    row = (t // GRID_W).astype(F32)
    col = (t % GRID_W).astype(F32)
    n_freq = HEAD_DIM // 4
    inv = ROPE_THETA ** (-jnp.arange(n_freq, dtype=F32) / n_freq)
    ang = jnp.concatenate([row[:, None] * inv, col[:, None] * inv], axis=-1)
    cos, sin = jnp.cos(ang), jnp.sin(ang)
    zero = jnp.zeros_like(sin)

    def table(first_half, second_half, ctx_value):
        lat = jnp.tile(jnp.concatenate([first_half, second_half], axis=-1), (1, LANE // HEAD_DIM))
        return jnp.concatenate([jnp.full((n_ctx, LANE), ctx_value, F32), lat], axis=0)

    return table(cos, cos, 1.0), table(zero, sin, 0.0), table(-sin, zero, 0.0)


def kernel(x, c, ctx, c_ctx, ada_w, ada_b, ffn_w_gate, ffn_w_up, ffn_w_down, par_w_in, par_w_out, na_rpb,
           gqa_q_gain, gqa_k_gain, diff_w_in, diff_w_out, diff_lambda_q1, diff_lambda_k1, diff_lambda_q2,
           diff_lambda_k2, diff_subln_gain, final_norm_gain):
    b, seq, d = x.shape
    n_ctx = ctx.shape[1]
    assert n_ctx == TM and seq % TM == 0 and d % LANE == 0 and b < 8 and ada_w.shape[0] == 2
    rows = seq // GRID_W
    scale = HEAD_DIM ** -0.5 * LOG2_E

    xa = jnp.concatenate([ctx, x], axis=1)
    cond = jnp.zeros((8, d), F32).at[:b].set(c).at[b].set(c_ctx)
    mods_all = _ada(cond, ada_w, ada_b).reshape(2, 8, N_MOD, d)

    def mods_of(layer):
        m = mods_all[layer]
        return jnp.stack([jnp.broadcast_to(m[b], (b, N_MOD, d)), m[:b]], axis=1)

    rope = _rope_tables(seq, n_ctx)

    cols0 = np.concatenate([np.arange(0, 1536), _head_cols(1536, _GQA_HEAD_ORDER), _head_cols(2048, (0, 1)),
                            np.arange(2176, 2304)])
    col_scale0 = np.ones((2304,), np.float32)
    col_scale0[0:512] = scale
    w_in0 = (par_w_in[0][:, cols0] * col_scale0).astype(BF16)
    gains = jnp.zeros((8, LANE), F32)
    gains = gains.at[0].set(jnp.tile(gqa_q_gain[0][_DEINT] * scale, 2)).at[1].set(jnp.tile(gqa_k_gain[0][_DEINT], 2))
    block_mean = jnp.asarray(np.kron(np.eye(LANE // HEAD_DIM), np.full((HEAD_DIM, HEAD_DIM), 1.0 / HEAD_DIM)), BF16)
    naq, nak, nav, gq, gk, gv = _proj_par(xa, mods_of(0), w_in0, rope, gains, block_mean)
    att_na = _na(naq, nak, nav, _na_bias_table(na_rpb[0], rows), n_ctx)
    att_g = _gqa(gq, gk, gv, n_ctx, tk=_key_chunk(n_ctx + seq))
    wo_na = par_w_out[0][0:512].astype(BF16)
    wo_g = par_w_out[0][512 + np.concatenate([HEAD_DIM * h + np.arange(HEAD_DIM) for h in _GQA_HEAD_ORDER])].astype(BF16)
    xa = _ffn(xa, mods_of(0), [att_na, att_g], [wo_na, wo_g],
              ffn_w_gate[0].astype(BF16), ffn_w_up[0].astype(BF16), ffn_w_down[0].astype(BF16))

    cols1 = np.concatenate([_head_cols(0, range(16)), _head_cols(1024, range(16)), np.arange(2048, 3072)])
    col_scale1 = np.ones((3072,), np.float32)
    col_scale1[0:1024] = scale
    w_in1 = (diff_w_in[0][:, cols1] * col_scale1).astype(BF16)
    dq, dk, dv = _proj_diff(xa, mods_of(1), w_in1, rope)
    lambda_init = 0.8 - 0.6 * float(np.exp(-0.3 * 1))
    lam_params = jnp.stack([diff_lambda_q1[0], diff_lambda_k1[0], diff_lambda_q2[0], diff_lambda_k2[0]]).astype(F32)
    att_d = _diff(lam_params, diff_subln_gain[0].reshape(1, -1).astype(F32), dq, dk, dv, n_ctx,
                  tk=_key_chunk(n_ctx + seq),
                  lambda_init=lambda_init)
    return _ffn(xa, mods_of(1), [att_d], [diff_w_out[0].astype(BF16)],
                ffn_w_gate[1].astype(BF16), ffn_w_up[1].astype(BF16), ffn_w_down[1].astype(BF16),
                final_gain=final_norm_gain.reshape(1, -1).astype(F32))
```

```python
import functools

import numpy as np
import jax
import jax.numpy as jnp
from jax import lax
from jax.experimental import pallas as pl
from jax.experimental.pallas import tpu as pltpu

F32 = jnp.float32
BF16 = jnp.bfloat16

GRID_W = 64
HEAD_DIM = 64
WIN_R = 8
WIN_C = 16
N_MOD = 6
ROPE_THETA = 10000.0
EPS = 1e-6

LANE = 128
TM = 256
NA_ROWS_PER_TILE = TM // GRID_W
NA_KEY_ROWS = 12
NA_WIN = NA_KEY_ROWS * GRID_W
ROW_BLOCK = 128
DIFF_HEADS_PER_STEP = 4
Q_TILES_PER_STEP = 2
SOFTMAX_ROWS = 32
LOG2_E = 1.4426950408889634
MXU_DEPTH = 256
NEG = -1e30
VMEM_LIMIT = 56 * 1024 * 1024

_DEINT = np.concatenate([np.arange(0, HEAD_DIM, 2), np.arange(1, HEAD_DIM, 2)])
_GQA_HEAD_ORDER = (0, 4, 1, 5, 2, 6, 3, 7)


def _dot(a, b):
    return jnp.dot(a, b, preferred_element_type=F32)


def _dot_nt(a, b):
    return lax.dot_general(a, b, (((1,), (1,)), ((), ())), preferred_element_type=F32)


def _params(n_grid):
    return pltpu.CompilerParams(dimension_semantics=("arbitrary",) * n_grid, vmem_limit_bytes=VMEM_LIMIT)


def _const_spec(shape):
    return pl.BlockSpec(shape, lambda *_: (0,) * len(shape), pipeline_mode=pl.Buffered(1))


def _split_bf16(a):
    hi = a.astype(BF16)
    return hi, (a - hi.astype(F32)).astype(BF16)


def _ada_kernel(cond_ref, w_ref, b_ref, o_ref):
    c = cond_ref[...]
    a_hi, a_lo = _split_bf16(c * (1.0 / (1.0 + jnp.exp(-c))))
    w_hi, w_lo = _split_bf16(w_ref[0])
    o_ref[0] = _dot(a_hi, w_hi) + _dot(a_lo, w_hi) + _dot(a_hi, w_lo) + b_ref[0]


def _ada(cond, ada_w, ada_b):
    depth, d, n = ada_w.shape
    tn = 1024
    return pl.pallas_call(
        _ada_kernel,
        grid=(depth, n // tn),
        in_specs=[pl.BlockSpec(cond.shape, lambda l, j: (0, 0)),
                  pl.BlockSpec((1, d, tn), lambda l, j: (l, 0, j)),
                  pl.BlockSpec((1, 1, tn), lambda l, j: (l, 0, j))],
        out_specs=pl.BlockSpec((1, cond.shape[0], tn), lambda l, j: (l, 0, j)),
        out_shape=jax.ShapeDtypeStruct((depth, cond.shape[0], n), F32),
        compiler_params=_params(2),
        name="ada_modulation",
    )(cond, ada_w, ada_b.reshape(depth, 1, n))


def _modulated_norm(x, shift, scale):
    ms = jnp.mean(x * x, axis=-1, keepdims=True)
    return (x * lax.rsqrt(ms + EPS)) * (1.0 + scale) + shift


def _group_mean_sq(x, bd):
    hi, lo = _split_bf16(x * x)
    return _dot(hi, bd) + _dot(lo, bd)


def _rope(x, c, sa, sb):
    return x * c + pltpu.roll(x, 32, 1) * sa + pltpu.roll(x, 96, 1) * sb


def _ones_column(rows):
    return (lax.broadcasted_iota(jnp.int32, (rows, LANE), 1) == 0).astype(BF16)


def _store_values(v_ref, p, col0, n_blocks):
    ones = _ones_column(p.shape[0])
    for j in range(n_blocks):
        v_ref[0, :, 2 * LANE * j:2 * LANE * j + LANE] = p[:, col0 + LANE * j:col0 + LANE * (j + 1)].astype(BF16)
        v_ref[0, :, 2 * LANE * j + LANE:2 * LANE * (j + 1)] = ones


def _proj_par_kernel(x_ref, mod_ref, w_ref, c_ref, sa_ref, sb_ref, gain_ref, bd_ref,
                     naq_ref, nak_ref, nav_ref, gq_ref, gk_ref, gv_ref):
    mod = mod_ref[0, 0]
    h = _modulated_norm(x_ref[0], mod[0:1], mod[1:2]).astype(BF16)
    p = _dot(h, w_ref[...])
    naq_ref[0] = p[:, 0:512].astype(BF16)
    nak_ref[0] = p[:, 512:1024].astype(BF16)
    _store_values(nav_ref, p, 1024, 4)
    c, sa, sb, bd = c_ref[...], sa_ref[...], sb_ref[...], bd_ref[...]

    def qk_norm_rope(g, gain):
        g = g * lax.rsqrt(_group_mean_sq(g, bd) + EPS) * gain
        return _rope(g, c, sa, sb).astype(BF16)

    for j in range(4):
        gq_ref[0, :, LANE * j:LANE * (j + 1)] = qk_norm_rope(p[:, 1536 + LANE * j:1536 + LANE * (j + 1)], gain_ref[0:1, :])
    gk_ref[0] = qk_norm_rope(p[:, 2048:2176], gain_ref[1:2, :])
    _store_values(gv_ref, p, 2176, 1)


def _proj_diff_kernel(x_ref, mod_ref, w_ref, c_ref, sa_ref, sb_ref, dq_ref, dk_ref, dv_ref):
    mod = mod_ref[0, 0]
    h = _modulated_norm(x_ref[0], mod[0:1], mod[1:2]).astype(BF16)
    p = _dot(h, w_ref[...])
    c, sa, sb = c_ref[...], sa_ref[...], sb_ref[...]
    for j in range(8):
        dq_ref[0, :, LANE * j:LANE * (j + 1)] = _rope(p[:, LANE * j:LANE * (j + 1)], c, sa, sb).astype(BF16)
        dk_ref[0, :, LANE * j:LANE * (j + 1)] = _rope(p[:, 1024 + LANE * j:1024 + LANE * (j + 1)], c, sa, sb).astype(BF16)
    _store_values(dv_ref, p, 2048, 8)


def _token_spec(width, tile_off=0):
    return pl.BlockSpec((1, TM, width), lambda b, t: (b, t + tile_off, 0))


def _mod_spec(d, latent_only=False):
    if latent_only:
        return pl.BlockSpec((1, 1, N_MOD, d), lambda b, t: (b, 1, 0, 0))
    return pl.BlockSpec((1, 1, N_MOD, d), lambda b, t: (b, jnp.minimum(t, 1), 0, 0))


def _rope_spec():
    return pl.BlockSpec((TM, LANE), lambda b, t: (t, 0))


def _proj_par(xa, mods, w, rope, gains, bd):
    b, t_len, d = xa.shape
    widths = (512, 512, 1024, 512, 128, 256)
    return pl.pallas_call(
        _proj_par_kernel,
        grid=(b, t_len // TM),
        in_specs=[_token_spec(d), _mod_spec(d), _const_spec(w.shape), _rope_spec(), _rope_spec(), _rope_spec(),
                  _const_spec(gains.shape), _const_spec(bd.shape)],
        out_specs=[_token_spec(n) for n in widths],
        out_shape=[jax.ShapeDtypeStruct((b, t_len, n), BF16) for n in widths],
        compiler_params=_params(2),
        name="proj_parallel_mixer",
    )(xa, mods, w, *rope, gains, bd)


def _proj_diff(xa, mods, w, rope):
    b, t_len, d = xa.shape
    widths = (1024, 1024, 2048)
    return pl.pallas_call(
        _proj_diff_kernel,
        grid=(b, t_len // TM),
        in_specs=[_token_spec(d), _mod_spec(d), _const_spec(w.shape), _rope_spec(), _rope_spec(), _rope_spec()],
        out_specs=[_token_spec(n) for n in widths],
        out_shape=[jax.ShapeDtypeStruct((b, t_len, n), BF16) for n in widths],
        compiler_params=_params(2),
        name="proj_diff_mixer",
    )(xa, mods, w, *rope)


def _split_heads(q):
    lane = lax.broadcasted_iota(jnp.int32, q.shape, 1)
    zero = jnp.zeros_like(q)
    return jnp.concatenate([jnp.where(lane < HEAD_DIM, q, zero), jnp.where(lane >= HEAD_DIM, q, zero)], axis=0)


def _softmax_update(s, m_prev):
    m_new = jnp.maximum(m_prev, jnp.max(s, axis=-1, keepdims=True))
    p = jnp.exp2(s - m_new)
    return p, jnp.exp2(m_prev - m_new), m_new


def _flash_chunk(q2_ref, k, v, m_ref, acc_ref, row0, bias_of=None):
    for r in range(2 * TM // ROW_BLOCK):
        rows = slice(row0 + ROW_BLOCK * r, row0 + ROW_BLOCK * (r + 1))
        s = _dot_nt(q2_ref[rows], k)
        if bias_of is not None:
            s = s + bias_of(ROW_BLOCK * r, ROW_BLOCK)
        p, alpha, m_new = _softmax_update(s, m_ref[rows])
        acc_ref[rows] = alpha * acc_ref[rows] + _dot(p.astype(BF16), v)
        m_ref[rows] = m_new


def _flash_pipeline(q2_ref, k_of, v_of, n_chunks, n_blocks, m_ref, acc_ref, s_refs, p_refs, alpha_refs):
    assert n_blocks % 2 == 0
    r2 = 2 * TM

    def score(i, j, slot):
        s_refs[slot][...] = _dot_nt(q2_ref[r2 * j:r2 * (j + 1)], k_of(i, j))

    def softmax(j, slot):
        for r in range(r2 // SOFTMAX_ROWS):
            rows = slice(SOFTMAX_ROWS * r, SOFTMAX_ROWS * (r + 1))
            state_rows = slice(r2 * j + SOFTMAX_ROWS * r, r2 * j + SOFTMAX_ROWS * (r + 1))
            p, alpha, m_new = _softmax_update(s_refs[slot][rows], m_ref[state_rows])
            p_refs[slot][rows] = p.astype(BF16)
            alpha_refs[slot][rows] = alpha
            m_ref[state_rows] = m_new

    def accumulate(i, j, slot):
        blk = slice(r2 * j, r2 * (j + 1))
        acc_ref[blk] = alpha_refs[slot][...] * acc_ref[blk] + _dot(p_refs[slot][...], v_of(i, j))

    last_slot = (n_blocks - 1) % 2
    p_refs[last_slot][...] = jnp.zeros(p_refs[last_slot].shape, BF16)
    alpha_refs[last_slot][...] = jnp.ones(alpha_refs[last_slot].shape, F32)
    score(0, 0, 0)

    def body(i, carry):
        for j in range(n_blocks):
            if j + 1 < n_blocks:
                score(i, j + 1, (j + 1) % 2)
            else:
                score(jnp.minimum(i + 1, n_chunks - 1), 0, 0)
            softmax(j, j % 2)
            if j > 0:
                accumulate(i, j - 1, (j - 1) % 2)
            else:
                accumulate(jnp.maximum(i - 1, 0), n_blocks - 1, last_slot)
        return carry

    lax.fori_loop(0, n_chunks, body, 0)
    accumulate(n_chunks - 1, n_blocks - 1, last_slot)


def _pipeline_scratch(tk):
    rows = 2 * TM
    return [pltpu.VMEM((rows, tk), F32)] * 2 + [pltpu.VMEM((rows, tk), BF16)] * 2 + [pltpu.VMEM((rows, 1), F32)] * 2


def _normalize(acc):
    return acc[:, 0:LANE] / acc[:, LANE:LANE + 1]


def _merge_heads(o):
    lane = lax.broadcasted_iota(jnp.int32, (TM, LANE), 1)
    return jnp.where(lane < HEAD_DIM, o[0:TM], o[TM:2 * TM])


def _init_flash(q_refs, q2_ref, m_ref, acc_ref):
    n = 0
    for q_ref in q_refs:
        for j in range(q_ref.shape[2] // LANE):
            q2_ref[2 * TM * n:2 * TM * (n + 1)] = _split_heads(q_ref[0, :, LANE * j:LANE * (j + 1)])
            n += 1
    m_ref[...] = jnp.full(m_ref.shape, NEG, F32)
    acc_ref[...] = jnp.zeros(acc_ref.shape, F32)


def _flash_scratch(n_blocks):
    rows = 2 * TM * n_blocks
    return [pltpu.VMEM((rows, LANE), BF16), pltpu.VMEM((rows, 1), F32), pltpu.VMEM((rows, 2 * LANE), F32)]


def _key_chunk(t_len):
    return next(tk for tk in (768, 640, 512, 384, 256) if t_len % tk == 0)


def _store_heads(o_ref, acc_ref, n_tiles, finish):
    n_lane_blocks = o_ref.shape[2] // LANE
    for u in range(n_tiles):
        for j in range(n_lane_blocks):
            n = u * n_lane_blocks + j
            o = _normalize(acc_ref[2 * TM * n:2 * TM * (n + 1)])
            o_ref[0, TM * u:TM * (u + 1), LANE * j:LANE * (j + 1)] = finish(o).astype(BF16)


def _gqa_ctx_kernel(q_ref, k_ref, v_ref, o_ref, q2_ref, m_ref, acc_ref):
    _init_flash([q_ref], q2_ref, m_ref, acc_ref)
    for j in range(q_ref.shape[2] // LANE):
        _flash_chunk(q2_ref, k_ref[0], v_ref[0], m_ref, acc_ref, 2 * TM * j)
    _store_heads(o_ref, acc_ref, 1, _merge_heads)


def _gqa_kernel(*refs, tk):
    q_refs = refs[:Q_TILES_PER_STEP]
    k_ref, v_ref, o_ref, q2_ref, m_ref, acc_ref = refs[Q_TILES_PER_STEP:Q_TILES_PER_STEP + 6]
    pipe_refs = refs[Q_TILES_PER_STEP + 6:]
    _init_flash(q_refs, q2_ref, m_ref, acc_ref)

    def chunk_of(ref):
        return lambda i, n: ref[0, pl.ds(pl.multiple_of(i * tk, tk), tk), :]
    _flash_pipeline(q2_ref, chunk_of(k_ref), chunk_of(v_ref), k_ref.shape[1] // tk,
                    Q_TILES_PER_STEP * (o_ref.shape[2] // LANE), m_ref, acc_ref,
                    pipe_refs[0:2], pipe_refs[2:4], pipe_refs[4:6])
    _store_heads(o_ref, acc_ref, Q_TILES_PER_STEP, _merge_heads)


def _gqa(gq, gk, gv, n_ctx, tk):
    b, t_len, qw = gq.shape
    assert n_ctx == TM and (t_len - n_ctx) % (Q_TILES_PER_STEP * TM) == 0
    n_lane_blocks = qw // LANE
    ctx_out = pl.pallas_call(
        _gqa_ctx_kernel,
        grid=(b,),
        in_specs=[pl.BlockSpec((1, TM, qw), lambda b: (b, 0, 0)),
                  pl.BlockSpec((1, TM, LANE), lambda b: (b, 0, 0)),
                  pl.BlockSpec((1, TM, 2 * LANE), lambda b: (b, 0, 0))],
        out_specs=pl.BlockSpec((1, TM, qw), lambda b: (b, 0, 0)),
        out_shape=jax.ShapeDtypeStruct((b, TM, qw), BF16),
        scratch_shapes=_flash_scratch(n_lane_blocks),
        compiler_params=_params(1),
        name="gqa_attention_ctx",
    )(gq, gk, gv)
    latent_out = pl.pallas_call(
        functools.partial(_gqa_kernel, tk=tk),
        grid=(b, (t_len - n_ctx) // (Q_TILES_PER_STEP * TM)),
        in_specs=([pl.BlockSpec((1, TM, qw), lambda b, s, u=u: (b, 1 + Q_TILES_PER_STEP * s + u, 0))
                   for u in range(Q_TILES_PER_STEP)]
                  + [pl.BlockSpec((1, t_len, LANE), lambda b, s: (b, 0, 0)),
                     pl.BlockSpec((1, t_len, 2 * LANE), lambda b, s: (b, 0, 0))]),
        out_specs=pl.BlockSpec((1, Q_TILES_PER_STEP * TM, qw), lambda b, s: (b, s, 0)),
        out_shape=jax.ShapeDtypeStruct((b, t_len - n_ctx, qw), BF16),
        scratch_shapes=_flash_scratch(Q_TILES_PER_STEP * n_lane_blocks) + _pipeline_scratch(tk),
        compiler_params=_params(2),
        name="gqa_attention",
    )(*([gq] * Q_TILES_PER_STEP), gk, gv)
    return jnp.concatenate([ctx_out, latent_out], axis=1)


def _na_kernel(q_ref, k_ref, v_ref, bias_ref, o_ref, q2_ref, m_ref, acc_ref, *pipe_refs, n_ctx, rows):
    t = pl.program_id(1)
    n_blocks = q_ref.shape[2] // LANE
    r2 = 2 * TM
    s_refs, p_refs = pipe_refs[0:2], pipe_refs[2:4]
    _init_flash([q_ref], q2_ref, m_ref, acc_ref)

    @pl.when(t == 0)
    def _():
        for j in range(n_blocks):
            _flash_chunk(q2_ref, k_ref[0, 0:n_ctx, LANE * j:LANE * (j + 1)],
                         v_ref[0, 0:n_ctx, 2 * LANE * j:2 * LANE * (j + 1)], m_ref, acc_ref, r2 * j)
            o_ref[0, :, LANE * j:LANE * (j + 1)] = _merge_heads(_normalize(acc_ref[r2 * j:r2 * (j + 1)])).astype(BF16)

    @pl.when(t > 0)
    def _():
        first_row = NA_ROWS_PER_TILE * (t - 1)
        start = jnp.clip(first_row - WIN_R // 2, 0, rows - NA_KEY_ROWS)
        off = pl.multiple_of(n_ctx + GRID_W * start, GRID_W)

        def score(j, slot):
            q2 = q2_ref[r2 * j:r2 * (j + 1)]
            s_refs[slot][:, 0:n_ctx] = _dot_nt(q2, k_ref[0, 0:n_ctx, LANE * j:LANE * (j + 1)])
            s_refs[slot][:, n_ctx:] = _dot_nt(q2, k_ref[0, pl.ds(off, NA_WIN), LANE * j:LANE * (j + 1)])

        def softmax(j, slot):
            for r in range(r2 // SOFTMAX_ROWS):
                rows_r = slice(SOFTMAX_ROWS * r, SOFTMAX_ROWS * (r + 1))
                head, row = divmod(SOFTMAX_ROWS * r, TM)
                s_ctx = s_refs[slot][rows_r, 0:n_ctx]
                s_win = s_refs[slot][rows_r, n_ctx:] + bias_ref[2 * j + head, 0, row:row + SOFTMAX_ROWS, :]
                m = jnp.maximum(jnp.max(s_ctx, axis=-1, keepdims=True), jnp.max(s_win, axis=-1, keepdims=True))
                p_refs[slot][rows_r, 0:n_ctx] = jnp.exp2(s_ctx - m).astype(BF16)
                p_refs[slot][rows_r, n_ctx:] = jnp.exp2(s_win - m).astype(BF16)

        def output(j, slot):
            acc = (_dot(p_refs[slot][:, 0:n_ctx], v_ref[0, 0:n_ctx, 2 * LANE * j:2 * LANE * (j + 1)])
                   + _dot(p_refs[slot][:, n_ctx:], v_ref[0, pl.ds(off, NA_WIN), 2 * LANE * j:2 * LANE * (j + 1)]))
            o_ref[0, :, LANE * j:LANE * (j + 1)] = _merge_heads(_normalize(acc)).astype(BF16)

        score(0, 0)
        for j in range(n_blocks):
            if j + 1 < n_blocks:
                score(j + 1, (j + 1) % 2)
            softmax(j, j % 2)
            if j > 0:
                output(j - 1, (j - 1) % 2)
        output(n_blocks - 1, (n_blocks - 1) % 2)


def _na(naq, nak, nav, bias, n_ctx):
    b, t_len, qw = naq.shape
    n_tiles = t_len // TM
    rows = (t_len - n_ctx) // GRID_W
    n_keys = n_ctx + NA_WIN

    def bias_index(b, t):
        return (0, jnp.where(t <= 1, 0, jnp.where(t == n_tiles - 1, 2, 1)), 0, 0)

    def resident(width):
        return pl.BlockSpec((1, t_len, width), lambda b, t: (b, 0, 0), pipeline_mode=pl.Buffered(1))

    return pl.pallas_call(
        functools.partial(_na_kernel, n_ctx=n_ctx, rows=rows),
        grid=(b, n_tiles),
        in_specs=[pl.BlockSpec((1, TM, qw), lambda b, t: (b, t, 0)), resident(qw), resident(2 * qw),
                  pl.BlockSpec((bias.shape[0], 1, TM, NA_WIN), bias_index, pipeline_mode=pl.Buffered(1))],
        out_specs=pl.BlockSpec((1, TM, qw), lambda b, t: (b, t, 0)),
        out_shape=jax.ShapeDtypeStruct((b, t_len, qw), BF16),
        scratch_shapes=(_flash_scratch(qw // LANE)
                        + [pltpu.VMEM((2 * TM, n_keys), F32)] * 2 + [pltpu.VMEM((2 * TM, n_keys), BF16)] * 2),
        compiler_params=_params(2),
        name="neighbourhood_attention",
    )(naq, nak, nav, bias)


def _na_bias_table(rpb, rows):
    g_of_pattern = np.array([0, 2, rows // NA_ROWS_PER_TILE - 1])
    a = np.arange(NA_ROWS_PER_TILE)
    r = NA_ROWS_PER_TILE * g_of_pattern[:, None] + a[None, :]
    start = np.clip(NA_ROWS_PER_TILE * g_of_pattern - WIN_R // 2, 0, rows - NA_KEY_ROWS)
    rs = np.clip(r - WIN_R // 2, 0, rows - WIN_R)
    key_row = start[:, None] + np.arange(NA_KEY_ROWS)[None, :]
    row_ok = (key_row[:, None, :] >= rs[:, :, None]) & (key_row[:, None, :] < rs[:, :, None] + WIN_R)
    row_off = np.clip(key_row[:, None, :] - r[:, :, None] + (WIN_R - 1), 0, 2 * WIN_R - 2)
    cols = np.arange(GRID_W)
    col_start = np.clip(cols - WIN_C // 2, 0, GRID_W - WIN_C)
    col_ok = (cols[None, :] >= col_start[:, None]) & (cols[None, :] < col_start[:, None] + WIN_C)
    col_off = np.clip(cols[None, :] - cols[:, None] + (WIN_C - 1), 0, 2 * WIN_C - 2)
    ok = row_ok[:, :, None, :, None] & col_ok[None, None, :, None, :]
    bias_rows = rpb.astype(F32)[:, row_off]
    pick_col = (col_off[:, :, None] == np.arange(2 * WIN_C - 1)).astype(np.float32)
    vals = jnp.einsum('hpakb,cjb->hpackj', bias_rows, pick_col, precision=lax.Precision.HIGHEST)
    table = jnp.where(ok[None], vals * LOG2_E, NEG)
    return table.reshape(rpb.shape[0], 3, TM, NA_WIN)


def _diff_kernel(lam_ref, gain_ref, *refs, tk, lambda_init):
    q_refs = refs[:Q_TILES_PER_STEP]
    k_ref, v_ref, o_ref, q2_ref, m_ref, acc_ref = refs[Q_TILES_PER_STEP:Q_TILES_PER_STEP + 6]
    pipe_refs = refs[Q_TILES_PER_STEP + 6:]
    n_heads = o_ref.shape[2] // LANE
    _init_flash(q_refs, q2_ref, m_ref, acc_ref)

    def chunk_of(ref, width):
        def chunk(i, n):
            h = n % n_heads
            return ref[0, pl.ds(pl.multiple_of(i * tk, tk), tk), width * h:width * (h + 1)]
        return chunk
    _flash_pipeline(q2_ref, chunk_of(k_ref, LANE), chunk_of(v_ref, 2 * LANE), k_ref.shape[1] // tk,
                    Q_TILES_PER_STEP * n_heads, m_ref, acc_ref, pipe_refs[0:2], pipe_refs[2:4], pipe_refs[4:6])

    lp = lam_ref[...]
    lam = (jnp.exp(jnp.sum(lp[0:1] * lp[1:2], axis=-1, keepdims=True))
           - jnp.exp(jnp.sum(lp[2:3] * lp[3:4], axis=-1, keepdims=True)) + lambda_init)

    def sub_layer_norm(o):
        d = o[0:TM] - lam * o[TM:2 * TM]
        y = d * lax.rsqrt(jnp.mean(d * d, axis=-1, keepdims=True) + EPS) * gain_ref[...]
        return y * (1.0 - lambda_init)
    _store_heads(o_ref, acc_ref, Q_TILES_PER_STEP, sub_layer_norm)


def _diff(lam_params, subln_gain, dq, dk, dv, n_ctx, tk, lambda_init):
    b, t_len, qw = dq.shape
    assert n_ctx == TM and (t_len - n_ctx) % (Q_TILES_PER_STEP * TM) == 0
    hw = LANE * DIFF_HEADS_PER_STEP
    return pl.pallas_call(
        functools.partial(_diff_kernel, tk=tk, lambda_init=lambda_init),
        grid=(b, qw // hw, (t_len - n_ctx) // (Q_TILES_PER_STEP * TM)),
        in_specs=([pl.BlockSpec(lam_params.shape, lambda b, h, s: (0, 0)),
                   pl.BlockSpec(subln_gain.shape, lambda b, h, s: (0, 0))]
                  + [pl.BlockSpec((1, TM, hw), lambda b, h, s, u=u: (b, 1 + Q_TILES_PER_STEP * s + u, h))
                     for u in range(Q_TILES_PER_STEP)]
                  + [pl.BlockSpec((1, t_len, hw), lambda b, h, s: (b, 0, h), pipeline_mode=pl.Buffered(1)),
                     pl.BlockSpec((1, t_len, 2 * hw), lambda b, h, s: (b, 0, h), pipeline_mode=pl.Buffered(1))]),
        out_specs=pl.BlockSpec((1, Q_TILES_PER_STEP * TM, hw), lambda b, h, s: (b, s, h)),
        out_shape=jax.ShapeDtypeStruct((b, t_len - n_ctx, qw), BF16),
        scratch_shapes=_flash_scratch(Q_TILES_PER_STEP * DIFF_HEADS_PER_STEP) + _pipeline_scratch(tk),
        compiler_params=_params(3),
        name="diff_attention",
    )(lam_params, subln_gain, *([dq] * Q_TILES_PER_STEP), dk, dv)


def _ffn_kernel(*refs, n_att, ff_chunks, final):
    x_ref, mod_ref = refs[0], refs[1]
    att_refs = refs[2:2 + n_att]
    wo_refs = refs[2 + n_att:2 + 2 * n_att]
    wg_ref, wu_ref, wd_ref = refs[2 + 2 * n_att:5 + 2 * n_att]
    o_ref = refs[-1]
    mod = mod_ref[0, 0]
    y = _dot(att_refs[0][0], wo_refs[0][...])
    for a_ref, w_ref in zip(att_refs[1:], wo_refs[1:]):
        y = y + _dot(a_ref[0], w_ref[...])
    x1 = x_ref[0] + mod[2:3] * y
    h = _modulated_norm(x1, mod[3:4], mod[4:5]).astype(BF16)
    d_ff = wg_ref.shape[1]
    bounds = [0]
    for i in range(ff_chunks):
        bounds.append(min(d_ff, -(-(d_ff * (i + 1) // ff_chunks) // MXU_DEPTH) * MXU_DEPTH))
    down = None
    for lo, hi in zip(bounds[:-1], bounds[1:]):
        g = _dot(h, wg_ref[:, lo:hi])
        u = _dot(h, wu_ref[:, lo:hi])
        a = (g * (1.0 / (1.0 + jnp.exp(-g))) * u).astype(BF16)
        part = _dot(a, wd_ref[lo:hi, :])
        down = part if down is None else down + part
    x2 = x1 + mod[5:6] * down
    if final:
        gain_ref = refs[5 + 2 * n_att]
        x2 = x2 * lax.rsqrt(jnp.mean(x2 * x2, axis=-1, keepdims=True) + EPS) * gain_ref[...]
    o_ref[0] = x2


def _ffn(xa, mods, atts, wos, wg, wu, wd, final_gain=None):
    b, t_len, d = xa.shape
    n_rows = atts[0].shape[1]
    tile_off = (t_len - n_rows) // TM
    final = final_gain is not None
    in_specs = ([_token_spec(d, tile_off), _mod_spec(d, latent_only=tile_off > 0)]
                + [_token_spec(a.shape[2]) for a in atts]
                + [_const_spec(w.shape) for w in wos]
                + [_const_spec(wg.shape), _const_spec(wu.shape), _const_spec(wd.shape)])
    args = [xa, mods, *atts, *wos, wg, wu, wd]
    if final:
        in_specs.append(_const_spec(final_gain.shape))
        args.append(final_gain)
    return pl.pallas_call(
        functools.partial(_ffn_kernel, n_att=len(atts), ff_chunks=2, final=final),
        grid=(b, n_rows // TM),
        in_specs=in_specs,
        out_specs=_token_spec(d),
        out_shape=jax.ShapeDtypeStruct((b, n_rows, d), F32),
        compiler_params=_params(2),
        name="outproj_ffn_final" if final else "outproj_ffn",
    )(*args)


def _head_cols(base, heads):
    return np.concatenate([base + HEAD_DIM * h + _DEINT for h in heads])


def _rope_tables(seq, n_ctx):
    t = jnp.arange(seq)
    row = (t // GRID_W).astype(F32)
    col = (t % GRID_W).astype(F32)
    n_freq = HEAD_DIM // 4
    inv = ROPE_THETA ** (-jnp.arange(n_freq, dtype=F32) / n_freq)
    ang = jnp.concatenate([row[:, None] * inv, col[:, None] * inv], axis=-1)
    cos, sin = jnp.cos(ang), jnp.sin(ang)
    zero = jnp.zeros_like(sin)

    def table(first_half, second_half, ctx_value):
        lat = jnp.tile(jnp.concatenate([first_half, second_half], axis=-1), (1, LANE // HEAD_DIM))
        return jnp.concatenate([jnp.full((n_ctx, LANE), ctx_value, F32), lat], axis=0)

    return table(cos, cos, 1.0), table(zero, sin, 0.0), table(-sin, zero, 0.0)


def kernel(x, c, ctx, c_ctx, ada_w, ada_b, ffn_w_gate, ffn_w_up, ffn_w_down, par_w_in, par_w_out, na_rpb,
           gqa_q_gain, gqa_k_gain, diff_w_in, diff_w_out, diff_lambda_q1, diff_lambda_k1, diff_lambda_q2,
           diff_lambda_k2, diff_subln_gain, final_norm_gain):
    b, seq, d = x.shape
    n_ctx = ctx.shape[1]
    assert n_ctx == TM and seq % TM == 0 and d % LANE == 0 and b < 8 and ada_w.shape[0] == 2
    rows = seq // GRID_W
    scale = HEAD_DIM ** -0.5 * LOG2_E

    xa = jnp.concatenate([ctx, x], axis=1)
    cond = jnp.zeros((8, d), F32).at[:b].set(c).at[b].set(c_ctx)
    mods_all = _ada(cond, ada_w, ada_b).reshape(2, 8, N_MOD, d)

    def mods_of(layer):
        m = mods_all[layer]
        return jnp.stack([jnp.broadcast_to(m[b], (b, N_MOD, d)), m[:b]], axis=1)

    rope = _rope_tables(seq, n_ctx)

    cols0 = np.concatenate([np.arange(0, 1536), _head_cols(1536, _GQA_HEAD_ORDER), _head_cols(2048, (0, 1)),
                            np.arange(2176, 2304)])
    col_scale0 = np.ones((2304,), np.float32)
    col_scale0[0:512] = scale
    w_in0 = (par_w_in[0][:, cols0] * col_scale0).astype(BF16)
    gains = jnp.zeros((8, LANE), F32)
    gains = gains.at[0].set(jnp.tile(gqa_q_gain[0][_DEINT] * scale, 2)).at[1].set(jnp.tile(gqa_k_gain[0][_DEINT], 2))
    block_mean = jnp.asarray(np.kron(np.eye(LANE // HEAD_DIM), np.full((HEAD_DIM, HEAD_DIM), 1.0 / HEAD_DIM)), BF16)
    naq, nak, nav, gq, gk, gv = _proj_par(xa, mods_of(0), w_in0, rope, gains, block_mean)
    att_na = _na(naq, nak, nav, _na_bias_table(na_rpb[0], rows), n_ctx)
    att_g = _gqa(gq, gk, gv, n_ctx, tk=_key_chunk(n_ctx + seq))
    wo_na = par_w_out[0][0:512].astype(BF16)
    wo_g = par_w_out[0][512 + np.concatenate([HEAD_DIM * h + np.arange(HEAD_DIM) for h in _GQA_HEAD_ORDER])].astype(BF16)
    xa = _ffn(xa, mods_of(0), [att_na, att_g], [wo_na, wo_g],
              ffn_w_gate[0].astype(BF16), ffn_w_up[0].astype(BF16), ffn_w_down[0].astype(BF16))

    cols1 = np.concatenate([_head_cols(0, range(16)), _head_cols(1024, range(16)), np.arange(2048, 3072)])
    col_scale1 = np.ones((3072,), np.float32)
    col_scale1[0:1024] = scale
    w_in1 = (diff_w_in[0][:, cols1] * col_scale1).astype(BF16)
    dq, dk, dv = _proj_diff(xa, mods_of(1), w_in1, rope)
    lambda_init = 0.8 - 0.6 * float(np.exp(-0.3 * 1))
    lam_params = jnp.stack([diff_lambda_q1[0], diff_lambda_k1[0], diff_lambda_q2[0], diff_lambda_k2[0]]).astype(F32)
    att_d = _diff(lam_params, diff_subln_gain[0].reshape(1, -1).astype(F32), dq, dk, dv, n_ctx,
                  tk=_key_chunk(n_ctx + seq),
                  lambda_init=lambda_init)
    return _ffn(xa, mods_of(1), [att_d], [diff_w_out[0].astype(BF16)],
                ffn_w_gate[1].astype(BF16), ffn_w_up[1].astype(BF16), ffn_w_down[1].astype(BF16),
                final_gain=final_norm_gain.reshape(1, -1).astype(F32))
```

```python
import functools

import numpy as np
import jax
import jax.numpy as jnp
from jax import lax
from jax.experimental import pallas as pl
from jax.experimental.pallas import tpu as pltpu

F32 = jnp.float32
BF16 = jnp.bfloat16

GRID_W = 64
HEAD_DIM = 64
WIN_R = 8
WIN_C = 16
N_MOD = 6
ROPE_THETA = 10000.0
EPS = 1e-6

LANE = 128
TM = 256
NA_ROWS_PER_TILE = TM // GRID_W
NA_KEY_ROWS = 12
NA_WIN = NA_KEY_ROWS * GRID_W
ROW_BLOCK = 128
DIFF_HEADS_PER_STEP = 4
Q_TILES_PER_STEP = 2
PIPE_SLOTS = 2
SOFTMAX_ROWS = 32
LOG2_E = 1.4426950408889634
MXU_DEPTH = 256
NEG = -1e30
VMEM_LIMIT = 56 * 1024 * 1024

_DEINT = np.concatenate([np.arange(0, HEAD_DIM, 2), np.arange(1, HEAD_DIM, 2)])
_GQA_HEAD_ORDER = (0, 4, 1, 5, 2, 6, 3, 7)


def _dot(a, b):
    return jnp.dot(a, b, preferred_element_type=F32)


def _dot_nt(a, b):
    return lax.dot_general(a, b, (((1,), (1,)), ((), ())), preferred_element_type=F32)


def _params(n_grid):
    return pltpu.CompilerParams(dimension_semantics=("arbitrary",) * n_grid, vmem_limit_bytes=VMEM_LIMIT)


def _const_spec(shape):
    return pl.BlockSpec(shape, lambda *_: (0,) * len(shape), pipeline_mode=pl.Buffered(1))


def _split_bf16(a):
    hi = a.astype(BF16)
    return hi, (a - hi.astype(F32)).astype(BF16)


def _ada_kernel(cond_ref, w_ref, b_ref, o_ref):
    c = cond_ref[...]
    a_hi, a_lo = _split_bf16(c * (1.0 / (1.0 + jnp.exp(-c))))
    w_hi, w_lo = _split_bf16(w_ref[0])
    o_ref[0] = _dot(a_hi, w_hi) + _dot(a_lo, w_hi) + _dot(a_hi, w_lo) + b_ref[0]


def _ada(cond, ada_w, ada_b):
    depth, d, n = ada_w.shape
    tn = 1024
    return pl.pallas_call(
        _ada_kernel,
        grid=(depth, n // tn),
        in_specs=[pl.BlockSpec(cond.shape, lambda l, j: (0, 0)),
                  pl.BlockSpec((1, d, tn), lambda l, j: (l, 0, j)),
                  pl.BlockSpec((1, 1, tn), lambda l, j: (l, 0, j))],
        out_specs=pl.BlockSpec((1, cond.shape[0], tn), lambda l, j: (l, 0, j)),
        out_shape=jax.ShapeDtypeStruct((depth, cond.shape[0], n), F32),
        compiler_params=_params(2),
        name="ada_modulation",
    )(cond, ada_w, ada_b.reshape(depth, 1, n))


def _modulated_norm(x, shift, scale):
    ms = jnp.mean(x * x, axis=-1, keepdims=True)
    return (x * lax.rsqrt(ms + EPS)) * (1.0 + scale) + shift


def _group_mean_sq(x, bd):
    hi, lo = _split_bf16(x * x)
    return _dot(hi, bd) + _dot(lo, bd)


def _rope(x, c, sa, sb):
    return x * c + pltpu.roll(x, 32, 1) * sa + pltpu.roll(x, 96, 1) * sb


def _ones_column(rows):
    return (lax.broadcasted_iota(jnp.int32, (rows, LANE), 1) == 0).astype(BF16)


def _store_values(v_ref, p, col0, n_blocks):
    ones = _ones_column(p.shape[0])
    for j in range(n_blocks):
        v_ref[0, :, 2 * LANE * j:2 * LANE * j + LANE] = p[:, col0 + LANE * j:col0 + LANE * (j + 1)].astype(BF16)
        v_ref[0, :, 2 * LANE * j + LANE:2 * LANE * (j + 1)] = ones


def _proj_par_kernel(x_ref, mod_ref, w_ref, c_ref, sa_ref, sb_ref, gain_ref, bd_ref,
                     naq_ref, nak_ref, nav_ref, gq_ref, gk_ref, gv_ref):
    mod = mod_ref[0, 0]
    h = _modulated_norm(x_ref[0], mod[0:1], mod[1:2]).astype(BF16)
    p = _dot(h, w_ref[...])
    naq_ref[0] = p[:, 0:512].astype(BF16)
    nak_ref[0] = p[:, 512:1024].astype(BF16)
    _store_values(nav_ref, p, 1024, 4)
    c, sa, sb, bd = c_ref[...], sa_ref[...], sb_ref[...], bd_ref[...]

    def qk_norm_rope(g, gain):
        g = g * lax.rsqrt(_group_mean_sq(g, bd) + EPS) * gain
        return _rope(g, c, sa, sb).astype(BF16)

    for j in range(4):
        gq_ref[0, :, LANE * j:LANE * (j + 1)] = qk_norm_rope(p[:, 1536 + LANE * j:1536 + LANE * (j + 1)], gain_ref[0:1, :])
    gk_ref[0] = qk_norm_rope(p[:, 2048:2176], gain_ref[1:2, :])
    _store_values(gv_ref, p, 2176, 1)


def _proj_diff_kernel(x_ref, mod_ref, w_ref, c_ref, sa_ref, sb_ref, dq_ref, dk_ref, dv_ref):
    mod = mod_ref[0, 0]
    h = _modulated_norm(x_ref[0], mod[0:1], mod[1:2]).astype(BF16)
    p = _dot(h, w_ref[...])
    c, sa, sb = c_ref[...], sa_ref[...], sb_ref[...]
    for j in range(8):
        dq_ref[0, :, LANE * j:LANE * (j + 1)] = _rope(p[:, LANE * j:LANE * (j + 1)], c, sa, sb).astype(BF16)
        dk_ref[0, :, LANE * j:LANE * (j + 1)] = _rope(p[:, 1024 + LANE * j:1024 + LANE * (j + 1)], c, sa, sb).astype(BF16)
    _store_values(dv_ref, p, 2048, 8)


def _token_spec(width, tile_off=0):
    return pl.BlockSpec((1, TM, width), lambda b, t: (b, t + tile_off, 0))


def _mod_spec(d, latent_only=False):
    if latent_only:
        return pl.BlockSpec((1, 1, N_MOD, d), lambda b, t: (b, 1, 0, 0))
    return pl.BlockSpec((1, 1, N_MOD, d), lambda b, t: (b, jnp.minimum(t, 1), 0, 0))


def _rope_spec():
    return pl.BlockSpec((TM, LANE), lambda b, t: (t, 0))


def _proj_par(xa, mods, w, rope, gains, bd):
    b, t_len, d = xa.shape
    widths = (512, 512, 1024, 512, 128, 256)
    return pl.pallas_call(
        _proj_par_kernel,
        grid=(b, t_len // TM),
        in_specs=[_token_spec(d), _mod_spec(d), _const_spec(w.shape), _rope_spec(), _rope_spec(), _rope_spec(),
                  _const_spec(gains.shape), _const_spec(bd.shape)],
        out_specs=[_token_spec(n) for n in widths],
        out_shape=[jax.ShapeDtypeStruct((b, t_len, n), BF16) for n in widths],
        compiler_params=_params(2),
        name="proj_parallel_mixer",
    )(xa, mods, w, *rope, gains, bd)


def _proj_diff(xa, mods, w, rope):
    b, t_len, d = xa.shape
    widths = (1024, 1024, 2048)
    return pl.pallas_call(
        _proj_diff_kernel,
        grid=(b, t_len // TM),
        in_specs=[_token_spec(d), _mod_spec(d), _const_spec(w.shape), _rope_spec(), _rope_spec(), _rope_spec()],
        out_specs=[_token_spec(n) for n in widths],
        out_shape=[jax.ShapeDtypeStruct((b, t_len, n), BF16) for n in widths],
        compiler_params=_params(2),
        name="proj_diff_mixer",
    )(xa, mods, w, *rope)


def _split_heads(q):
    lane = lax.broadcasted_iota(jnp.int32, q.shape, 1)
    zero = jnp.zeros_like(q)
    return jnp.concatenate([jnp.where(lane < HEAD_DIM, q, zero), jnp.where(lane >= HEAD_DIM, q, zero)], axis=0)


def _softmax_update(s, m_prev):
    m_new = jnp.maximum(m_prev, jnp.max(s, axis=-1, keepdims=True))
    p = jnp.exp2(s - m_new)
    return p, jnp.exp2(m_prev - m_new), m_new


def _flash_chunk(q2_ref, k, v, m_ref, acc_ref, row0, bias_of=None):
    for r in range(2 * TM // ROW_BLOCK):
        rows = slice(row0 + ROW_BLOCK * r, row0 + ROW_BLOCK * (r + 1))
        s = _dot_nt(q2_ref[rows], k)
        if bias_of is not None:
            s = s + bias_of(ROW_BLOCK * r, ROW_BLOCK)
        p, alpha, m_new = _softmax_update(s, m_ref[rows])
        acc_ref[rows] = alpha * acc_ref[rows] + _dot(p.astype(BF16), v)
        m_ref[rows] = m_new


def _flash_pipeline(q2_ref, k_of, v_of, n_chunks, n_blocks, m_ref, acc_ref, s_refs, p_refs, d_ref):
    assert n_blocks % PIPE_SLOTS == 0
    r2 = 2 * TM

    def score(i, j, slot):
        s_refs[slot][...] = _dot_nt(q2_ref[r2 * j:r2 * (j + 1)], k_of(i, j))

    def softmax(j, slot, zero):
        for r in range(r2 // SOFTMAX_ROWS):
            rows = slice(SOFTMAX_ROWS * r, SOFTMAX_ROWS * (r + 1))
            state_rows = slice(r2 * j + SOFTMAX_ROWS * r, r2 * j + SOFTMAX_ROWS * (r + 1))
            s = s_refs[slot][pl.ds(zero + SOFTMAX_ROWS * r, SOFTMAX_ROWS)]
            p, alpha, m_new = _softmax_update(s, m_ref[state_rows])
            p_refs[slot][rows] = p.astype(BF16)
            d = d_ref[pl.ds(zero + r2 * j + SOFTMAX_ROWS * r, SOFTMAX_ROWS)]
            acc_ref[state_rows] = (acc_ref[state_rows] + d) * alpha
            m_ref[state_rows] = m_new

    def value_product(i, j, slot, zero):
        d_ref[r2 * j:r2 * (j + 1)] = _dot(p_refs[slot][pl.ds(zero, r2)], v_of(i, j))

    last_slot = (n_blocks - 1) % PIPE_SLOTS
    p_refs[last_slot][...] = jnp.zeros(p_refs[last_slot].shape, BF16)
    d_ref[...] = jnp.zeros(d_ref.shape, F32)
    score(0, 0, 0)

    def body(i, carry):
        zero = pl.multiple_of(jnp.minimum(i, 0), SOFTMAX_ROWS)
        for j in range(n_blocks):
            if j + 1 < n_blocks:
                score(i, j + 1, (j + 1) % PIPE_SLOTS)
            else:
                score(jnp.minimum(i + 1, n_chunks - 1), 0, 0)
            softmax(j, j % PIPE_SLOTS, zero)
            if j > 0:
                value_product(i, j - 1, (j - 1) % PIPE_SLOTS, zero)
            else:
                value_product(jnp.maximum(i - 1, 0), n_blocks - 1, last_slot, zero)
        return carry

    lax.fori_loop(0, n_chunks, body, 0)
    value_product(n_chunks - 1, n_blocks - 1, last_slot, 0)
    acc_ref[...] += d_ref[...]


def _pipeline_scratch(tk, n_blocks):
    rows = 2 * TM
    return ([pltpu.VMEM((rows, tk), F32)] * PIPE_SLOTS + [pltpu.VMEM((rows, tk), BF16)] * PIPE_SLOTS
            + [pltpu.VMEM((rows * n_blocks, 2 * LANE), F32)])


def _normalize(acc):
    return acc[:, 0:LANE] / acc[:, LANE:LANE + 1]


def _merge_heads(o):
    lane = lax.broadcasted_iota(jnp.int32, (TM, LANE), 1)
    return jnp.where(lane < HEAD_DIM, o[0:TM], o[TM:2 * TM])


def _init_flash(q_refs, q2_ref, m_ref, acc_ref):
    n = 0
    for q_ref in q_refs:
        for j in range(q_ref.shape[2] // LANE):
            q2_ref[2 * TM * n:2 * TM * (n + 1)] = _split_heads(q_ref[0, :, LANE * j:LANE * (j + 1)])
            n += 1
    m_ref[...] = jnp.full(m_ref.shape, NEG, F32)
    acc_ref[...] = jnp.zeros(acc_ref.shape, F32)


def _flash_scratch(n_blocks):
    rows = 2 * TM * n_blocks
    return [pltpu.VMEM((rows, LANE), BF16), pltpu.VMEM((rows, 1), F32), pltpu.VMEM((rows, 2 * LANE), F32)]


def _key_chunk(t_len):
    return next(tk for tk in (768, 640, 512, 384, 256) if t_len % tk == 0)


def _store_heads(o_ref, acc_ref, n_tiles, finish):
    n_lane_blocks = o_ref.shape[2] // LANE
    for u in range(n_tiles):
        for j in range(n_lane_blocks):
            n = u * n_lane_blocks + j
            o = _normalize(acc_ref[2 * TM * n:2 * TM * (n + 1)])
            o_ref[0, TM * u:TM * (u + 1), LANE * j:LANE * (j + 1)] = finish(o).astype(BF16)


def _gqa_ctx_kernel(q_ref, k_ref, v_ref, o_ref, q2_ref, m_ref, acc_ref):
    _init_flash([q_ref], q2_ref, m_ref, acc_ref)
    for j in range(q_ref.shape[2] // LANE):
        _flash_chunk(q2_ref, k_ref[0], v_ref[0], m_ref, acc_ref, 2 * TM * j)
    _store_heads(o_ref, acc_ref, 1, _merge_heads)


def _gqa_kernel(*refs, tk):
    q_refs = refs[:Q_TILES_PER_STEP]
    k_ref, v_ref, o_ref, q2_ref, m_ref, acc_ref = refs[Q_TILES_PER_STEP:Q_TILES_PER_STEP + 6]
    pipe_refs = refs[Q_TILES_PER_STEP + 6:]
    _init_flash(q_refs, q2_ref, m_ref, acc_ref)

    def chunk_of(ref):
        return lambda i, n: ref[0, pl.ds(pl.multiple_of(i * tk, tk), tk), :]
    _flash_pipeline(q2_ref, chunk_of(k_ref), chunk_of(v_ref), k_ref.shape[1] // tk,
                    Q_TILES_PER_STEP * (o_ref.shape[2] // LANE), m_ref, acc_ref,
                    pipe_refs[0:PIPE_SLOTS], pipe_refs[PIPE_SLOTS:2 * PIPE_SLOTS], pipe_refs[2 * PIPE_SLOTS])
    _store_heads(o_ref, acc_ref, Q_TILES_PER_STEP, _merge_heads)


def _gqa(gq, gk, gv, n_ctx, tk):
    b, t_len, qw = gq.shape
    assert n_ctx == TM and (t_len - n_ctx) % (Q_TILES_PER_STEP * TM) == 0
    n_lane_blocks = qw // LANE
    ctx_out = pl.pallas_call(
        _gqa_ctx_kernel,
        grid=(b,),
        in_specs=[pl.BlockSpec((1, TM, qw), lambda b: (b, 0, 0)),
                  pl.BlockSpec((1, TM, LANE), lambda b: (b, 0, 0)),
                  pl.BlockSpec((1, TM, 2 * LANE), lambda b: (b, 0, 0))],
        out_specs=pl.BlockSpec((1, TM, qw), lambda b: (b, 0, 0)),
        out_shape=jax.ShapeDtypeStruct((b, TM, qw), BF16),
        scratch_shapes=_flash_scratch(n_lane_blocks),
        compiler_params=_params(1),
        name="gqa_attention_ctx",
    )(gq, gk, gv)
    latent_out = pl.pallas_call(
        functools.partial(_gqa_kernel, tk=tk),
        grid=(b, (t_len - n_ctx) // (Q_TILES_PER_STEP * TM)),
        in_specs=([pl.BlockSpec((1, TM, qw), lambda b, s, u=u: (b, 1 + Q_TILES_PER_STEP * s + u, 0))
                   for u in range(Q_TILES_PER_STEP)]
                  + [pl.BlockSpec((1, t_len, LANE), lambda b, s: (b, 0, 0)),
                     pl.BlockSpec((1, t_len, 2 * LANE), lambda b, s: (b, 0, 0))]),
        out_specs=pl.BlockSpec((1, Q_TILES_PER_STEP * TM, qw), lambda b, s: (b, s, 0)),
        out_shape=jax.ShapeDtypeStruct((b, t_len - n_ctx, qw), BF16),
        scratch_shapes=_flash_scratch(Q_TILES_PER_STEP * n_lane_blocks) + _pipeline_scratch(tk, Q_TILES_PER_STEP * n_lane_blocks),
        compiler_params=_params(2),
        name="gqa_attention",
    )(*([gq] * Q_TILES_PER_STEP), gk, gv)
    return jnp.concatenate([ctx_out, latent_out], axis=1)


def _na_kernel(q_ref, k_ref, v_ref, bias_ref, o_ref, q2_ref, m_ref, acc_ref, *pipe_refs, n_ctx, rows):
    t = pl.program_id(1)
    n_blocks = q_ref.shape[2] // LANE
    r2 = 2 * TM
    s_refs, p_refs = pipe_refs[0:2], pipe_refs[2:4]
    _init_flash([q_ref], q2_ref, m_ref, acc_ref)

    @pl.when(t == 0)
    def _():
        for j in range(n_blocks):
            _flash_chunk(q2_ref, k_ref[0, 0:n_ctx, LANE * j:LANE * (j + 1)],
                         v_ref[0, 0:n_ctx, 2 * LANE * j:2 * LANE * (j + 1)], m_ref, acc_ref, r2 * j)
            o_ref[0, :, LANE * j:LANE * (j + 1)] = _merge_heads(_normalize(acc_ref[r2 * j:r2 * (j + 1)])).astype(BF16)

    @pl.when(t > 0)
    def _():
        first_row = NA_ROWS_PER_TILE * (t - 1)
        start = jnp.clip(first_row - WIN_R // 2, 0, rows - NA_KEY_ROWS)
        off = pl.multiple_of(n_ctx + GRID_W * start, GRID_W)

        def score(j, slot):
            q2 = q2_ref[r2 * j:r2 * (j + 1)]
            s_refs[slot][:, 0:n_ctx] = _dot_nt(q2, k_ref[0, 0:n_ctx, LANE * j:LANE * (j + 1)])
            s_refs[slot][:, n_ctx:] = _dot_nt(q2, k_ref[0, pl.ds(off, NA_WIN), LANE * j:LANE * (j + 1)])

        def softmax(j, slot):
            for r in range(r2 // SOFTMAX_ROWS):
                rows_r = slice(SOFTMAX_ROWS * r, SOFTMAX_ROWS * (r + 1))
                head, row = divmod(SOFTMAX_ROWS * r, TM)
                s_ctx = s_refs[slot][rows_r, 0:n_ctx]
                s_win = s_refs[slot][rows_r, n_ctx:] + bias_ref[2 * j + head, 0, row:row + SOFTMAX_ROWS, :]
                m = jnp.maximum(jnp.max(s_ctx, axis=-1, keepdims=True), jnp.max(s_win, axis=-1, keepdims=True))
                p_refs[slot][rows_r, 0:n_ctx] = jnp.exp2(s_ctx - m).astype(BF16)
                p_refs[slot][rows_r, n_ctx:] = jnp.exp2(s_win - m).astype(BF16)

        def output(j, slot):
            acc = (_dot(p_refs[slot][:, 0:n_ctx], v_ref[0, 0:n_ctx, 2 * LANE * j:2 * LANE * (j + 1)])
                   + _dot(p_refs[slot][:, n_ctx:], v_ref[0, pl.ds(off, NA_WIN), 2 * LANE * j:2 * LANE * (j + 1)]))
            o_ref[0, :, LANE * j:LANE * (j + 1)] = _merge_heads(_normalize(acc)).astype(BF16)

        score(0, 0)
        for j in range(n_blocks):
            if j + 1 < n_blocks:
                score(j + 1, (j + 1) % 2)
            softmax(j, j % 2)
            if j > 0:
                output(j - 1, (j - 1) % 2)
        output(n_blocks - 1, (n_blocks - 1) % 2)


def _na(naq, nak, nav, bias, n_ctx):
    b, t_len, qw = naq.shape
    n_tiles = t_len // TM
    rows = (t_len - n_ctx) // GRID_W
    n_keys = n_ctx + NA_WIN

    def bias_index(b, t):
        return (0, jnp.where(t <= 1, 0, jnp.where(t == n_tiles - 1, 2, 1)), 0, 0)

    def resident(width):
        return pl.BlockSpec((1, t_len, width), lambda b, t: (b, 0, 0), pipeline_mode=pl.Buffered(1))

    return pl.pallas_call(
        functools.partial(_na_kernel, n_ctx=n_ctx, rows=rows),
        grid=(b, n_tiles),
        in_specs=[pl.BlockSpec((1, TM, qw), lambda b, t: (b, t, 0)), resident(qw), resident(2 * qw),
                  pl.BlockSpec((bias.shape[0], 1, TM, NA_WIN), bias_index, pipeline_mode=pl.Buffered(1))],
        out_specs=pl.BlockSpec((1, TM, qw), lambda b, t: (b, t, 0)),
        out_shape=jax.ShapeDtypeStruct((b, t_len, qw), BF16),
        scratch_shapes=(_flash_scratch(qw // LANE)
                        + [pltpu.VMEM((2 * TM, n_keys), F32)] * 2 + [pltpu.VMEM((2 * TM, n_keys), BF16)] * 2),
        compiler_params=_params(2),
        name="neighbourhood_attention",
    )(naq, nak, nav, bias)


def _na_bias_table(rpb, rows):
    g_of_pattern = np.array([0, 2, rows // NA_ROWS_PER_TILE - 1])
    a = np.arange(NA_ROWS_PER_TILE)
    r = NA_ROWS_PER_TILE * g_of_pattern[:, None] + a[None, :]
    start = np.clip(NA_ROWS_PER_TILE * g_of_pattern - WIN_R // 2, 0, rows - NA_KEY_ROWS)
    rs = np.clip(r - WIN_R // 2, 0, rows - WIN_R)
    key_row = start[:, None] + np.arange(NA_KEY_ROWS)[None, :]
    row_ok = (key_row[:, None, :] >= rs[:, :, None]) & (key_row[:, None, :] < rs[:, :, None] + WIN_R)
    row_off = np.clip(key_row[:, None, :] - r[:, :, None] + (WIN_R - 1), 0, 2 * WIN_R - 2)
    cols = np.arange(GRID_W)
    col_start = np.clip(cols - WIN_C // 2, 0, GRID_W - WIN_C)
    col_ok = (cols[None, :] >= col_start[:, None]) & (cols[None, :] < col_start[:, None] + WIN_C)
    col_off = np.clip(cols[None, :] - cols[:, None] + (WIN_C - 1), 0, 2 * WIN_C - 2)
    ok = row_ok[:, :, None, :, None] & col_ok[None, None, :, None, :]
    bias_rows = rpb.astype(F32)[:, row_off]
    pick_col = (col_off[:, :, None] == np.arange(2 * WIN_C - 1)).astype(np.float32)
    vals = jnp.einsum('hpakb,cjb->hpackj', bias_rows, pick_col, precision=lax.Precision.HIGHEST)
    table = jnp.where(ok[None], vals * LOG2_E, NEG)
    return table.reshape(rpb.shape[0], 3, TM, NA_WIN)


def _diff_kernel(lam_ref, gain_ref, *refs, tk, lambda_init):
    q_refs = refs[:Q_TILES_PER_STEP]
    k_ref, v_ref, o_ref, q2_ref, m_ref, acc_ref = refs[Q_TILES_PER_STEP:Q_TILES_PER_STEP + 6]
    pipe_refs = refs[Q_TILES_PER_STEP + 6:]
    n_heads = o_ref.shape[2] // LANE
    _init_flash(q_refs, q2_ref, m_ref, acc_ref)

    def chunk_of(ref, width):
        def chunk(i, n):
            h = n % n_heads
            return ref[0, pl.ds(pl.multiple_of(i * tk, tk), tk), width * h:width * (h + 1)]
        return chunk
    _flash_pipeline(q2_ref, chunk_of(k_ref, LANE), chunk_of(v_ref, 2 * LANE), k_ref.shape[1] // tk,
                    Q_TILES_PER_STEP * n_heads, m_ref, acc_ref, pipe_refs[0:PIPE_SLOTS], pipe_refs[PIPE_SLOTS:2 * PIPE_SLOTS], pipe_refs[2 * PIPE_SLOTS])

    lp = lam_ref[...]
    lam = (jnp.exp(jnp.sum(lp[0:1] * lp[1:2], axis=-1, keepdims=True))
           - jnp.exp(jnp.sum(lp[2:3] * lp[3:4], axis=-1, keepdims=True)) + lambda_init)

    def sub_layer_norm(o):
        d = o[0:TM] - lam * o[TM:2 * TM]
        y = d * lax.rsqrt(jnp.mean(d * d, axis=-1, keepdims=True) + EPS) * gain_ref[...]
        return y * (1.0 - lambda_init)
    _store_heads(o_ref, acc_ref, Q_TILES_PER_STEP, sub_layer_norm)


def _diff(lam_params, subln_gain, dq, dk, dv, n_ctx, tk, lambda_init):
    b, t_len, qw = dq.shape
    assert n_ctx == TM and (t_len - n_ctx) % (Q_TILES_PER_STEP * TM) == 0
    hw = LANE * DIFF_HEADS_PER_STEP
    return pl.pallas_call(
        functools.partial(_diff_kernel, tk=tk, lambda_init=lambda_init),
        grid=(b, qw // hw, (t_len - n_ctx) // (Q_TILES_PER_STEP * TM)),
        in_specs=([pl.BlockSpec(lam_params.shape, lambda b, h, s: (0, 0)),
                   pl.BlockSpec(subln_gain.shape, lambda b, h, s: (0, 0))]
                  + [pl.BlockSpec((1, TM, hw), lambda b, h, s, u=u: (b, 1 + Q_TILES_PER_STEP * s + u, h))
                     for u in range(Q_TILES_PER_STEP)]
                  + [pl.BlockSpec((1, t_len, hw), lambda b, h, s: (b, 0, h), pipeline_mode=pl.Buffered(1)),
                     pl.BlockSpec((1, t_len, 2 * hw), lambda b, h, s: (b, 0, h), pipeline_mode=pl.Buffered(1))]),
        out_specs=pl.BlockSpec((1, Q_TILES_PER_STEP * TM, hw), lambda b, h, s: (b, s, h)),
        out_shape=jax.ShapeDtypeStruct((b, t_len - n_ctx, qw), BF16),
        scratch_shapes=_flash_scratch(Q_TILES_PER_STEP * DIFF_HEADS_PER_STEP) + _pipeline_scratch(tk, Q_TILES_PER_STEP * DIFF_HEADS_PER_STEP),
        compiler_params=_params(3),
        name="diff_attention",
    )(lam_params, subln_gain, *([dq] * Q_TILES_PER_STEP), dk, dv)


def _ffn_kernel(*refs, n_att, ff_chunks, final):
    x_ref, mod_ref = refs[0], refs[1]
    att_refs = refs[2:2 + n_att]
    wo_refs = refs[2 + n_att:2 + 2 * n_att]
    wg_ref, wu_ref, wd_ref = refs[2 + 2 * n_att:5 + 2 * n_att]
    o_ref = refs[-1]
    mod = mod_ref[0, 0]
    y = _dot(att_refs[0][0], wo_refs[0][...])
    for a_ref, w_ref in zip(att_refs[1:], wo_refs[1:]):
        y = y + _dot(a_ref[0], w_ref[...])
    x1 = x_ref[0] + mod[2:3] * y
    h = _modulated_norm(x1, mod[3:4], mod[4:5]).astype(BF16)
    d_ff = wg_ref.shape[1]
    bounds = [0]
    for i in range(ff_chunks):
        bounds.append(min(d_ff, -(-(d_ff * (i + 1) // ff_chunks) // MXU_DEPTH) * MXU_DEPTH))
    down = None
    for lo, hi in zip(bounds[:-1], bounds[1:]):
        g = _dot(h, wg_ref[:, lo:hi])
        u = _dot(h, wu_ref[:, lo:hi])
        a = (g * (1.0 / (1.0 + jnp.exp(-g))) * u).astype(BF16)
        part = _dot(a, wd_ref[lo:hi, :])
        down = part if down is None else down + part
    x2 = x1 + mod[5:6] * down
    if final:
        gain_ref = refs[5 + 2 * n_att]
        x2 = x2 * lax.rsqrt(jnp.mean(x2 * x2, axis=-1, keepdims=True) + EPS) * gain_ref[...]
    o_ref[0] = x2


def _ffn(xa, mods, atts, wos, wg, wu, wd, final_gain=None):
    b, t_len, d = xa.shape
    n_rows = atts[0].shape[1]
    tile_off = (t_len - n_rows) // TM
    final = final_gain is not None
    in_specs = ([_token_spec(d, tile_off), _mod_spec(d, latent_only=tile_off > 0)]
                + [_token_spec(a.shape[2]) for a in atts]
                + [_const_spec(w.shape) for w in wos]
                + [_const_spec(wg.shape), _const_spec(wu.shape), _const_spec(wd.shape)])
    args = [xa, mods, *atts, *wos, wg, wu, wd]
    if final:
        in_specs.append(_const_spec(final_gain.shape))
        args.append(final_gain)
    return pl.pallas_call(
        functools.partial(_ffn_kernel, n_att=len(atts), ff_chunks=2, final=final),
        grid=(b, n_rows // TM),
        in_specs=in_specs,
        out_specs=_token_spec(d),
        out_shape=jax.ShapeDtypeStruct((b, n_rows, d), F32),
        compiler_params=_params(2),
        name="outproj_ffn_final" if final else "outproj_ffn",
    )(*args)


def _head_cols(base, heads):
    return np.concatenate([base + HEAD_DIM * h + _DEINT for h in heads])


def _rope_tables(seq, n_ctx):
    t = jnp.arange(seq)
    row = (t // GRID_W).astype(F32)
    col = (t % GRID_W).astype(F32)
    n_freq = HEAD_DIM // 4
    inv = ROPE_THETA ** (-jnp.arange(n_freq, dtype=F32) / n_freq)
    ang = jnp.concatenate([row[:, None] * inv, col[:, None] * inv], axis=-1)
    cos, sin = jnp.cos(ang), jnp.sin(ang)
    zero = jnp.zeros_like(sin)

    def table(first_half, second_half, ctx_value):
        lat = jnp.tile(jnp.concatenate([first_half, second_half], axis=-1), (1, LANE // HEAD_DIM))
        return jnp.concatenate([jnp.full((n_ctx, LANE), ctx_value, F32), lat], axis=0)

    return table(cos, cos, 1.0), table(zero, sin, 0.0), table(-sin, zero, 0.0)


def kernel(x, c, ctx, c_ctx, ada_w, ada_b, ffn_w_gate, ffn_w_up, ffn_w_down, par_w_in, par_w_out, na_rpb,
           gqa_q_gain, gqa_k_gain, diff_w_in, diff_w_out, diff_lambda_q1, diff_lambda_k1, diff_lambda_q2,
           diff_lambda_k2, diff_subln_gain, final_norm_gain):
    b, seq, d = x.shape
    n_ctx = ctx.shape[1]
    assert n_ctx == TM and seq % TM == 0 and d % LANE == 0 and b < 8 and ada_w.shape[0] == 2
    rows = seq // GRID_W
    scale = HEAD_DIM ** -0.5 * LOG2_E

    xa = jnp.concatenate([ctx, x], axis=1)
    cond = jnp.zeros((8, d), F32).at[:b].set(c).at[b].set(c_ctx)
    mods_all = _ada(cond, ada_w, ada_b).reshape(2, 8, N_MOD, d)

    def mods_of(layer):
        m = mods_all[layer]
        return jnp.stack([jnp.broadcast_to(m[b], (b, N_MOD, d)), m[:b]], axis=1)

    rope = _rope_tables(seq, n_ctx)

    cols0 = np.concatenate([np.arange(0, 1536), _head_cols(1536, _GQA_HEAD_ORDER), _head_cols(2048, (0, 1)),
                            np.arange(2176, 2304)])
    col_scale0 = np.ones((2304,), np.float32)
    col_scale0[0:512] = scale
    w_in0 = (par_w_in[0][:, cols0] * col_scale0).astype(BF16)
    gains = jnp.zeros((8, LANE), F32)
    gains = gains.at[0].set(jnp.tile(gqa_q_gain[0][_DEINT] * scale, 2)).at[1].set(jnp.tile(gqa_k_gain[0][_DEINT], 2))
    block_mean = jnp.asarray(np.kron(np.eye(LANE // HEAD_DIM), np.full((HEAD_DIM, HEAD_DIM), 1.0 / HEAD_DIM)), BF16)
    naq, nak, nav, gq, gk, gv = _proj_par(xa, mods_of(0), w_in0, rope, gains, block_mean)
    att_na = _na(naq, nak, nav, _na_bias_table(na_rpb[0], rows), n_ctx)
    att_g = _gqa(gq, gk, gv, n_ctx, tk=_key_chunk(n_ctx + seq))
    wo_na = par_w_out[0][0:512].astype(BF16)
    wo_g = par_w_out[0][512 + np.concatenate([HEAD_DIM * h + np.arange(HEAD_DIM) for h in _GQA_HEAD_ORDER])].astype(BF16)
    xa = _ffn(xa, mods_of(0), [att_na, att_g], [wo_na, wo_g],
              ffn_w_gate[0].astype(BF16), ffn_w_up[0].astype(BF16), ffn_w_down[0].astype(BF16))

    cols1 = np.concatenate([_head_cols(0, range(16)), _head_cols(1024, range(16)), np.arange(2048, 3072)])
    col_scale1 = np.ones((3072,), np.float32)
    col_scale1[0:1024] = scale
    w_in1 = (diff_w_in[0][:, cols1] * col_scale1).astype(BF16)
    dq, dk, dv = _proj_diff(xa, mods_of(1), w_in1, rope)
    lambda_init = 0.8 - 0.6 * float(np.exp(-0.3 * 1))
    lam_params = jnp.stack([diff_lambda_q1[0], diff_lambda_k1[0], diff_lambda_q2[0], diff_lambda_k2[0]]).astype(F32)
    att_d = _diff(lam_params, diff_subln_gain[0].reshape(1, -1).astype(F32), dq, dk, dv, n_ctx,
                  tk=_key_chunk(n_ctx + seq),
                  lambda_init=lambda_init)
    return _ffn(xa, mods_of(1), [att_d], [diff_w_out[0].astype(BF16)],
                ffn_w_gate[1].astype(BF16), ffn_w_up[1].astype(BF16), ffn_w_down[1].astype(BF16),
                final_gain=final_norm_gain.reshape(1, -1).astype(F32))
```

```python
import functools

import numpy as np
import jax
import jax.numpy as jnp
from jax import lax
from jax.experimental import pallas as pl
from jax.experimental.pallas import tpu as pltpu

F32 = jnp.float32
BF16 = jnp.bfloat16

GRID_W = 64
HEAD_DIM = 64
WIN_R = 8
WIN_C = 16
N_MOD = 6
ROPE_THETA = 10000.0
EPS = 1e-6

LANE = 128
TM = 256
NA_ROWS_PER_TILE = TM // GRID_W
NA_KEY_ROWS = 12
NA_WIN = NA_KEY_ROWS * GRID_W
ROW_BLOCK = 128
DIFF_HEADS_PER_STEP = 2
GQA_Q_TILES = 2
DIFF_Q_TILES = 4
SOFTMAX_ROWS = 32
LOG2_E = 1.4426950408889634
MXU_DEPTH = 256
NEG = -1e30
VMEM_LIMIT = 56 * 1024 * 1024

_DEINT = np.concatenate([np.arange(0, HEAD_DIM, 2), np.arange(1, HEAD_DIM, 2)])
_GQA_HEAD_ORDER = (0, 4, 1, 5, 2, 6, 3, 7)


def _dot(a, b):
    return jnp.dot(a, b, preferred_element_type=F32)


def _dot_nt(a, b):
    return lax.dot_general(a, b, (((1,), (1,)), ((), ())), preferred_element_type=F32)


def _params(n_grid):
    return pltpu.CompilerParams(dimension_semantics=("arbitrary",) * n_grid, vmem_limit_bytes=VMEM_LIMIT)


def _const_spec(shape):
    return pl.BlockSpec(shape, lambda *_: (0,) * len(shape), pipeline_mode=pl.Buffered(1))


def _split_bf16(a):
    hi = a.astype(BF16)
    return hi, (a - hi.astype(F32)).astype(BF16)


def _ada_kernel(cond_ref, w_ref, b_ref, o_ref):
    c = cond_ref[...]
    a_hi, a_lo = _split_bf16(c * (1.0 / (1.0 + jnp.exp(-c))))
    w_hi, w_lo = _split_bf16(w_ref[0])
    o_ref[0] = _dot(a_hi, w_hi) + _dot(a_lo, w_hi) + _dot(a_hi, w_lo) + b_ref[0]


def _ada(cond, ada_w, ada_b):
    depth, d, n = ada_w.shape
    tn = 1024
    return pl.pallas_call(
        _ada_kernel,
        grid=(depth, n // tn),
        in_specs=[pl.BlockSpec(cond.shape, lambda l, j: (0, 0)),
                  pl.BlockSpec((1, d, tn), lambda l, j: (l, 0, j)),
                  pl.BlockSpec((1, 1, tn), lambda l, j: (l, 0, j))],
        out_specs=pl.BlockSpec((1, cond.shape[0], tn), lambda l, j: (l, 0, j)),
        out_shape=jax.ShapeDtypeStruct((depth, cond.shape[0], n), F32),
        compiler_params=_params(2),
        name="ada_modulation",
    )(cond, ada_w, ada_b.reshape(depth, 1, n))


def _modulated_norm(x, shift, scale):
    ms = jnp.mean(x * x, axis=-1, keepdims=True)
    return (x * lax.rsqrt(ms + EPS)) * (1.0 + scale) + shift


def _group_mean_sq(x, bd):
    hi, lo = _split_bf16(x * x)
    return _dot(hi, bd) + _dot(lo, bd)


def _rope(x, c, sa, sb):
    return x * c + pltpu.roll(x, 32, 1) * sa + pltpu.roll(x, 96, 1) * sb


def _ones_column(rows):
    return (lax.broadcasted_iota(jnp.int32, (rows, LANE), 1) == 0).astype(BF16)


def _store_values(v_ref, p, col0, n_blocks):
    ones = _ones_column(p.shape[0])
    for j in range(n_blocks):
        v_ref[0, :, 2 * LANE * j:2 * LANE * j + LANE] = p[:, col0 + LANE * j:col0 + LANE * (j + 1)].astype(BF16)
        v_ref[0, :, 2 * LANE * j + LANE:2 * LANE * (j + 1)] = ones


def _token_tile(ctx_ref, x_ref):
    return jnp.where(pl.program_id(1) == 0, ctx_ref[0], x_ref[0])


def _proj_par_kernel(ctx_ref, x_ref, mod_ref, w_ref, c_ref, sa_ref, sb_ref, gain_ref, bd_ref,
                     naq_ref, nak_ref, nav_ref, gq_ref, gk_ref, gv_ref):
    mod = mod_ref[0, 0]
    h = _modulated_norm(_token_tile(ctx_ref, x_ref), mod[0:1], mod[1:2]).astype(BF16)
    p = _dot(h, w_ref[...])
    naq_ref[0] = p[:, 0:512].astype(BF16)
    nak_ref[0] = p[:, 512:1024].astype(BF16)
    _store_values(nav_ref, p, 1024, 4)
    c, sa, sb, bd = c_ref[...], sa_ref[...], sb_ref[...], bd_ref[...]

    def qk_norm_rope(g, gain):
        g = g * lax.rsqrt(_group_mean_sq(g, bd) + EPS) * gain
        return _rope(g, c, sa, sb).astype(BF16)

    for j in range(4):
        gq_ref[0, :, LANE * j:LANE * (j + 1)] = qk_norm_rope(p[:, 1536 + LANE * j:1536 + LANE * (j + 1)], gain_ref[0:1, :])
    gk_ref[0] = qk_norm_rope(p[:, 2048:2176], gain_ref[1:2, :])
    _store_values(gv_ref, p, 2176, 1)


def _proj_diff_kernel(x_ref, mod_ref, w_ref, c_ref, sa_ref, sb_ref, dq_ref, dk_ref, dv_ref):
    mod = mod_ref[0, 0]
    h = _modulated_norm(x_ref[0], mod[0:1], mod[1:2]).astype(BF16)
    p = _dot(h, w_ref[...])
    c, sa, sb = c_ref[...], sa_ref[...], sb_ref[...]
    for j in range(8):
        dq_ref[0, :, LANE * j:LANE * (j + 1)] = _rope(p[:, LANE * j:LANE * (j + 1)], c, sa, sb).astype(BF16)
        dk_ref[0, :, LANE * j:LANE * (j + 1)] = _rope(p[:, 1024 + LANE * j:1024 + LANE * (j + 1)], c, sa, sb).astype(BF16)
    _store_values(dv_ref, p, 2048, 8)


def _token_spec(width, tile_off=0):
    return pl.BlockSpec((1, TM, width), lambda b, t: (b, t + tile_off, 0))


def _split_token_specs(d):
    return [pl.BlockSpec((1, TM, d), lambda b, t: (b, 0, 0)),
            pl.BlockSpec((1, TM, d), lambda b, t: (b, jnp.maximum(t - 1, 0), 0))]


def _mod_spec(d, latent_only=False):
    if latent_only:
        return pl.BlockSpec((1, 1, N_MOD, d), lambda b, t: (b, 1, 0, 0))
    return pl.BlockSpec((1, 1, N_MOD, d), lambda b, t: (b, jnp.minimum(t, 1), 0, 0))


def _rope_spec():
    return pl.BlockSpec((TM, LANE), lambda b, t: (t, 0))


def _proj_par(ctx, x, mods, w, rope, gains, bd):
    b, seq, d = x.shape
    t_len = ctx.shape[1] + seq
    widths = (512, 512, 1024, 512, 128, 256)
    return pl.pallas_call(
        _proj_par_kernel,
        grid=(b, t_len // TM),
        in_specs=_split_token_specs(d) + [_mod_spec(d), _const_spec(w.shape), _rope_spec(), _rope_spec(), _rope_spec(),
                                          _const_spec(gains.shape), _const_spec(bd.shape)],
        out_specs=[_token_spec(n) for n in widths],
        out_shape=[jax.ShapeDtypeStruct((b, t_len, n), BF16) for n in widths],
        compiler_params=_params(2),
        name="proj_parallel_mixer",
    )(ctx, x, mods, w, *rope, gains, bd)


def _proj_diff(xa, mods, w, rope):
    b, t_len, d = xa.shape
    widths = (1024, 1024, 2048)
    return pl.pallas_call(
        _proj_diff_kernel,
        grid=(b, t_len // TM),
        in_specs=[_token_spec(d), _mod_spec(d), _const_spec(w.shape), _rope_spec(), _rope_spec(), _rope_spec()],
        out_specs=[_token_spec(n) for n in widths],
        out_shape=[jax.ShapeDtypeStruct((b, t_len, n), BF16) for n in widths],
        compiler_params=_params(2),
        name="proj_diff_mixer",
    )(xa, mods, w, *rope)


def _split_heads(q):
    lane = lax.broadcasted_iota(jnp.int32, q.shape, 1)
    zero = jnp.zeros_like(q)
    return jnp.concatenate([jnp.where(lane < HEAD_DIM, q, zero), jnp.where(lane >= HEAD_DIM, q, zero)], axis=0)


def _softmax_update(s, m_prev):
    m_new = jnp.maximum(m_prev, jnp.max(s, axis=-1, keepdims=True))
    p = jnp.exp2(s - m_new)
    return p, jnp.exp2(m_prev - m_new), m_new


def _flash_chunk(q2_ref, k, v, m_ref, acc_ref, row0, bias_of=None):
    for r in range(2 * TM // ROW_BLOCK):
        rows = slice(row0 + ROW_BLOCK * r, row0 + ROW_BLOCK * (r + 1))
        s = _dot_nt(q2_ref[rows], k)
        if bias_of is not None:
            s = s + bias_of(ROW_BLOCK * r, ROW_BLOCK)
        p, alpha, m_new = _softmax_update(s, m_ref[rows])
        acc_ref[rows] = alpha * acc_ref[rows] + _dot(p.astype(BF16), v)
        m_ref[rows] = m_new


def _flash_pipeline(q2_ref, k_of, v_of, n_chunks, n_blocks, m_ref, acc_ref, s_refs, p_refs, alpha_refs):
    assert n_blocks % 2 == 0
    r2 = 2 * TM

    def score(i, j, slot):
        s_refs[slot][...] = _dot_nt(q2_ref[r2 * j:r2 * (j + 1)], k_of(i, j))

    def softmax(j, slot):
        for r in range(r2 // SOFTMAX_ROWS):
            rows = slice(SOFTMAX_ROWS * r, SOFTMAX_ROWS * (r + 1))
            state_rows = slice(r2 * j + SOFTMAX_ROWS * r, r2 * j + SOFTMAX_ROWS * (r + 1))
            p, alpha, m_new = _softmax_update(s_refs[slot][rows], m_ref[state_rows])
            p_refs[slot][rows] = p.astype(BF16)
            alpha_refs[slot][rows] = alpha
            m_ref[state_rows] = m_new

    def accumulate(i, j, slot):
        blk = slice(r2 * j, r2 * (j + 1))
        acc_ref[blk] = alpha_refs[slot][...] * acc_ref[blk] + _dot(p_refs[slot][...], v_of(i, j))

    last_slot = (n_blocks - 1) % 2
    p_refs[last_slot][...] = jnp.zeros(p_refs[last_slot].shape, BF16)
    alpha_refs[last_slot][...] = jnp.ones(alpha_refs[last_slot].shape, F32)
    score(0, 0, 0)

    def body(i, carry):
        for j in range(n_blocks):
            if j + 1 < n_blocks:
                score(i, j + 1, (j + 1) % 2)
            else:
                score(jnp.minimum(i + 1, n_chunks - 1), 0, 0)
            softmax(j, j % 2)
            if j > 0:
                accumulate(i, j - 1, (j - 1) % 2)
            else:
                accumulate(jnp.maximum(i - 1, 0), n_blocks - 1, last_slot)
        return carry

    lax.fori_loop(0, n_chunks, body, 0)
    accumulate(n_chunks - 1, n_blocks - 1, last_slot)


def _pipeline_scratch(tk):
    rows = 2 * TM
    return [pltpu.VMEM((rows, tk), F32)] * 2 + [pltpu.VMEM((rows, tk), BF16)] * 2 + [pltpu.VMEM((rows, 1), F32)] * 2


def _normalize(acc):
    return acc[:, 0:LANE] / acc[:, LANE:LANE + 1]


def _merge_heads(o):
    lane = lax.broadcasted_iota(jnp.int32, (TM, LANE), 1)
    return jnp.where(lane < HEAD_DIM, o[0:TM], o[TM:2 * TM])


def _init_flash(q_refs, q2_ref, m_ref, acc_ref):
    n = 0
    for q_ref in q_refs:
        for j in range(q_ref.shape[2] // LANE):
            q2_ref[2 * TM * n:2 * TM * (n + 1)] = _split_heads(q_ref[0, :, LANE * j:LANE * (j + 1)])
            n += 1
    m_ref[...] = jnp.full(m_ref.shape, NEG, F32)
    acc_ref[...] = jnp.zeros(acc_ref.shape, F32)


def _flash_scratch(n_blocks):
    rows = 2 * TM * n_blocks
    return [pltpu.VMEM((rows, LANE), BF16), pltpu.VMEM((rows, 1), F32), pltpu.VMEM((rows, 2 * LANE), F32)]


def _key_chunk(t_len):
    return next(tk for tk in (768, 640, 512, 384, 256) if t_len % tk == 0)


def _store_heads(o_ref, acc_ref, n_tiles, finish):
    n_lane_blocks = o_ref.shape[2] // LANE
    for u in range(n_tiles):
        for j in range(n_lane_blocks):
            n = u * n_lane_blocks + j
            o = _normalize(acc_ref[2 * TM * n:2 * TM * (n + 1)])
            o_ref[0, TM * u:TM * (u + 1), LANE * j:LANE * (j + 1)] = finish(o).astype(BF16)


def _gqa_ctx_kernel(q_ref, k_ref, v_ref, o_ref, q2_ref, m_ref, acc_ref):
    _init_flash([q_ref], q2_ref, m_ref, acc_ref)
    for j in range(q_ref.shape[2] // LANE):
        _flash_chunk(q2_ref, k_ref[0], v_ref[0], m_ref, acc_ref, 2 * TM * j)
    _store_heads(o_ref, acc_ref, 1, _merge_heads)


def _gqa_kernel(*refs, tk):
    q_refs = refs[:GQA_Q_TILES]
    k_ref, v_ref, o_ref, q2_ref, m_ref, acc_ref = refs[GQA_Q_TILES:GQA_Q_TILES + 6]
    pipe_refs = refs[GQA_Q_TILES + 6:]
    _init_flash(q_refs, q2_ref, m_ref, acc_ref)

    def chunk_of(ref):
        return lambda i, n: ref[0, pl.ds(pl.multiple_of(i * tk, tk), tk), :]
    _flash_pipeline(q2_ref, chunk_of(k_ref), chunk_of(v_ref), k_ref.shape[1] // tk,
                    GQA_Q_TILES * (o_ref.shape[2] // LANE), m_ref, acc_ref,
                    pipe_refs[0:2], pipe_refs[2:4], pipe_refs[4:6])
    _store_heads(o_ref, acc_ref, GQA_Q_TILES, _merge_heads)


def _gqa(gq, gk, gv, n_ctx, tk):
    b, t_len, qw = gq.shape
    assert n_ctx == TM and (t_len - n_ctx) % (GQA_Q_TILES * TM) == 0
    n_lane_blocks = qw // LANE
    ctx_out = pl.pallas_call(
        _gqa_ctx_kernel,
        grid=(b,),
        in_specs=[pl.BlockSpec((1, TM, qw), lambda b: (b, 0, 0)),
                  pl.BlockSpec((1, TM, LANE), lambda b: (b, 0, 0)),
                  pl.BlockSpec((1, TM, 2 * LANE), lambda b: (b, 0, 0))],
        out_specs=pl.BlockSpec((1, TM, qw), lambda b: (b, 0, 0)),
        out_shape=jax.ShapeDtypeStruct((b, TM, qw), BF16),
        scratch_shapes=_flash_scratch(n_lane_blocks),
        compiler_params=_params(1),
        name="gqa_attention_ctx",
    )(gq, gk, gv)
    latent_out = pl.pallas_call(
        functools.partial(_gqa_kernel, tk=tk),
        grid=(b, (t_len - n_ctx) // (GQA_Q_TILES * TM)),
        in_specs=([pl.BlockSpec((1, TM, qw), lambda b, s, u=u: (b, 1 + GQA_Q_TILES * s + u, 0))
                   for u in range(GQA_Q_TILES)]
                  + [pl.BlockSpec((1, t_len, LANE), lambda b, s: (b, 0, 0)),
                     pl.BlockSpec((1, t_len, 2 * LANE), lambda b, s: (b, 0, 0))]),
        out_specs=pl.BlockSpec((1, GQA_Q_TILES * TM, qw), lambda b, s: (b, s, 0)),
        out_shape=jax.ShapeDtypeStruct((b, t_len - n_ctx, qw), BF16),
        scratch_shapes=_flash_scratch(GQA_Q_TILES * n_lane_blocks) + _pipeline_scratch(tk),
        compiler_params=_params(2),
        name="gqa_attention",
    )(*([gq] * GQA_Q_TILES), gk, gv)
    return jnp.concatenate([ctx_out, latent_out], axis=1)


def _na_kernel(q_ref, k_ref, v_ref, bias_ref, o_ref, q2_ref, m_ref, acc_ref, *pipe_refs, n_ctx, rows):
    t = pl.program_id(1)
    n_blocks = q_ref.shape[2] // LANE
    r2 = 2 * TM
    s_refs, p_refs = pipe_refs[0:2], pipe_refs[2:4]
    _init_flash([q_ref], q2_ref, m_ref, acc_ref)

    @pl.when(t == 0)
    def _():
        for j in range(n_blocks):
            _flash_chunk(q2_ref, k_ref[0, 0:n_ctx, LANE * j:LANE * (j + 1)],
                         v_ref[0, 0:n_ctx, 2 * LANE * j:2 * LANE * (j + 1)], m_ref, acc_ref, r2 * j)
            o_ref[0, :, LANE * j:LANE * (j + 1)] = _merge_heads(_normalize(acc_ref[r2 * j:r2 * (j + 1)])).astype(BF16)

    @pl.when(t > 0)
    def _():
        first_row = NA_ROWS_PER_TILE * (t - 1)
        start = jnp.clip(first_row - WIN_R // 2, 0, rows - NA_KEY_ROWS)
        off = pl.multiple_of(n_ctx + GRID_W * start, GRID_W)

        def score(j, slot):
            q2 = q2_ref[r2 * j:r2 * (j + 1)]
            s_refs[slot][:, 0:n_ctx] = _dot_nt(q2, k_ref[0, 0:n_ctx, LANE * j:LANE * (j + 1)])
            s_refs[slot][:, n_ctx:] = _dot_nt(q2, k_ref[0, pl.ds(off, NA_WIN), LANE * j:LANE * (j + 1)])

        def softmax(j, slot):
            for r in range(r2 // SOFTMAX_ROWS):
                rows_r = slice(SOFTMAX_ROWS * r, SOFTMAX_ROWS * (r + 1))
                head, row = divmod(SOFTMAX_ROWS * r, TM)
                s_ctx = s_refs[slot][rows_r, 0:n_ctx]
                s_win = s_refs[slot][rows_r, n_ctx:] + bias_ref[2 * j + head, 0, row:row + SOFTMAX_ROWS, :]
                m = jnp.maximum(jnp.max(s_ctx, axis=-1, keepdims=True), jnp.max(s_win, axis=-1, keepdims=True))
                p_refs[slot][rows_r, 0:n_ctx] = jnp.exp2(s_ctx - m).astype(BF16)
                p_refs[slot][rows_r, n_ctx:] = jnp.exp2(s_win - m).astype(BF16)

        def output(j, slot):
            acc = (_dot(p_refs[slot][:, 0:n_ctx], v_ref[0, 0:n_ctx, 2 * LANE * j:2 * LANE * (j + 1)])
                   + _dot(p_refs[slot][:, n_ctx:], v_ref[0, pl.ds(off, NA_WIN), 2 * LANE * j:2 * LANE * (j + 1)]))
            o_ref[0, :, LANE * j:LANE * (j + 1)] = _merge_heads(_normalize(acc)).astype(BF16)

        score(0, 0)
        for j in range(n_blocks):
            if j + 1 < n_blocks:
                score(j + 1, (j + 1) % 2)
            softmax(j, j % 2)
            if j > 0:
                output(j - 1, (j - 1) % 2)
        output(n_blocks - 1, (n_blocks - 1) % 2)


def _na(naq, nak, nav, bias, n_ctx):
    b, t_len, qw = naq.shape
    n_tiles = t_len // TM
    rows = (t_len - n_ctx) // GRID_W
    n_keys = n_ctx + NA_WIN

    def bias_index(b, t):
        return (0, jnp.where(t <= 1, 0, jnp.where(t == n_tiles - 1, 2, 1)), 0, 0)

    def resident(width):
        return pl.BlockSpec((1, t_len, width), lambda b, t: (b, 0, 0), pipeline_mode=pl.Buffered(1))

    return pl.pallas_call(
        functools.partial(_na_kernel, n_ctx=n_ctx, rows=rows),
        grid=(b, n_tiles),
        in_specs=[pl.BlockSpec((1, TM, qw), lambda b, t: (b, t, 0)), resident(qw), resident(2 * qw),
                  pl.BlockSpec((bias.shape[0], 1, TM, NA_WIN), bias_index, pipeline_mode=pl.Buffered(1))],
        out_specs=pl.BlockSpec((1, TM, qw), lambda b, t: (b, t, 0)),
        out_shape=jax.ShapeDtypeStruct((b, t_len, qw), BF16),
        scratch_shapes=(_flash_scratch(qw // LANE)
                        + [pltpu.VMEM((2 * TM, n_keys), F32)] * 2 + [pltpu.VMEM((2 * TM, n_keys), BF16)] * 2),
        compiler_params=_params(2),
        name="neighbourhood_attention",
    )(naq, nak, nav, bias)


def _na_bias_table(rpb, rows):
    g_of_pattern = np.array([0, 2, rows // NA_ROWS_PER_TILE - 1])
    a = np.arange(NA_ROWS_PER_TILE)
    r = NA_ROWS_PER_TILE * g_of_pattern[:, None] + a[None, :]
    start = np.clip(NA_ROWS_PER_TILE * g_of_pattern - WIN_R // 2, 0, rows - NA_KEY_ROWS)
    rs = np.clip(r - WIN_R // 2, 0, rows - WIN_R)
    key_row = start[:, None] + np.arange(NA_KEY_ROWS)[None, :]
    row_ok = (key_row[:, None, :] >= rs[:, :, None]) & (key_row[:, None, :] < rs[:, :, None] + WIN_R)
    row_off = np.clip(key_row[:, None, :] - r[:, :, None] + (WIN_R - 1), 0, 2 * WIN_R - 2)
    cols = np.arange(GRID_W)
    col_start = np.clip(cols - WIN_C // 2, 0, GRID_W - WIN_C)
    col_ok = (cols[None, :] >= col_start[:, None]) & (cols[None, :] < col_start[:, None] + WIN_C)
    col_off = np.clip(cols[None, :] - cols[:, None] + (WIN_C - 1), 0, 2 * WIN_C - 2)
    ok = row_ok[:, :, None, :, None] & col_ok[None, None, :, None, :]
    bias_rows = rpb.astype(F32)[:, row_off]
    pick_col = (col_off[:, :, None] == np.arange(2 * WIN_C - 1)).astype(np.float32)
    vals = jnp.einsum('hpakb,cjb->hpackj', bias_rows, pick_col, precision=lax.Precision.HIGHEST)
    table = jnp.where(ok[None], vals * LOG2_E, NEG)
    return table.reshape(rpb.shape[0], 3, TM, NA_WIN)


def _diff_kernel(lam_ref, gain_ref, *refs, tk, lambda_init):
    q_refs = refs[:DIFF_Q_TILES]
    k_ref, v_ref, o_ref, q2_ref, m_ref, acc_ref = refs[DIFF_Q_TILES:DIFF_Q_TILES + 6]
    pipe_refs = refs[DIFF_Q_TILES + 6:]
    n_heads = o_ref.shape[2] // LANE
    _init_flash(q_refs, q2_ref, m_ref, acc_ref)

    def chunk_of(ref, width):
        def chunk(i, n):
            h = n % n_heads
            return ref[0, pl.ds(pl.multiple_of(i * tk, tk), tk), width * h:width * (h + 1)]
        return chunk
    _flash_pipeline(q2_ref, chunk_of(k_ref, LANE), chunk_of(v_ref, 2 * LANE), k_ref.shape[1] // tk,
                    DIFF_Q_TILES * n_heads, m_ref, acc_ref, pipe_refs[0:2], pipe_refs[2:4], pipe_refs[4:6])

    lp = lam_ref[...]
    lam = (jnp.exp(jnp.sum(lp[0:1] * lp[1:2], axis=-1, keepdims=True))
           - jnp.exp(jnp.sum(lp[2:3] * lp[3:4], axis=-1, keepdims=True)) + lambda_init)

    def sub_layer_norm(o):
        d = o[0:TM] - lam * o[TM:2 * TM]
        y = d * lax.rsqrt(jnp.mean(d * d, axis=-1, keepdims=True) + EPS) * gain_ref[...]
        return y * (1.0 - lambda_init)
    _store_heads(o_ref, acc_ref, DIFF_Q_TILES, sub_layer_norm)


def _diff(lam_params, subln_gain, dq, dk, dv, n_ctx, tk, lambda_init):
    b, t_len, qw = dq.shape
    assert n_ctx == TM and (t_len - n_ctx) % (DIFF_Q_TILES * TM) == 0
    hw = LANE * DIFF_HEADS_PER_STEP
    return pl.pallas_call(
        functools.partial(_diff_kernel, tk=tk, lambda_init=lambda_init),
        grid=(b, qw // hw, (t_len - n_ctx) // (DIFF_Q_TILES * TM)),
        in_specs=([pl.BlockSpec(lam_params.shape, lambda b, h, s: (0, 0)),
                   pl.BlockSpec(subln_gain.shape, lambda b, h, s: (0, 0))]
                  + [pl.BlockSpec((1, TM, hw), lambda b, h, s, u=u: (b, 1 + DIFF_Q_TILES * s + u, h))
                     for u in range(DIFF_Q_TILES)]
                  + [pl.BlockSpec((1, t_len, hw), lambda b, h, s: (b, 0, h)),
                     pl.BlockSpec((1, t_len, 2 * hw), lambda b, h, s: (b, 0, h))]),
        out_specs=pl.BlockSpec((1, DIFF_Q_TILES * TM, hw), lambda b, h, s: (b, s, h)),
        out_shape=jax.ShapeDtypeStruct((b, t_len - n_ctx, qw), BF16),
        scratch_shapes=_flash_scratch(DIFF_Q_TILES * DIFF_HEADS_PER_STEP) + _pipeline_scratch(tk),
        compiler_params=_params(3),
        name="diff_attention",
    )(lam_params, subln_gain, *([dq] * DIFF_Q_TILES), dk, dv)


def _ffn_kernel(*refs, n_att, ff_chunks, final, split_tokens):
    if split_tokens:
        x_tile = _token_tile(refs[0], refs[1])
        refs = refs[1:]
    else:
        x_tile = refs[0][0]
    mod_ref = refs[1]
    att_refs = refs[2:2 + n_att]
    wo_refs = refs[2 + n_att:2 + 2 * n_att]
    wg_ref, wu_ref, wd_ref = refs[2 + 2 * n_att:5 + 2 * n_att]
    o_ref = refs[-1]
    mod = mod_ref[0, 0]
    y = _dot(att_refs[0][0], wo_refs[0][...])
    for a_ref, w_ref in zip(att_refs[1:], wo_refs[1:]):
        y = y + _dot(a_ref[0], w_ref[...])
    x1 = x_tile + mod[2:3] * y
    h = _modulated_norm(x1, mod[3:4], mod[4:5]).astype(BF16)
    d_ff = wg_ref.shape[1]
    bounds = [0]
    for i in range(ff_chunks):
        bounds.append(min(d_ff, -(-(d_ff * (i + 1) // ff_chunks) // MXU_DEPTH) * MXU_DEPTH))
    down = None
    for lo, hi in zip(bounds[:-1], bounds[1:]):
        g = _dot(h, wg_ref[:, lo:hi])
        u = _dot(h, wu_ref[:, lo:hi])
        a = (g * (1.0 / (1.0 + jnp.exp(-g))) * u).astype(BF16)
        part = _dot(a, wd_ref[lo:hi, :])
        down = part if down is None else down + part
    x2 = x1 + mod[5:6] * down
    if final:
        gain_ref = refs[5 + 2 * n_att]
        x2 = x2 * lax.rsqrt(jnp.mean(x2 * x2, axis=-1, keepdims=True) + EPS) * gain_ref[...]
    o_ref[0] = x2


def _ffn(tokens, mods, atts, wos, wg, wu, wd, final_gain=None):
    split_tokens = isinstance(tokens, tuple)
    n_rows = atts[0].shape[1]
    b, _, d = tokens[-1].shape if split_tokens else tokens.shape
    final = final_gain is not None
    if split_tokens:
        token_specs = _split_token_specs(d)
        tokens = list(tokens)
    else:
        token_specs = [_token_spec(d, (tokens.shape[1] - n_rows) // TM)]
        tokens = [tokens]
    in_specs = (token_specs + [_mod_spec(d, latent_only=not split_tokens)]
                + [_token_spec(a.shape[2]) for a in atts]
                + [_const_spec(w.shape) for w in wos]
                + [_const_spec(wg.shape), _const_spec(wu.shape), _const_spec(wd.shape)])
    args = [*tokens, mods, *atts, *wos, wg, wu, wd]
    if final:
        in_specs.append(_const_spec(final_gain.shape))
        args.append(final_gain)
    return pl.pallas_call(
        functools.partial(_ffn_kernel, n_att=len(atts), ff_chunks=2, final=final, split_tokens=split_tokens),
        grid=(b, n_rows // TM),
        in_specs=in_specs,
        out_specs=_token_spec(d),
        out_shape=jax.ShapeDtypeStruct((b, n_rows, d), F32),
        compiler_params=_params(2),
        name="outproj_ffn_final" if final else "outproj_ffn",
    )(*args)


def _head_cols(base, heads):
    return np.concatenate([base + HEAD_DIM * h + _DEINT for h in heads])


def _rope_tables(seq, n_ctx):
    t = jnp.arange(seq)
    row = (t // GRID_W).astype(F32)
    col = (t % GRID_W).astype(F32)
    n_freq = HEAD_DIM // 4
    inv = ROPE_THETA ** (-jnp.arange(n_freq, dtype=F32) / n_freq)
    ang = jnp.concatenate([row[:, None] * inv, col[:, None] * inv], axis=-1)
    cos, sin = jnp.cos(ang), jnp.sin(ang)
    zero = jnp.zeros_like(sin)

    def table(first_half, second_half, ctx_value):
        lat = jnp.tile(jnp.concatenate([first_half, second_half], axis=-1), (1, LANE // HEAD_DIM))
        return jnp.concatenate([jnp.full((n_ctx, LANE), ctx_value, F32), lat], axis=0)

    return table(cos, cos, 1.0), table(zero, sin, 0.0), table(-sin, zero, 0.0)


def kernel(x, c, ctx, c_ctx, ada_w, ada_b, ffn_w_gate, ffn_w_up, ffn_w_down, par_w_in, par_w_out, na_rpb,
           gqa_q_gain, gqa_k_gain, diff_w_in, diff_w_out, diff_lambda_q1, diff_lambda_k1, diff_lambda_q2,
           diff_lambda_k2, diff_subln_gain, final_norm_gain):
    b, seq, d = x.shape
    n_ctx = ctx.shape[1]
    assert n_ctx == TM and seq % TM == 0 and d % LANE == 0 and b < 8 and ada_w.shape[0] == 2
    rows = seq // GRID_W
    scale = HEAD_DIM ** -0.5 * LOG2_E

    cond = jnp.zeros((8, d), F32).at[:b].set(c).at[b].set(c_ctx)
    mods_all = _ada(cond, ada_w, ada_b).reshape(2, 8, N_MOD, d)

    def mods_of(layer):
        m = mods_all[layer]
        return jnp.stack([jnp.broadcast_to(m[b], (b, N_MOD, d)), m[:b]], axis=1)

    rope = _rope_tables(seq, n_ctx)

    cols0 = np.concatenate([np.arange(0, 1536), _head_cols(1536, _GQA_HEAD_ORDER), _head_cols(2048, (0, 1)),
                            np.arange(2176, 2304)])
    col_scale0 = np.ones((2304,), np.float32)
    col_scale0[0:512] = scale
    w_in0 = (par_w_in[0][:, cols0] * col_scale0).astype(BF16)
    gains = jnp.zeros((8, LANE), F32)
    gains = gains.at[0].set(jnp.tile(gqa_q_gain[0][_DEINT] * scale, 2)).at[1].set(jnp.tile(gqa_k_gain[0][_DEINT], 2))
    block_mean = jnp.asarray(np.kron(np.eye(LANE // HEAD_DIM), np.full((HEAD_DIM, HEAD_DIM), 1.0 / HEAD_DIM)), BF16)
    naq, nak, nav, gq, gk, gv = _proj_par(ctx, x, mods_of(0), w_in0, rope, gains, block_mean)
    att_na = _na(naq, nak, nav, _na_bias_table(na_rpb[0], rows), n_ctx)
    att_g = _gqa(gq, gk, gv, n_ctx, tk=_key_chunk(n_ctx + seq))
    wo_na = par_w_out[0][0:512].astype(BF16)
    wo_g = par_w_out[0][512 + np.concatenate([HEAD_DIM * h + np.arange(HEAD_DIM) for h in _GQA_HEAD_ORDER])].astype(BF16)
    xa = _ffn((ctx, x), mods_of(0), [att_na, att_g], [wo_na, wo_g],
              ffn_w_gate[0].astype(BF16), ffn_w_up[0].astype(BF16), ffn_w_down[0].astype(BF16))

    cols1 = np.concatenate([_head_cols(0, range(16)), _head_cols(1024, range(16)), np.arange(2048, 3072)])
    col_scale1 = np.ones((3072,), np.float32)
    col_scale1[0:1024] = scale
    w_in1 = (diff_w_in[0][:, cols1] * col_scale1).astype(BF16)
    dq, dk, dv = _proj_diff(xa, mods_of(1), w_in1, rope)
    lambda_init = 0.8 - 0.6 * float(np.exp(-0.3 * 1))
    lam_params = jnp.stack([diff_lambda_q1[0], diff_lambda_k1[0], diff_lambda_q2[0], diff_lambda_k2[0]]).astype(F32)
    att_d = _diff(lam_params, diff_subln_gain[0].reshape(1, -1).astype(F32), dq, dk, dv, n_ctx,
                  tk=_key_chunk(n_ctx + seq),
                  lambda_init=lambda_init)
    return _ffn(xa, mods_of(1), [att_d], [diff_w_out[0].astype(BF16)],
                ffn_w_gate[1].astype(BF16), ffn_w_up[1].astype(BF16), ffn_w_down[1].astype(BF16),
                final_gain=final_norm_gain.reshape(1, -1).astype(F32))
```

```python
import functools

import numpy as np
import jax
import jax.numpy as jnp
from jax import lax
from jax.experimental import pallas as pl
from jax.experimental.pallas import tpu as pltpu

F32 = jnp.float32
BF16 = jnp.bfloat16

GRID_W = 64
HEAD_DIM = 64
WIN_R = 8
WIN_C = 16
N_MOD = 6
ROPE_THETA = 10000.0
EPS = 1e-6

LANE = 128
TM = 256
NA_ROWS_PER_TILE = TM // GRID_W
NA_KEY_ROWS = 12
NA_WIN = NA_KEY_ROWS * GRID_W
ROW_BLOCK = 128
DIFF_HEADS_PER_STEP = 2
GQA_Q_TILES = 2
DIFF_Q_TILES = 4
SOFTMAX_ROWS = 32
LOG2_E = 1.4426950408889634
MXU_DEPTH = 256
NEG = -1e30
VMEM_LIMIT = 56 * 1024 * 1024

N_HEADS_NA = 8
N_HEADS_GQ = 8
N_HEADS_GKV = 2
N_HEADS_DIFF = 8
NA_W = N_HEADS_NA * HEAD_DIM
GQ_W = N_HEADS_GQ * HEAD_DIM
GKV_W = N_HEADS_GKV * HEAD_DIM
NAQ0, NAK0, NAV0, GQ0, GK0, GV0, PAR_W = (int(v) for v in np.cumsum([0, NA_W, NA_W, NA_W, GQ_W, GKV_W, GKV_W]))
DIFF_W = N_HEADS_DIFF * 2 * HEAD_DIM
ADA_TILE = 1024

_DEINT = np.concatenate([np.arange(0, HEAD_DIM, 2), np.arange(1, HEAD_DIM, 2)])
_GQA_HEAD_ORDER = (0, 4, 1, 5, 2, 6, 3, 7)


def _dot(a, b):
    return jnp.dot(a, b, preferred_element_type=F32)


def _dot_nt(a, b):
    return lax.dot_general(a, b, (((1,), (1,)), ((), ())), preferred_element_type=F32)


def _params(n_grid):
    return pltpu.CompilerParams(dimension_semantics=("arbitrary",) * n_grid, vmem_limit_bytes=VMEM_LIMIT)


def _const_spec(shape):
    return pl.BlockSpec(shape, lambda *_: (0,) * len(shape), pipeline_mode=pl.Buffered(1))


def _split_bf16(a):
    hi = a.astype(BF16)
    return hi, (a - hi.astype(F32)).astype(BF16)


def _ada_kernel(cond_ref, w_ref, b_ref, o_ref):
    c = cond_ref[...]
    a_hi, a_lo = _split_bf16(c * (1.0 / (1.0 + jnp.exp(-c))))
    w_hi, w_lo = _split_bf16(w_ref[0])
    o_ref[0] = _dot(a_hi, w_hi) + _dot(a_lo, w_hi) + _dot(a_hi, w_lo) + b_ref[0]


def _ada(cond, ada_w, ada_b):
    depth, d, n = ada_w.shape
    tn = ADA_TILE
    return pl.pallas_call(
        _ada_kernel,
        grid=(depth, n // tn),
        in_specs=[pl.BlockSpec(cond.shape, lambda l, j: (0, 0)),
                  pl.BlockSpec((1, d, tn), lambda l, j: (l, 0, j)),
                  pl.BlockSpec((1, 1, tn), lambda l, j: (l, 0, j))],
        out_specs=pl.BlockSpec((1, cond.shape[0], tn), lambda l, j: (l, 0, j)),
        out_shape=jax.ShapeDtypeStruct((depth, cond.shape[0], n), F32),
        compiler_params=_params(2),
        name="ada_modulation",
    )(cond, ada_w, ada_b.reshape(depth, 1, n))


def _modulated_norm(x, shift, scale):
    ms = jnp.mean(x * x, axis=-1, keepdims=True)
    return (x * lax.rsqrt(ms + EPS)) * (1.0 + scale) + shift


def _group_mean_sq(x, bd):
    hi, lo = _split_bf16(x * x)
    return _dot(hi, bd) + _dot(lo, bd)


def _rope(x, c, sa, sb):
    return x * c + pltpu.roll(x, 32, 1) * sa + pltpu.roll(x, 96, 1) * sb


def _ones_column(rows):
    return (lax.broadcasted_iota(jnp.int32, (rows, LANE), 1) == 0).astype(BF16)


def _store_values(v_ref, p, col0, n_blocks):
    ones = _ones_column(p.shape[0])
    for j in range(n_blocks):
        v_ref[0, :, 2 * LANE * j:2 * LANE * j + LANE] = p[:, col0 + LANE * j:col0 + LANE * (j + 1)].astype(BF16)
        v_ref[0, :, 2 * LANE * j + LANE:2 * LANE * (j + 1)] = ones


def _token_tile(ctx_ref, x_ref):
    return jnp.where(pl.program_id(1) == 0, ctx_ref[0], x_ref[0])


def _proj_par_kernel(ctx_ref, x_ref, mod_ref, w_ref, c_ref, sa_ref, sb_ref, gain_ref, bd_ref,
                     naq_ref, nak_ref, nav_ref, gq_ref, gk_ref, gv_ref):
    mod = mod_ref[0, 0]
    h = _modulated_norm(_token_tile(ctx_ref, x_ref), mod[0:1], mod[1:2]).astype(BF16)
    p = _dot(h, w_ref[...])
    naq_ref[0] = p[:, NAQ0:NAK0].astype(BF16)
    nak_ref[0] = p[:, NAK0:NAV0].astype(BF16)
    _store_values(nav_ref, p, NAV0, NA_W // LANE)
    c, sa, sb, bd = c_ref[...], sa_ref[...], sb_ref[...], bd_ref[...]

    def qk_norm_rope(g, gain):
        g = g * lax.rsqrt(_group_mean_sq(g, bd) + EPS) * gain
        return _rope(g, c, sa, sb).astype(BF16)

    for j in range(GQ_W // LANE):
        gq_ref[0, :, LANE * j:LANE * (j + 1)] = qk_norm_rope(p[:, GQ0 + LANE * j:GQ0 + LANE * (j + 1)], gain_ref[0:1, :])
    gk_ref[0] = qk_norm_rope(p[:, GK0:GV0], gain_ref[1:2, :])
    _store_values(gv_ref, p, GV0, GKV_W // LANE)


def _proj_diff_kernel(x_ref, mod_ref, w_ref, c_ref, sa_ref, sb_ref, dq_ref, dk_ref, dv_ref):
    mod = mod_ref[0, 0]
    h = _modulated_norm(x_ref[0], mod[0:1], mod[1:2]).astype(BF16)
    p = _dot(h, w_ref[...])
    c, sa, sb = c_ref[...], sa_ref[...], sb_ref[...]
    for j in range(DIFF_W // LANE):
        dq_ref[0, :, LANE * j:LANE * (j + 1)] = _rope(p[:, LANE * j:LANE * (j + 1)], c, sa, sb).astype(BF16)
        dk_ref[0, :, LANE * j:LANE * (j + 1)] = _rope(p[:, DIFF_W + LANE * j:DIFF_W + LANE * (j + 1)], c, sa, sb).astype(BF16)
    _store_values(dv_ref, p, 2 * DIFF_W, DIFF_W // LANE)


def _token_spec(width, tile_off=0):
    return pl.BlockSpec((1, TM, width), lambda b, t: (b, t + tile_off, 0))


def _split_token_specs(d):
    return [pl.BlockSpec((1, TM, d), lambda b, t: (b, 0, 0)),
            pl.BlockSpec((1, TM, d), lambda b, t: (b, jnp.maximum(t - 1, 0), 0))]


def _mod_spec(d, latent_only=False):
    if latent_only:
        return pl.BlockSpec((1, 1, N_MOD, d), lambda b, t: (b, 1, 0, 0))
    return pl.BlockSpec((1, 1, N_MOD, d), lambda b, t: (b, jnp.minimum(t, 1), 0, 0))


def _rope_spec():
    return pl.BlockSpec((TM, LANE), lambda b, t: (t, 0))


def _proj_par(ctx, x, mods, w, rope, gains, bd):
    b, seq, d = x.shape
    t_len = ctx.shape[1] + seq
    widths = (NA_W, NA_W, 2 * NA_W, GQ_W, GKV_W, 2 * GKV_W)
    return pl.pallas_call(
        _proj_par_kernel,
        grid=(b, t_len // TM),
        in_specs=_split_token_specs(d) + [_mod_spec(d), _const_spec(w.shape), _rope_spec(), _rope_spec(), _rope_spec(),
                                          _const_spec(gains.shape), _const_spec(bd.shape)],
        out_specs=[_token_spec(n) for n in widths],
        out_shape=[jax.ShapeDtypeStruct((b, t_len, n), BF16) for n in widths],
        compiler_params=_params(2),
        name="proj_parallel_mixer",
    )(ctx, x, mods, w, *rope, gains, bd)


def _proj_diff(xa, mods, w, rope):
    b, t_len, d = xa.shape
    widths = (DIFF_W, DIFF_W, 2 * DIFF_W)
    return pl.pallas_call(
        _proj_diff_kernel,
        grid=(b, t_len // TM),
        in_specs=[_token_spec(d), _mod_spec(d), _const_spec(w.shape), _rope_spec(), _rope_spec(), _rope_spec()],
        out_specs=[_token_spec(n) for n in widths],
        out_shape=[jax.ShapeDtypeStruct((b, t_len, n), BF16) for n in widths],
        compiler_params=_params(2),
        name="proj_diff_mixer",
    )(xa, mods, w, *rope)


def _split_heads(q):
    lane = lax.broadcasted_iota(jnp.int32, q.shape, 1)
    zero = jnp.zeros_like(q)
    return jnp.concatenate([jnp.where(lane < HEAD_DIM, q, zero), jnp.where(lane >= HEAD_DIM, q, zero)], axis=0)


def _softmax_update(s, m_prev):
    m_new = jnp.maximum(m_prev, jnp.max(s, axis=-1, keepdims=True))
    p = jnp.exp2(s - m_new)
    return p, jnp.exp2(m_prev - m_new), m_new


def _flash_chunk(q2_ref, k, v, m_ref, acc_ref, row0):
    for r in range(2 * TM // ROW_BLOCK):
        rows = slice(row0 + ROW_BLOCK * r, row0 + ROW_BLOCK * (r + 1))
        s = _dot_nt(q2_ref[rows], k)
        p, alpha, m_new = _softmax_update(s, m_ref[rows])
        acc_ref[rows] = alpha * acc_ref[rows] + _dot(p.astype(BF16), v)
        m_ref[rows] = m_new


def _flash_pipeline(q2_ref, k_of, v_of, n_chunks, n_blocks, m_ref, acc_ref, s_refs, p_refs, alpha_refs):
    assert n_blocks % 2 == 0
    r2 = 2 * TM

    def score(i, j, slot):
        s_refs[slot][...] = _dot_nt(q2_ref[r2 * j:r2 * (j + 1)], k_of(i, j))

    def softmax(j, slot):
        for r in range(r2 // SOFTMAX_ROWS):
            rows = slice(SOFTMAX_ROWS * r, SOFTMAX_ROWS * (r + 1))
            state_rows = slice(r2 * j + SOFTMAX_ROWS * r, r2 * j + SOFTMAX_ROWS * (r + 1))
            p, alpha, m_new = _softmax_update(s_refs[slot][rows], m_ref[state_rows])
            p_refs[slot][rows] = p.astype(BF16)
            alpha_refs[slot][rows] = alpha
            m_ref[state_rows] = m_new

    def accumulate(i, j, slot):
        blk = slice(r2 * j, r2 * (j + 1))
        acc_ref[blk] = alpha_refs[slot][...] * acc_ref[blk] + _dot(p_refs[slot][...], v_of(i, j))

    last_slot = (n_blocks - 1) % 2
    p_refs[last_slot][...] = jnp.zeros(p_refs[last_slot].shape, BF16)
    alpha_refs[last_slot][...] = jnp.ones(alpha_refs[last_slot].shape, F32)
    score(0, 0, 0)

    def body(i, carry):
        for j in range(n_blocks):
            if j + 1 < n_blocks:
                score(i, j + 1, (j + 1) % 2)
            else:
                score(jnp.minimum(i + 1, n_chunks - 1), 0, 0)
            softmax(j, j % 2)
            if j > 0:
                accumulate(i, j - 1, (j - 1) % 2)
            else:
                accumulate(jnp.maximum(i - 1, 0), n_blocks - 1, last_slot)
        return carry

    lax.fori_loop(0, n_chunks, body, 0)
    accumulate(n_chunks - 1, n_blocks - 1, last_slot)


def _pipeline_scratch(tk):
    rows = 2 * TM
    return [pltpu.VMEM((rows, tk), F32)] * 2 + [pltpu.VMEM((rows, tk), BF16)] * 2 + [pltpu.VMEM((rows, 1), F32)] * 2


def _normalize(acc):
    return acc[:, 0:LANE] / acc[:, LANE:LANE + 1]


def _merge_heads(o):
    lane = lax.broadcasted_iota(jnp.int32, (TM, LANE), 1)
    return jnp.where(lane < HEAD_DIM, o[0:TM], o[TM:2 * TM])


def _init_flash(q_refs, q2_ref, m_ref, acc_ref):
    n = 0
    for q_ref in q_refs:
        for j in range(q_ref.shape[2] // LANE):
            q2_ref[2 * TM * n:2 * TM * (n + 1)] = _split_heads(q_ref[0, :, LANE * j:LANE * (j + 1)])
            n += 1
    m_ref[...] = jnp.full(m_ref.shape, NEG, F32)
    acc_ref[...] = jnp.zeros(acc_ref.shape, F32)


def _flash_scratch(n_blocks):
    rows = 2 * TM * n_blocks
    return [pltpu.VMEM((rows, LANE), BF16), pltpu.VMEM((rows, 1), F32), pltpu.VMEM((rows, 2 * LANE), F32)]


def _key_chunk(t_len):
    return next(tk for tk in (768, 640, 512, 384, 256) if t_len % tk == 0)


def _store_heads(o_ref, acc_ref, n_tiles, finish):
    n_lane_blocks = o_ref.shape[2] // LANE
    for u in range(n_tiles):
        for j in range(n_lane_blocks):
            n = u * n_lane_blocks + j
            o = _normalize(acc_ref[2 * TM * n:2 * TM * (n + 1)])
            o_ref[0, TM * u:TM * (u + 1), LANE * j:LANE * (j + 1)] = finish(o).astype(BF16)


def _gqa_ctx_kernel(q_ref, k_ref, v_ref, o_ref, q2_ref, m_ref, acc_ref):
    _init_flash([q_ref], q2_ref, m_ref, acc_ref)
    for j in range(q_ref.shape[2] // LANE):
        _flash_chunk(q2_ref, k_ref[0], v_ref[0], m_ref, acc_ref, 2 * TM * j)
    _store_heads(o_ref, acc_ref, 1, _merge_heads)


def _gqa_kernel(*refs, tk):
    q_refs = refs[:GQA_Q_TILES]
    k_ref, v_ref, o_ref, q2_ref, m_ref, acc_ref = refs[GQA_Q_TILES:GQA_Q_TILES + 6]
    pipe_refs = refs[GQA_Q_TILES + 6:]
    _init_flash(q_refs, q2_ref, m_ref, acc_ref)

    def chunk_of(ref):
        return lambda i, n: ref[0, pl.ds(pl.multiple_of(i * tk, tk), tk), :]
    _flash_pipeline(q2_ref, chunk_of(k_ref), chunk_of(v_ref), k_ref.shape[1] // tk,
                    GQA_Q_TILES * (o_ref.shape[2] // LANE), m_ref, acc_ref,
                    pipe_refs[0:2], pipe_refs[2:4], pipe_refs[4:6])
    _store_heads(o_ref, acc_ref, GQA_Q_TILES, _merge_heads)


def _gqa(gq, gk, gv, n_ctx, tk):
    b, t_len, qw = gq.shape
    assert n_ctx == TM and (t_len - n_ctx) % (GQA_Q_TILES * TM) == 0
    n_lane_blocks = qw // LANE
    ctx_out = pl.pallas_call(
        _gqa_ctx_kernel,
        grid=(b,),
        in_specs=[pl.BlockSpec((1, TM, qw), lambda b: (b, 0, 0)),
                  pl.BlockSpec((1, TM, LANE), lambda b: (b, 0, 0)),
                  pl.BlockSpec((1, TM, 2 * LANE), lambda b: (b, 0, 0))],
        out_specs=pl.BlockSpec((1, TM, qw), lambda b: (b, 0, 0)),
        out_shape=jax.ShapeDtypeStruct((b, TM, qw), BF16),
        scratch_shapes=_flash_scratch(n_lane_blocks),
        compiler_params=_params(1),
        name="gqa_attention_ctx",
    )(gq, gk, gv)
    latent_out = pl.pallas_call(
        functools.partial(_gqa_kernel, tk=tk),
        grid=(b, (t_len - n_ctx) // (GQA_Q_TILES * TM)),
        in_specs=([pl.BlockSpec((1, TM, qw), lambda b, s, u=u: (b, 1 + GQA_Q_TILES * s + u, 0))
                   for u in range(GQA_Q_TILES)]
                  + [pl.BlockSpec((1, t_len, LANE), lambda b, s: (b, 0, 0)),
                     pl.BlockSpec((1, t_len, 2 * LANE), lambda b, s: (b, 0, 0))]),
        out_specs=pl.BlockSpec((1, GQA_Q_TILES * TM, qw), lambda b, s: (b, s, 0)),
        out_shape=jax.ShapeDtypeStruct((b, t_len - n_ctx, qw), BF16),
        scratch_shapes=_flash_scratch(GQA_Q_TILES * n_lane_blocks) + _pipeline_scratch(tk),
        compiler_params=_params(2),
        name="gqa_attention",
    )(*([gq] * GQA_Q_TILES), gk, gv)
    return jnp.concatenate([ctx_out, latent_out], axis=1)


def _na_kernel(q_ref, k_ref, v_ref, bias_ref, o_ref, q2_ref, m_ref, acc_ref, *pipe_refs, n_ctx, rows):
    t = pl.program_id(1)
    n_blocks = q_ref.shape[2] // LANE
    r2 = 2 * TM
    s_refs, p_refs = pipe_refs[0:2], pipe_refs[2:4]
    _init_flash([q_ref], q2_ref, m_ref, acc_ref)

    @pl.when(t == 0)
    def _():
        for j in range(n_blocks):
            _flash_chunk(q2_ref, k_ref[0, 0:n_ctx, LANE * j:LANE * (j + 1)],
                         v_ref[0, 0:n_ctx, 2 * LANE * j:2 * LANE * (j + 1)], m_ref, acc_ref, r2 * j)
            o_ref[0, :, LANE * j:LANE * (j + 1)] = _merge_heads(_normalize(acc_ref[r2 * j:r2 * (j + 1)])).astype(BF16)

    @pl.when(t > 0)
    def _():
        first_row = NA_ROWS_PER_TILE * (t - 1)
        start = jnp.clip(first_row - WIN_R // 2, 0, rows - NA_KEY_ROWS)
        off = pl.multiple_of(n_ctx + GRID_W * start, GRID_W)

        def score(j, slot):
            q2 = q2_ref[r2 * j:r2 * (j + 1)]
            s_refs[slot][:, 0:n_ctx] = _dot_nt(q2, k_ref[0, 0:n_ctx, LANE * j:LANE * (j + 1)])
            s_refs[slot][:, n_ctx:] = _dot_nt(q2, k_ref[0, pl.ds(off, NA_WIN), LANE * j:LANE * (j + 1)])

        def softmax(j, slot):
            for r in range(r2 // SOFTMAX_ROWS):
                rows_r = slice(SOFTMAX_ROWS * r, SOFTMAX_ROWS * (r + 1))
                head, row = divmod(SOFTMAX_ROWS * r, TM)
                s_ctx = s_refs[slot][rows_r, 0:n_ctx]
                s_win = s_refs[slot][rows_r, n_ctx:] + bias_ref[2 * j + head, 0, row:row + SOFTMAX_ROWS, :]
                m = jnp.maximum(jnp.max(s_ctx, axis=-1, keepdims=True), jnp.max(s_win, axis=-1, keepdims=True))
                p_refs[slot][rows_r, 0:n_ctx] = jnp.exp2(s_ctx - m).astype(BF16)
                p_refs[slot][rows_r, n_ctx:] = jnp.exp2(s_win - m).astype(BF16)

        def output(j, slot):
            acc = (_dot(p_refs[slot][:, 0:n_ctx], v_ref[0, 0:n_ctx, 2 * LANE * j:2 * LANE * (j + 1)])
                   + _dot(p_refs[slot][:, n_ctx:], v_ref[0, pl.ds(off, NA_WIN), 2 * LANE * j:2 * LANE * (j + 1)]))
            o_ref[0, :, LANE * j:LANE * (j + 1)] = _merge_heads(_normalize(acc)).astype(BF16)

        score(0, 0)
        for j in range(n_blocks):
            if j + 1 < n_blocks:
                score(j + 1, (j + 1) % 2)
            softmax(j, j % 2)
            if j > 0:
                output(j - 1, (j - 1) % 2)
        output(n_blocks - 1, (n_blocks - 1) % 2)


def _na(naq, nak, nav, bias, n_ctx):
    b, t_len, qw = naq.shape
    n_tiles = t_len // TM
    rows = (t_len - n_ctx) // GRID_W
    n_keys = n_ctx + NA_WIN

    def bias_index(b, t):
        return (0, jnp.where(t <= 1, 0, jnp.where(t == n_tiles - 1, 2, 1)), 0, 0)

    def resident(width):
        return pl.BlockSpec((1, t_len, width), lambda b, t: (b, 0, 0), pipeline_mode=pl.Buffered(1))

    return pl.pallas_call(
        functools.partial(_na_kernel, n_ctx=n_ctx, rows=rows),
        grid=(b, n_tiles),
        in_specs=[pl.BlockSpec((1, TM, qw), lambda b, t: (b, t, 0)), resident(qw), resident(2 * qw),
                  pl.BlockSpec((bias.shape[0], 1, TM, NA_WIN), bias_index, pipeline_mode=pl.Buffered(1))],
        out_specs=pl.BlockSpec((1, TM, qw), lambda b, t: (b, t, 0)),
        out_shape=jax.ShapeDtypeStruct((b, t_len, qw), BF16),
        scratch_shapes=(_flash_scratch(qw // LANE)
                        + [pltpu.VMEM((2 * TM, n_keys), F32)] * 2 + [pltpu.VMEM((2 * TM, n_keys), BF16)] * 2),
        compiler_params=_params(2),
        name="neighbourhood_attention",
    )(naq, nak, nav, bias)


def _na_bias_table(rpb, rows):
    g_of_pattern = np.array([0, 2, rows // NA_ROWS_PER_TILE - 1])
    a = np.arange(NA_ROWS_PER_TILE)
    r = NA_ROWS_PER_TILE * g_of_pattern[:, None] + a[None, :]
    start = np.clip(NA_ROWS_PER_TILE * g_of_pattern - WIN_R // 2, 0, rows - NA_KEY_ROWS)
    rs = np.clip(r - WIN_R // 2, 0, rows - WIN_R)
    key_row = start[:, None] + np.arange(NA_KEY_ROWS)[None, :]
    row_ok = (key_row[:, None, :] >= rs[:, :, None]) & (key_row[:, None, :] < rs[:, :, None] + WIN_R)
    row_off = np.clip(key_row[:, None, :] - r[:, :, None] + (WIN_R - 1), 0, 2 * WIN_R - 2)
    cols = np.arange(GRID_W)
    col_start = np.clip(cols - WIN_C // 2, 0, GRID_W - WIN_C)
    col_ok = (cols[None, :] >= col_start[:, None]) & (cols[None, :] < col_start[:, None] + WIN_C)
    col_off = np.clip(cols[None, :] - cols[:, None] + (WIN_C - 1), 0, 2 * WIN_C - 2)
    ok = row_ok[:, :, None, :, None] & col_ok[None, None, :, None, :]
    bias_rows = rpb.astype(F32)[:, row_off]
    pick_col = (col_off[:, :, None] == np.arange(2 * WIN_C - 1)).astype(np.float32)
    vals = jnp.einsum('hpakb,cjb->hpackj', bias_rows, pick_col, precision=lax.Precision.HIGHEST)
    table = jnp.where(ok[None], vals * LOG2_E, NEG)
    return table.reshape(rpb.shape[0], 3, TM, NA_WIN)


def _diff_kernel(lam_ref, gain_ref, *refs, tk, lambda_init):
    q_refs = refs[:DIFF_Q_TILES]
    k_ref, v_ref, o_ref, q2_ref, m_ref, acc_ref = refs[DIFF_Q_TILES:DIFF_Q_TILES + 6]
    pipe_refs = refs[DIFF_Q_TILES + 6:]
    n_heads = o_ref.shape[2] // LANE
    _init_flash(q_refs, q2_ref, m_ref, acc_ref)

    def chunk_of(ref, width):
        def chunk(i, n):
            h = n % n_heads
            return ref[0, pl.ds(pl.multiple_of(i * tk, tk), tk), width * h:width * (h + 1)]
        return chunk
    _flash_pipeline(q2_ref, chunk_of(k_ref, LANE), chunk_of(v_ref, 2 * LANE), k_ref.shape[1] // tk,
                    DIFF_Q_TILES * n_heads, m_ref, acc_ref, pipe_refs[0:2], pipe_refs[2:4], pipe_refs[4:6])

    lp = lam_ref[...]
    lam = (jnp.exp(jnp.sum(lp[0:1] * lp[1:2], axis=-1, keepdims=True))
           - jnp.exp(jnp.sum(lp[2:3] * lp[3:4], axis=-1, keepdims=True)) + lambda_init)

    def sub_layer_norm(o):
        d = o[0:TM] - lam * o[TM:2 * TM]
        y = d * lax.rsqrt(jnp.mean(d * d, axis=-1, keepdims=True) + EPS) * gain_ref[...]
        return y * (1.0 - lambda_init)
    _store_heads(o_ref, acc_ref, DIFF_Q_TILES, sub_layer_norm)


def _diff(lam_params, subln_gain, dq, dk, dv, n_ctx, tk, lambda_init):
    b, t_len, qw = dq.shape
    assert n_ctx == TM and (t_len - n_ctx) % (DIFF_Q_TILES * TM) == 0
    hw = LANE * DIFF_HEADS_PER_STEP
    return pl.pallas_call(
        functools.partial(_diff_kernel, tk=tk, lambda_init=lambda_init),
        grid=(b, qw // hw, (t_len - n_ctx) // (DIFF_Q_TILES * TM)),
        in_specs=([pl.BlockSpec(lam_params.shape, lambda b, h, s: (0, 0)),
                   pl.BlockSpec(subln_gain.shape, lambda b, h, s: (0, 0))]
                  + [pl.BlockSpec((1, TM, hw), lambda b, h, s, u=u: (b, 1 + DIFF_Q_TILES * s + u, h))
                     for u in range(DIFF_Q_TILES)]
                  + [pl.BlockSpec((1, t_len, hw), lambda b, h, s: (b, 0, h)),
                     pl.BlockSpec((1, t_len, 2 * hw), lambda b, h, s: (b, 0, h))]),
        out_specs=pl.BlockSpec((1, DIFF_Q_TILES * TM, hw), lambda b, h, s: (b, s, h)),
        out_shape=jax.ShapeDtypeStruct((b, t_len - n_ctx, qw), BF16),
        scratch_shapes=_flash_scratch(DIFF_Q_TILES * DIFF_HEADS_PER_STEP) + _pipeline_scratch(tk),
        compiler_params=_params(3),
        name="diff_attention",
    )(lam_params, subln_gain, *([dq] * DIFF_Q_TILES), dk, dv)


def _ffn_kernel(*refs, n_att, ff_chunks, final, split_tokens):
    if split_tokens:
        x_tile = _token_tile(refs[0], refs[1])
        refs = refs[1:]
    else:
        x_tile = refs[0][0]
    mod_ref = refs[1]
    att_refs = refs[2:2 + n_att]
    wo_refs = refs[2 + n_att:2 + 2 * n_att]
    wg_ref, wu_ref, wd_ref = refs[2 + 2 * n_att:5 + 2 * n_att]
    o_ref = refs[-1]
    mod = mod_ref[0, 0]
    y = _dot(att_refs[0][0], wo_refs[0][...])
    for a_ref, w_ref in zip(att_refs[1:], wo_refs[1:]):
        y = y + _dot(a_ref[0], w_ref[...])
    x1 = x_tile + mod[2:3] * y
    h = _modulated_norm(x1, mod[3:4], mod[4:5]).astype(BF16)
    d_ff = wg_ref.shape[1]
    bounds = [0]
    for i in range(ff_chunks):
        bounds.append(min(d_ff, -(-(d_ff * (i + 1) // ff_chunks) // MXU_DEPTH) * MXU_DEPTH))
    down = None
    for lo, hi in zip(bounds[:-1], bounds[1:]):
        g = _dot(h, wg_ref[:, lo:hi])
        u = _dot(h, wu_ref[:, lo:hi])
        a = (g * (1.0 / (1.0 + jnp.exp(-g))) * u).astype(BF16)
        part = _dot(a, wd_ref[lo:hi, :])
        down = part if down is None else down + part
    x2 = x1 + mod[5:6] * down
    if final:
        gain_ref = refs[5 + 2 * n_att]
        x2 = x2 * lax.rsqrt(jnp.mean(x2 * x2, axis=-1, keepdims=True) + EPS) * gain_ref[...]
    o_ref[0] = x2


def _ffn(tokens, mods, atts, wos, wg, wu, wd, final_gain=None):
    split_tokens = isinstance(tokens, tuple)
    n_rows = atts[0].shape[1]
    b, _, d = tokens[-1].shape if split_tokens else tokens.shape
    final = final_gain is not None
    if split_tokens:
        token_specs = _split_token_specs(d)
        tokens = list(tokens)
    else:
        token_specs = [_token_spec(d, (tokens.shape[1] - n_rows) // TM)]
        tokens = [tokens]
    in_specs = (token_specs + [_mod_spec(d, latent_only=not split_tokens)]
                + [_token_spec(a.shape[2]) for a in atts]
                + [_const_spec(w.shape) for w in wos]
                + [_const_spec(wg.shape), _const_spec(wu.shape), _const_spec(wd.shape)])
    args = [*tokens, mods, *atts, *wos, wg, wu, wd]
    if final:
        in_specs.append(_const_spec(final_gain.shape))
        args.append(final_gain)
    return pl.pallas_call(
        functools.partial(_ffn_kernel, n_att=len(atts), ff_chunks=2, final=final, split_tokens=split_tokens),
        grid=(b, n_rows // TM),
        in_specs=in_specs,
        out_specs=_token_spec(d),
        out_shape=jax.ShapeDtypeStruct((b, n_rows, d), F32),
        compiler_params=_params(2),
        name="outproj_ffn_final" if final else "outproj_ffn",
    )(*args)


def _head_cols(base, heads):
    return np.concatenate([base + HEAD_DIM * h + _DEINT for h in heads])


def _rope_tables(seq, n_ctx):
    t = jnp.arange(seq)
    row = (t // GRID_W).astype(F32)
    col = (t % GRID_W).astype(F32)
    n_freq = HEAD_DIM // 4
    inv = ROPE_THETA ** (-jnp.arange(n_freq, dtype=F32) / n_freq)
    ang = jnp.concatenate([row[:, None] * inv, col[:, None] * inv], axis=-1)
    cos, sin = jnp.cos(ang), jnp.sin(ang)
    zero = jnp.zeros_like(sin)

    def table(first_half, second_half, ctx_value):
        lat = jnp.tile(jnp.concatenate([first_half, second_half], axis=-1), (1, LANE // HEAD_DIM))
        return jnp.concatenate([jnp.full((n_ctx, LANE), ctx_value, F32), lat], axis=0)

    return table(cos, cos, 1.0), table(zero, sin, 0.0), table(-sin, zero, 0.0)


def kernel(x, c, ctx, c_ctx, ada_w, ada_b, ffn_w_gate, ffn_w_up, ffn_w_down, par_w_in, par_w_out, na_rpb,
           gqa_q_gain, gqa_k_gain, diff_w_in, diff_w_out, diff_lambda_q1, diff_lambda_k1, diff_lambda_q2,
           diff_lambda_k2, diff_subln_gain, final_norm_gain):
    b, seq, d = x.shape
    n_ctx = ctx.shape[1]
    assert n_ctx == TM and seq % TM == 0 and d % LANE == 0 and b < 8 and ada_w.shape[0] == 2
    assert par_w_in.shape[-1] == PAR_W and diff_w_in.shape[-1] == 3 * DIFF_W
    rows = seq // GRID_W
    scale = HEAD_DIM ** -0.5 * LOG2_E

    cond = jnp.zeros((8, d), F32).at[:b].set(c).at[b].set(c_ctx)
    mods_all = _ada(cond, ada_w, ada_b).reshape(2, 8, N_MOD, d)

    def mods_of(layer):
        m = mods_all[layer]
        return jnp.stack([jnp.broadcast_to(m[b], (b, N_MOD, d)), m[:b]], axis=1)

    rope = _rope_tables(seq, n_ctx)

    cols0 = np.concatenate([np.arange(NAQ0, GQ0), _head_cols(GQ0, _GQA_HEAD_ORDER), _head_cols(GK0, range(N_HEADS_GKV)),
                            np.arange(GV0, PAR_W)])
    col_scale0 = np.ones((PAR_W,), np.float32)
    col_scale0[NAQ0:NAK0] = scale
    w_in0 = (par_w_in[0][:, cols0] * col_scale0).astype(BF16)
    gains = jnp.zeros((8, LANE), F32)
    gains = gains.at[0].set(jnp.tile(gqa_q_gain[0][_DEINT] * scale, 2)).at[1].set(jnp.tile(gqa_k_gain[0][_DEINT], 2))
    block_mean = jnp.asarray(np.kron(np.eye(LANE // HEAD_DIM), np.full((HEAD_DIM, HEAD_DIM), 1.0 / HEAD_DIM)), BF16)
    naq, nak, nav, gq, gk, gv = _proj_par(ctx, x, mods_of(0), w_in0, rope, gains, block_mean)
    att_na = _na(naq, nak, nav, _na_bias_table(na_rpb[0], rows), n_ctx)
    att_g = _gqa(gq, gk, gv, n_ctx, tk=_key_chunk(n_ctx + seq))
    wo_na = par_w_out[0][0:NA_W].astype(BF16)
    wo_g = par_w_out[0][NA_W + np.concatenate([HEAD_DIM * h + np.arange(HEAD_DIM) for h in _GQA_HEAD_ORDER])].astype(BF16)
    xa = _ffn((ctx, x), mods_of(0), [att_na, att_g], [wo_na, wo_g],
              ffn_w_gate[0].astype(BF16), ffn_w_up[0].astype(BF16), ffn_w_down[0].astype(BF16))

    sub_heads = range(2 * N_HEADS_DIFF)
    cols1 = np.concatenate([_head_cols(0, sub_heads), _head_cols(DIFF_W, sub_heads), np.arange(2 * DIFF_W, 3 * DIFF_W)])
    col_scale1 = np.ones((3 * DIFF_W,), np.float32)
    col_scale1[0:DIFF_W] = scale
    w_in1 = (diff_w_in[0][:, cols1] * col_scale1).astype(BF16)
    dq, dk, dv = _proj_diff(xa, mods_of(1), w_in1, rope)
    lambda_init = 0.8 - 0.6 * float(np.exp(-0.3 * 1))
    lam_params = jnp.stack([diff_lambda_q1[0], diff_lambda_k1[0], diff_lambda_q2[0], diff_lambda_k2[0]]).astype(F32)
    att_d = _diff(lam_params, diff_subln_gain[0].reshape(1, -1).astype(F32), dq, dk, dv, n_ctx,
                  tk=_key_chunk(n_ctx + seq),
                  lambda_init=lambda_init)
    return _ffn(xa, mods_of(1), [att_d], [diff_w_out[0].astype(BF16)],
                ffn_w_gate[1].astype(BF16), ffn_w_up[1].astype(BF16), ffn_w_down[1].astype(BF16),
                final_gain=final_norm_gain.reshape(1, -1).astype(F32))
```

```python
import functools

import numpy as np
import jax
import jax.numpy as jnp
from jax import lax
from jax.experimental import pallas as pl
from jax.experimental.pallas import tpu as pltpu

F32 = jnp.float32
BF16 = jnp.bfloat16

GRID_W = 64
HEAD_DIM = 64
WIN_R = 8
WIN_C = 16
N_MOD = 6
ROPE_THETA = 10000.0
EPS = 1e-6

LANE = 128
TM = 256
NA_ROWS_PER_TILE = TM // GRID_W
NA_KEY_ROWS = 12
NA_WIN = NA_KEY_ROWS * GRID_W
ROW_BLOCK = 128
DIFF_HEADS_PER_STEP = 2
GQA_Q_TILES = 2
DIFF_Q_TILES = 4
SOFTMAX_ROWS = 32
LOG2_E = 1.4426950408889634
MXU_DEPTH = 256
NEG = -1e30
VMEM_LIMIT = 56 * 1024 * 1024

N_HEADS_NA = 8
N_HEADS_GQ = 8
N_HEADS_GKV = 2
N_HEADS_DIFF = 8
NA_W = N_HEADS_NA * HEAD_DIM
GQ_W = N_HEADS_GQ * HEAD_DIM
GKV_W = N_HEADS_GKV * HEAD_DIM
NAQ0, NAK0, NAV0, GQ0, GK0, GV0, PAR_W = (int(v) for v in np.cumsum([0, NA_W, NA_W, NA_W, GQ_W, GKV_W, GKV_W]))
DIFF_W = N_HEADS_DIFF * 2 * HEAD_DIM
ADA_TILE = 1024

_DEINT = np.concatenate([np.arange(0, HEAD_DIM, 2), np.arange(1, HEAD_DIM, 2)])
_GQA_HEAD_ORDER = (0, 4, 1, 5, 2, 6, 3, 7)


def _dot(a, b):
    return jnp.dot(a, b, preferred_element_type=F32)


def _dot_nt(a, b):
    return lax.dot_general(a, b, (((1,), (1,)), ((), ())), preferred_element_type=F32)


def _params(n_grid):
    return pltpu.CompilerParams(dimension_semantics=("arbitrary",) * n_grid, vmem_limit_bytes=VMEM_LIMIT)


def _const_spec(shape):
    return pl.BlockSpec(shape, lambda *_: (0,) * len(shape), pipeline_mode=pl.Buffered(1))


def _split_bf16(a):
    hi = a.astype(BF16)
    return hi, (a - hi.astype(F32)).astype(BF16)


def _ada_kernel(cond_ref, w_ref, b_ref, o_ref):
    c = cond_ref[...]
    a_hi, a_lo = _split_bf16(c * (1.0 / (1.0 + jnp.exp(-c))))
    w_hi, w_lo = _split_bf16(w_ref[0])
    o_ref[0] = _dot(a_hi, w_hi) + _dot(a_lo, w_hi) + _dot(a_hi, w_lo) + b_ref[0]


def _ada(cond, ada_w, ada_b):
    depth, d, n = ada_w.shape
    tn = ADA_TILE
    return pl.pallas_call(
        _ada_kernel,
        grid=(depth, n // tn),
        in_specs=[pl.BlockSpec(cond.shape, lambda l, j: (0, 0)),
                  pl.BlockSpec((1, d, tn), lambda l, j: (l, 0, j)),
                  pl.BlockSpec((1, 1, tn), lambda l, j: (l, 0, j))],
        out_specs=pl.BlockSpec((1, cond.shape[0], tn), lambda l, j: (l, 0, j)),
        out_shape=jax.ShapeDtypeStruct((depth, cond.shape[0], n), F32),
        compiler_params=_params(2),
        name="ada_modulation",
    )(cond, ada_w, ada_b.reshape(depth, 1, n))


def _modulated_norm(x, shift, scale):
    ms = jnp.mean(x * x, axis=-1, keepdims=True)
    return (x * lax.rsqrt(ms + EPS)) * (1.0 + scale) + shift


def _group_mean_sq(x, bd):
    hi, lo = _split_bf16(x * x)
    return _dot(hi, bd) + _dot(lo, bd)


def _rope(x, c, sa, sb):
    return x * c + pltpu.roll(x, 32, 1) * sa + pltpu.roll(x, 96, 1) * sb


def _ones_column(rows):
    return (lax.broadcasted_iota(jnp.int32, (rows, LANE), 1) == 0).astype(BF16)


def _store_values(v_ref, p, col0, n_blocks):
    ones = _ones_column(p.shape[0])
    for j in range(n_blocks):
        v_ref[0, :, 2 * LANE * j:2 * LANE * j + LANE] = p[:, col0 + LANE * j:col0 + LANE * (j + 1)].astype(BF16)
        v_ref[0, :, 2 * LANE * j + LANE:2 * LANE * (j + 1)] = ones


def _token_tile(ctx_ref, x_ref):
    return jnp.where(pl.program_id(1) == 0, ctx_ref[0], x_ref[0])


def _proj_par_kernel(ctx_ref, x_ref, mod_ref, w_ref, c_ref, sa_ref, sb_ref, gain_ref, bd_ref,
                     naq_ref, nak_ref, nav_ref, gq_ref, gk_ref, gv_ref):
    mod = mod_ref[0, 0]
    h = _modulated_norm(_token_tile(ctx_ref, x_ref), mod[0:1], mod[1:2]).astype(BF16)
    p = _dot(h, w_ref[...])
    naq_ref[0] = p[:, NAQ0:NAK0].astype(BF16)
    nak_ref[0] = p[:, NAK0:NAV0].astype(BF16)
    _store_values(nav_ref, p, NAV0, NA_W // LANE)
    c, sa, sb, bd = c_ref[...], sa_ref[...], sb_ref[...], bd_ref[...]

    def qk_norm_rope(g, gain):
        g = g * lax.rsqrt(_group_mean_sq(g, bd) + EPS) * gain
        return _rope(g, c, sa, sb).astype(BF16)

    for j in range(GQ_W // LANE):
        gq_ref[0, :, LANE * j:LANE * (j + 1)] = qk_norm_rope(p[:, GQ0 + LANE * j:GQ0 + LANE * (j + 1)], gain_ref[0:1, :])
    gk_ref[0] = qk_norm_rope(p[:, GK0:GV0], gain_ref[1:2, :])
    _store_values(gv_ref, p, GV0, GKV_W // LANE)


def _proj_diff_kernel(x_ref, mod_ref, w_ref, c_ref, sa_ref, sb_ref, dq_ref, dk_ref, dv_ref):
    mod = mod_ref[0, 0]
    h = _modulated_norm(x_ref[0], mod[0:1], mod[1:2]).astype(BF16)
    p = _dot(h, w_ref[...])
    c, sa, sb = c_ref[...], sa_ref[...], sb_ref[...]
    for j in range(DIFF_W // LANE):
        dq_ref[0, :, LANE * j:LANE * (j + 1)] = _rope(p[:, LANE * j:LANE * (j + 1)], c, sa, sb).astype(BF16)
        dk_ref[0, :, LANE * j:LANE * (j + 1)] = _rope(p[:, DIFF_W + LANE * j:DIFF_W + LANE * (j + 1)], c, sa, sb).astype(BF16)
    _store_values(dv_ref, p, 2 * DIFF_W, DIFF_W // LANE)


def _token_spec(width, tile_off=0):
    return pl.BlockSpec((1, TM, width), lambda b, t: (b, t + tile_off, 0))


def _split_token_specs(d):
    return [pl.BlockSpec((1, TM, d), lambda b, t: (b, 0, 0)),
            pl.BlockSpec((1, TM, d), lambda b, t: (b, jnp.maximum(t - 1, 0), 0))]


def _mod_spec(d, latent_only=False):
    if latent_only:
        return pl.BlockSpec((1, 1, N_MOD, d), lambda b, t: (b, 1, 0, 0))
    return pl.BlockSpec((1, 1, N_MOD, d), lambda b, t: (b, jnp.minimum(t, 1), 0, 0))


def _rope_spec():
    return pl.BlockSpec((TM, LANE), lambda b, t: (t, 0))


def _proj_par(ctx, x, mods, w, rope, gains, bd):
    b, seq, d = x.shape
    t_len = ctx.shape[1] + seq
    widths = (NA_W, NA_W, 2 * NA_W, GQ_W, GKV_W, 2 * GKV_W)
    return pl.pallas_call(
        _proj_par_kernel,
        grid=(b, t_len // TM),
        in_specs=_split_token_specs(d) + [_mod_spec(d), _const_spec(w.shape), _rope_spec(), _rope_spec(), _rope_spec(),
                                          _const_spec(gains.shape), _const_spec(bd.shape)],
        out_specs=[_token_spec(n) for n in widths],
        out_shape=[jax.ShapeDtypeStruct((b, t_len, n), BF16) for n in widths],
        compiler_params=_params(2),
        name="proj_parallel_mixer",
    )(ctx, x, mods, w, *rope, gains, bd)


def _proj_diff(xa, mods, w, rope):
    b, t_len, d = xa.shape
    widths = (DIFF_W, DIFF_W, 2 * DIFF_W)
    return pl.pallas_call(
        _proj_diff_kernel,
        grid=(b, t_len // TM),
        in_specs=[_token_spec(d), _mod_spec(d), _const_spec(w.shape), _rope_spec(), _rope_spec(), _rope_spec()],
        out_specs=[_token_spec(n) for n in widths],
        out_shape=[jax.ShapeDtypeStruct((b, t_len, n), BF16) for n in widths],
        compiler_params=_params(2),
        name="proj_diff_mixer",
    )(xa, mods, w, *rope)


def _split_heads(q):
    lane = lax.broadcasted_iota(jnp.int32, q.shape, 1)
    zero = jnp.zeros_like(q)
    return jnp.concatenate([jnp.where(lane < HEAD_DIM, q, zero), jnp.where(lane >= HEAD_DIM, q, zero)], axis=0)


def _softmax_update(s, m_prev):
    m_new = jnp.maximum(m_prev, jnp.max(s, axis=-1, keepdims=True))
    p = jnp.exp2(s - m_new)
    return p, jnp.exp2(m_prev - m_new), m_new


def _flash_chunk(q2_ref, k, v, m_ref, acc_ref, row0):
    for r in range(2 * TM // ROW_BLOCK):
        rows = slice(row0 + ROW_BLOCK * r, row0 + ROW_BLOCK * (r + 1))
        s = _dot_nt(q2_ref[rows], k)
        p, alpha, m_new = _softmax_update(s, m_ref[rows])
        acc_ref[rows] = alpha * acc_ref[rows] + _dot(p.astype(BF16), v)
        m_ref[rows] = m_new


def _flash_pipeline(q2_ref, k_of, v_of, n_chunks, n_blocks, m_ref, acc_ref, s_refs, p_refs, alpha_refs):
    assert n_blocks % 2 == 0
    r2 = 2 * TM

    def score(i, j, slot):
        s_refs[slot][...] = _dot_nt(q2_ref[r2 * j:r2 * (j + 1)], k_of(i, j))

    def softmax(j, slot):
        for r in range(r2 // SOFTMAX_ROWS):
            rows = slice(SOFTMAX_ROWS * r, SOFTMAX_ROWS * (r + 1))
            state_rows = slice(r2 * j + SOFTMAX_ROWS * r, r2 * j + SOFTMAX_ROWS * (r + 1))
            p, alpha, m_new = _softmax_update(s_refs[slot][rows], m_ref[state_rows])
            p_refs[slot][rows] = p.astype(BF16)
            alpha_refs[slot][rows] = jnp.broadcast_to(alpha, (SOFTMAX_ROWS, LANE))
            m_ref[state_rows] = m_new

    def accumulate(i, j, slot):
        blk = slice(r2 * j, r2 * (j + 1))
        acc_ref[blk] = pltpu.repeat(alpha_refs[slot][...], 2, 1) * acc_ref[blk] + _dot(p_refs[slot][...], v_of(i, j))

    last_slot = (n_blocks - 1) % 2
    p_refs[last_slot][...] = jnp.zeros(p_refs[last_slot].shape, BF16)
    alpha_refs[last_slot][...] = jnp.ones(alpha_refs[last_slot].shape, F32)
    score(0, 0, 0)

    def body(i, carry):
        for j in range(n_blocks):
            if j + 1 < n_blocks:
                score(i, j + 1, (j + 1) % 2)
            else:
                score(jnp.minimum(i + 1, n_chunks - 1), 0, 0)
            softmax(j, j % 2)
            if j > 0:
                accumulate(i, j - 1, (j - 1) % 2)
            else:
                accumulate(jnp.maximum(i - 1, 0), n_blocks - 1, last_slot)
        return carry

    lax.fori_loop(0, n_chunks, body, 0)
    accumulate(n_chunks - 1, n_blocks - 1, last_slot)


def _pipeline_scratch(tk):
    rows = 2 * TM
    return [pltpu.VMEM((rows, tk), F32)] * 2 + [pltpu.VMEM((rows, tk), BF16)] * 2 + [pltpu.VMEM((rows, LANE), F32)] * 2


def _normalize(acc):
    return acc[:, 0:LANE] / acc[:, LANE:LANE + 1]


def _merge_heads(o):
    lane = lax.broadcasted_iota(jnp.int32, (TM, LANE), 1)
    return jnp.where(lane < HEAD_DIM, o[0:TM], o[TM:2 * TM])


def _init_flash(q_refs, q2_ref, m_ref, acc_ref):
    n = 0
    for q_ref in q_refs:
        for j in range(q_ref.shape[2] // LANE):
            q2_ref[2 * TM * n:2 * TM * (n + 1)] = _split_heads(q_ref[0, :, LANE * j:LANE * (j + 1)])
            n += 1
    m_ref[...] = jnp.full(m_ref.shape, NEG, F32)
    acc_ref[...] = jnp.zeros(acc_ref.shape, F32)


def _flash_scratch(n_blocks):
    rows = 2 * TM * n_blocks
    return [pltpu.VMEM((rows, LANE), BF16), pltpu.VMEM((rows, 1), F32), pltpu.VMEM((rows, 2 * LANE), F32)]


def _key_chunk(t_len):
    return next(tk for tk in (768, 640, 512, 384, 256) if t_len % tk == 0)


def _store_heads(o_ref, acc_ref, n_tiles, finish):
    n_lane_blocks = o_ref.shape[2] // LANE
    for u in range(n_tiles):
        for j in range(n_lane_blocks):
            n = u * n_lane_blocks + j
            o = _normalize(acc_ref[2 * TM * n:2 * TM * (n + 1)])
            o_ref[0, TM * u:TM * (u + 1), LANE * j:LANE * (j + 1)] = finish(o).astype(BF16)


def _gqa_ctx_kernel(q_ref, k_ref, v_ref, o_ref, q2_ref, m_ref, acc_ref):
    _init_flash([q_ref], q2_ref, m_ref, acc_ref)
    for j in range(q_ref.shape[2] // LANE):
        _flash_chunk(q2_ref, k_ref[0], v_ref[0], m_ref, acc_ref, 2 * TM * j)
    _store_heads(o_ref, acc_ref, 1, _merge_heads)


def _gqa_kernel(*refs, tk):
    q_refs = refs[:GQA_Q_TILES]
    k_ref, v_ref, o_ref, q2_ref, m_ref, acc_ref = refs[GQA_Q_TILES:GQA_Q_TILES + 6]
    pipe_refs = refs[GQA_Q_TILES + 6:]
    _init_flash(q_refs, q2_ref, m_ref, acc_ref)

    def chunk_of(ref):
        return lambda i, n: ref[0, pl.ds(pl.multiple_of(i * tk, tk), tk), :]
    _flash_pipeline(q2_ref, chunk_of(k_ref), chunk_of(v_ref), k_ref.shape[1] // tk,
                    GQA_Q_TILES * (o_ref.shape[2] // LANE), m_ref, acc_ref,
                    pipe_refs[0:2], pipe_refs[2:4], pipe_refs[4:6])
    _store_heads(o_ref, acc_ref, GQA_Q_TILES, _merge_heads)


def _gqa(gq, gk, gv, n_ctx, tk):
    b, t_len, qw = gq.shape
    assert n_ctx == TM and (t_len - n_ctx) % (GQA_Q_TILES * TM) == 0
    n_lane_blocks = qw // LANE
    ctx_out = pl.pallas_call(
        _gqa_ctx_kernel,
        grid=(b,),
        in_specs=[pl.BlockSpec((1, TM, qw), lambda b: (b, 0, 0)),
                  pl.BlockSpec((1, TM, LANE), lambda b: (b, 0, 0)),
                  pl.BlockSpec((1, TM, 2 * LANE), lambda b: (b, 0, 0))],
        out_specs=pl.BlockSpec((1, TM, qw), lambda b: (b, 0, 0)),
        out_shape=jax.ShapeDtypeStruct((b, TM, qw), BF16),
        scratch_shapes=_flash_scratch(n_lane_blocks),
        compiler_params=_params(1),
        name="gqa_attention_ctx",
    )(gq, gk, gv)
    latent_out = pl.pallas_call(
        functools.partial(_gqa_kernel, tk=tk),
        grid=(b, (t_len - n_ctx) // (GQA_Q_TILES * TM)),
        in_specs=([pl.BlockSpec((1, TM, qw), lambda b, s, u=u: (b, 1 + GQA_Q_TILES * s + u, 0))
                   for u in range(GQA_Q_TILES)]
                  + [pl.BlockSpec((1, t_len, LANE), lambda b, s: (b, 0, 0)),
                     pl.BlockSpec((1, t_len, 2 * LANE), lambda b, s: (b, 0, 0))]),
        out_specs=pl.BlockSpec((1, GQA_Q_TILES * TM, qw), lambda b, s: (b, s, 0)),
        out_shape=jax.ShapeDtypeStruct((b, t_len - n_ctx, qw), BF16),
        scratch_shapes=_flash_scratch(GQA_Q_TILES * n_lane_blocks) + _pipeline_scratch(tk),
        compiler_params=_params(2),
        name="gqa_attention",
    )(*([gq] * GQA_Q_TILES), gk, gv)
    return jnp.concatenate([ctx_out, latent_out], axis=1)


def _na_kernel(q_ref, k_ref, v_ref, bias_ref, o_ref, q2_ref, m_ref, acc_ref, *pipe_refs, n_ctx, rows):
    t = pl.program_id(1)
    n_blocks = q_ref.shape[2] // LANE
    r2 = 2 * TM
    s_refs, p_refs = pipe_refs[0:2], pipe_refs[2:4]
    _init_flash([q_ref], q2_ref, m_ref, acc_ref)

    @pl.when(t == 0)
    def _():
        for j in range(n_blocks):
            _flash_chunk(q2_ref, k_ref[0, 0:n_ctx, LANE * j:LANE * (j + 1)],
                         v_ref[0, 0:n_ctx, 2 * LANE * j:2 * LANE * (j + 1)], m_ref, acc_ref, r2 * j)
            o_ref[0, :, LANE * j:LANE * (j + 1)] = _merge_heads(_normalize(acc_ref[r2 * j:r2 * (j + 1)])).astype(BF16)

    @pl.when(t > 0)
    def _():
        first_row = NA_ROWS_PER_TILE * (t - 1)
        start = jnp.clip(first_row - WIN_R // 2, 0, rows - NA_KEY_ROWS)
        off = pl.multiple_of(n_ctx + GRID_W * start, GRID_W)

        def score(j, slot):
            q2 = q2_ref[r2 * j:r2 * (j + 1)]
            s_refs[slot][:, 0:n_ctx] = _dot_nt(q2, k_ref[0, 0:n_ctx, LANE * j:LANE * (j + 1)])
            s_refs[slot][:, n_ctx:] = _dot_nt(q2, k_ref[0, pl.ds(off, NA_WIN), LANE * j:LANE * (j + 1)])

        def softmax(j, slot):
            for r in range(r2 // SOFTMAX_ROWS):
                rows_r = slice(SOFTMAX_ROWS * r, SOFTMAX_ROWS * (r + 1))
                head, row = divmod(SOFTMAX_ROWS * r, TM)
                s_ctx = s_refs[slot][rows_r, 0:n_ctx]
                s_win = s_refs[slot][rows_r, n_ctx:] + bias_ref[2 * j + head, 0, row:row + SOFTMAX_ROWS, :]
                m = jnp.maximum(jnp.max(s_ctx, axis=-1, keepdims=True), jnp.max(s_win, axis=-1, keepdims=True))
                p_refs[slot][rows_r, 0:n_ctx] = jnp.exp2(s_ctx - m).astype(BF16)
                p_refs[slot][rows_r, n_ctx:] = jnp.exp2(s_win - m).astype(BF16)

        def output(j, slot):
            acc = (_dot(p_refs[slot][:, 0:n_ctx], v_ref[0, 0:n_ctx, 2 * LANE * j:2 * LANE * (j + 1)])
                   + _dot(p_refs[slot][:, n_ctx:], v_ref[0, pl.ds(off, NA_WIN), 2 * LANE * j:2 * LANE * (j + 1)]))
            o_ref[0, :, LANE * j:LANE * (j + 1)] = _merge_heads(_normalize(acc)).astype(BF16)

        score(0, 0)
        for j in range(n_blocks):
            if j + 1 < n_blocks:
                score(j + 1, (j + 1) % 2)
            softmax(j, j % 2)
            if j > 0:
                output(j - 1, (j - 1) % 2)
        output(n_blocks - 1, (n_blocks - 1) % 2)


def _na(naq, nak, nav, bias, n_ctx):
    b, t_len, qw = naq.shape
    n_tiles = t_len // TM
    rows = (t_len - n_ctx) // GRID_W
    n_keys = n_ctx + NA_WIN

    def bias_index(b, t):
        return (0, jnp.where(t <= 1, 0, jnp.where(t == n_tiles - 1, 2, 1)), 0, 0)

    def resident(width):
        return pl.BlockSpec((1, t_len, width), lambda b, t: (b, 0, 0), pipeline_mode=pl.Buffered(1))

    return pl.pallas_call(
        functools.partial(_na_kernel, n_ctx=n_ctx, rows=rows),
        grid=(b, n_tiles),
        in_specs=[pl.BlockSpec((1, TM, qw), lambda b, t: (b, t, 0)), resident(qw), resident(2 * qw),
                  pl.BlockSpec((bias.shape[0], 1, TM, NA_WIN), bias_index, pipeline_mode=pl.Buffered(1))],
        out_specs=pl.BlockSpec((1, TM, qw), lambda b, t: (b, t, 0)),
        out_shape=jax.ShapeDtypeStruct((b, t_len, qw), BF16),
        scratch_shapes=(_flash_scratch(qw // LANE)
                        + [pltpu.VMEM((2 * TM, n_keys), F32)] * 2 + [pltpu.VMEM((2 * TM, n_keys), BF16)] * 2),
        compiler_params=_params(2),
        name="neighbourhood_attention",
    )(naq, nak, nav, bias)


def _na_bias_table(rpb, rows):
    g_of_pattern = np.array([0, 2, rows // NA_ROWS_PER_TILE - 1])
    a = np.arange(NA_ROWS_PER_TILE)
    r = NA_ROWS_PER_TILE * g_of_pattern[:, None] + a[None, :]
    start = np.clip(NA_ROWS_PER_TILE * g_of_pattern - WIN_R // 2, 0, rows - NA_KEY_ROWS)
    rs = np.clip(r - WIN_R // 2, 0, rows - WIN_R)
    key_row = start[:, None] + np.arange(NA_KEY_ROWS)[None, :]
    row_ok = (key_row[:, None, :] >= rs[:, :, None]) & (key_row[:, None, :] < rs[:, :, None] + WIN_R)
    row_off = np.clip(key_row[:, None, :] - r[:, :, None] + (WIN_R - 1), 0, 2 * WIN_R - 2)
    cols = np.arange(GRID_W)
    col_start = np.clip(cols - WIN_C // 2, 0, GRID_W - WIN_C)
    col_ok = (cols[None, :] >= col_start[:, None]) & (cols[None, :] < col_start[:, None] + WIN_C)
    col_off = np.clip(cols[None, :] - cols[:, None] + (WIN_C - 1), 0, 2 * WIN_C - 2)
    ok = row_ok[:, :, None, :, None] & col_ok[None, None, :, None, :]
    bias_rows = rpb.astype(F32)[:, row_off]
    pick_col = (col_off[:, :, None] == np.arange(2 * WIN_C - 1)).astype(np.float32)
    vals = jnp.einsum('hpakb,cjb->hpackj', bias_rows, pick_col, precision=lax.Precision.HIGHEST)
    table = jnp.where(ok[None], vals * LOG2_E, NEG)
    return table.reshape(rpb.shape[0], 3, TM, NA_WIN)


def _diff_kernel(lam_ref, gain_ref, *refs, tk, lambda_init):
    q_refs = refs[:DIFF_Q_TILES]
    k_ref, v_ref, o_ref, q2_ref, m_ref, acc_ref = refs[DIFF_Q_TILES:DIFF_Q_TILES + 6]
    pipe_refs = refs[DIFF_Q_TILES + 6:]
    n_heads = o_ref.shape[2] // LANE
    _init_flash(q_refs, q2_ref, m_ref, acc_ref)

    def chunk_of(ref, width):
        def chunk(i, n):
            h = n % n_heads
            return ref[0, pl.ds(pl.multiple_of(i * tk, tk), tk), width * h:width * (h + 1)]
        return chunk
    _flash_pipeline(q2_ref, chunk_of(k_ref, LANE), chunk_of(v_ref, 2 * LANE), k_ref.shape[1] // tk,
                    DIFF_Q_TILES * n_heads, m_ref, acc_ref, pipe_refs[0:2], pipe_refs[2:4], pipe_refs[4:6])

    lp = lam_ref[...]
    lam = (jnp.exp(jnp.sum(lp[0:1] * lp[1:2], axis=-1, keepdims=True))
           - jnp.exp(jnp.sum(lp[2:3] * lp[3:4], axis=-1, keepdims=True)) + lambda_init)

    def sub_layer_norm(o):
        d = o[0:TM] - lam * o[TM:2 * TM]
        y = d * lax.rsqrt(jnp.mean(d * d, axis=-1, keepdims=True) + EPS) * gain_ref[...]
        return y * (1.0 - lambda_init)
    _store_heads(o_ref, acc_ref, DIFF_Q_TILES, sub_layer_norm)


def _diff(lam_params, subln_gain, dq, dk, dv, n_ctx, tk, lambda_init):
    b, t_len, qw = dq.shape
    assert n_ctx == TM and (t_len - n_ctx) % (DIFF_Q_TILES * TM) == 0
    hw = LANE * DIFF_HEADS_PER_STEP
    return pl.pallas_call(
        functools.partial(_diff_kernel, tk=tk, lambda_init=lambda_init),
        grid=(b, qw // hw, (t_len - n_ctx) // (DIFF_Q_TILES * TM)),
        in_specs=([pl.BlockSpec(lam_params.shape, lambda b, h, s: (0, 0)),
                   pl.BlockSpec(subln_gain.shape, lambda b, h, s: (0, 0))]
                  + [pl.BlockSpec((1, TM, hw), lambda b, h, s, u=u: (b, 1 + DIFF_Q_TILES * s + u, h))
                     for u in range(DIFF_Q_TILES)]
                  + [pl.BlockSpec((1, t_len, hw), lambda b, h, s: (b, 0, h)),
                     pl.BlockSpec((1, t_len, 2 * hw), lambda b, h, s: (b, 0, h))]),
        out_specs=pl.BlockSpec((1, DIFF_Q_TILES * TM, hw), lambda b, h, s: (b, s, h)),
        out_shape=jax.ShapeDtypeStruct((b, t_len - n_ctx, qw), BF16),
        scratch_shapes=_flash_scratch(DIFF_Q_TILES * DIFF_HEADS_PER_STEP) + _pipeline_scratch(tk),
        compiler_params=_params(3),
        name="diff_attention",
    )(lam_params, subln_gain, *([dq] * DIFF_Q_TILES), dk, dv)


def _ffn_kernel(*refs, n_att, ff_chunks, final, split_tokens):
    if split_tokens:
        x_tile = _token_tile(refs[0], refs[1])
        refs = refs[1:]
    else:
        x_tile = refs[0][0]
    mod_ref = refs[1]
    att_refs = refs[2:2 + n_att]
    wo_refs = refs[2 + n_att:2 + 2 * n_att]
    wg_ref, wu_ref, wd_ref = refs[2 + 2 * n_att:5 + 2 * n_att]
    o_ref = refs[-1]
    mod = mod_ref[0, 0]
    y = _dot(att_refs[0][0], wo_refs[0][...])
    for a_ref, w_ref in zip(att_refs[1:], wo_refs[1:]):
        y = y + _dot(a_ref[0], w_ref[...])
    x1 = x_tile + mod[2:3] * y
    h = _modulated_norm(x1, mod[3:4], mod[4:5]).astype(BF16)
    d_ff = wg_ref.shape[1]
    bounds = [0]
    for i in range(ff_chunks):
        bounds.append(min(d_ff, -(-(d_ff * (i + 1) // ff_chunks) // MXU_DEPTH) * MXU_DEPTH))
    down = None
    for lo, hi in zip(bounds[:-1], bounds[1:]):
        g = _dot(h, wg_ref[:, lo:hi])
        u = _dot(h, wu_ref[:, lo:hi])
        a = (g * (1.0 / (1.0 + jnp.exp(-g))) * u).astype(BF16)
        part = _dot(a, wd_ref[lo:hi, :])
        down = part if down is None else down + part
    x2 = x1 + mod[5:6] * down
    if final:
        gain_ref = refs[5 + 2 * n_att]
        x2 = x2 * lax.rsqrt(jnp.mean(x2 * x2, axis=-1, keepdims=True) + EPS) * gain_ref[...]
    o_ref[0] = x2


def _ffn(tokens, mods, atts, wos, wg, wu, wd, final_gain=None):
    split_tokens = isinstance(tokens, tuple)
    n_rows = atts[0].shape[1]
    b, _, d = tokens[-1].shape if split_tokens else tokens.shape
    final = final_gain is not None
    if split_tokens:
        token_specs = _split_token_specs(d)
        tokens = list(tokens)
    else:
        token_specs = [_token_spec(d, (tokens.shape[1] - n_rows) // TM)]
        tokens = [tokens]
    in_specs = (token_specs + [_mod_spec(d, latent_only=not split_tokens)]
                + [_token_spec(a.shape[2]) for a in atts]
                + [_const_spec(w.shape) for w in wos]
                + [_const_spec(wg.shape), _const_spec(wu.shape), _const_spec(wd.shape)])
    args = [*tokens, mods, *atts, *wos, wg, wu, wd]
    if final:
        in_specs.append(_const_spec(final_gain.shape))
        args.append(final_gain)
    return pl.pallas_call(
        functools.partial(_ffn_kernel, n_att=len(atts), ff_chunks=2, final=final, split_tokens=split_tokens),
        grid=(b, n_rows // TM),
        in_specs=in_specs,
        out_specs=_token_spec(d),
        out_shape=jax.ShapeDtypeStruct((b, n_rows, d), F32),
        compiler_params=_params(2),
        name="outproj_ffn_final" if final else "outproj_ffn",
    )(*args)


def _head_cols(base, heads):
    return np.concatenate([base + HEAD_DIM * h + _DEINT for h in heads])


def _rope_tables(seq, n_ctx):
    t = jnp.arange(seq)
    row = (t // GRID_W).astype(F32)
    col = (t % GRID_W).astype(F32)
    n_freq = HEAD_DIM // 4
    inv = ROPE_THETA ** (-jnp.arange(n_freq, dtype=F32) / n_freq)
    ang = jnp.concatenate([row[:, None] * inv, col[:, None] * inv], axis=-1)
    cos, sin = jnp.cos(ang), jnp.sin(ang)
    zero = jnp.zeros_like(sin)

    def table(first_half, second_half, ctx_value):
        lat = jnp.tile(jnp.concatenate([first_half, second_half], axis=-1), (1, LANE // HEAD_DIM))
        return jnp.concatenate([jnp.full((n_ctx, LANE), ctx_value, F32), lat], axis=0)

    return table(cos, cos, 1.0), table(zero, sin, 0.0), table(-sin, zero, 0.0)


def kernel(x, c, ctx, c_ctx, ada_w, ada_b, ffn_w_gate, ffn_w_up, ffn_w_down, par_w_in, par_w_out, na_rpb,
           gqa_q_gain, gqa_k_gain, diff_w_in, diff_w_out, diff_lambda_q1, diff_lambda_k1, diff_lambda_q2,
           diff_lambda_k2, diff_subln_gain, final_norm_gain):
    b, seq, d = x.shape
    n_ctx = ctx.shape[1]
    assert n_ctx == TM and seq % TM == 0 and d % LANE == 0 and b < 8 and ada_w.shape[0] == 2
    assert par_w_in.shape[-1] == PAR_W and diff_w_in.shape[-1] == 3 * DIFF_W
    rows = seq // GRID_W
    scale = HEAD_DIM ** -0.5 * LOG2_E

    cond = jnp.zeros((8, d), F32).at[:b].set(c).at[b].set(c_ctx)
    mods_all = _ada(cond, ada_w, ada_b).reshape(2, 8, N_MOD, d)

    def mods_of(layer):
        m = mods_all[layer]
        return jnp.stack([jnp.broadcast_to(m[b], (b, N_MOD, d)), m[:b]], axis=1)

    rope = _rope_tables(seq, n_ctx)

    cols0 = np.concatenate([np.arange(NAQ0, GQ0), _head_cols(GQ0, _GQA_HEAD_ORDER), _head_cols(GK0, range(N_HEADS_GKV)),
                            np.arange(GV0, PAR_W)])
    col_scale0 = np.ones((PAR_W,), np.float32)
    col_scale0[NAQ0:NAK0] = scale
    w_in0 = (par_w_in[0][:, cols0] * col_scale0).astype(BF16)
    gains = jnp.zeros((8, LANE), F32)
    gains = gains.at[0].set(jnp.tile(gqa_q_gain[0][_DEINT] * scale, 2)).at[1].set(jnp.tile(gqa_k_gain[0][_DEINT], 2))
    block_mean = jnp.asarray(np.kron(np.eye(LANE // HEAD_DIM), np.full((HEAD_DIM, HEAD_DIM), 1.0 / HEAD_DIM)), BF16)
    naq, nak, nav, gq, gk, gv = _proj_par(ctx, x, mods_of(0), w_in0, rope, gains, block_mean)
    att_na = _na(naq, nak, nav, _na_bias_table(na_rpb[0], rows), n_ctx)
    att_g = _gqa(gq, gk, gv, n_ctx, tk=_key_chunk(n_ctx + seq))
    wo_na = par_w_out[0][0:NA_W].astype(BF16)
    wo_g = par_w_out[0][NA_W + np.concatenate([HEAD_DIM * h + np.arange(HEAD_DIM) for h in _GQA_HEAD_ORDER])].astype(BF16)
    xa = _ffn((ctx, x), mods_of(0), [att_na, att_g], [wo_na, wo_g],
              ffn_w_gate[0].astype(BF16), ffn_w_up[0].astype(BF16), ffn_w_down[0].astype(BF16))

    sub_heads = range(2 * N_HEADS_DIFF)
    cols1 = np.concatenate([_head_cols(0, sub_heads), _head_cols(DIFF_W, sub_heads), np.arange(2 * DIFF_W, 3 * DIFF_W)])
    col_scale1 = np.ones((3 * DIFF_W,), np.float32)
    col_scale1[0:DIFF_W] = scale
    w_in1 = (diff_w_in[0][:, cols1] * col_scale1).astype(BF16)
    dq, dk, dv = _proj_diff(xa, mods_of(1), w_in1, rope)
    lambda_init = 0.8 - 0.6 * float(np.exp(-0.3 * 1))
    lam_params = jnp.stack([diff_lambda_q1[0], diff_lambda_k1[0], diff_lambda_q2[0], diff_lambda_k2[0]]).astype(F32)
    att_d = _diff(lam_params, diff_subln_gain[0].reshape(1, -1).astype(F32), dq, dk, dv, n_ctx,
                  tk=_key_chunk(n_ctx + seq),
                  lambda_init=lambda_init)
    return _ffn(xa, mods_of(1), [att_d], [diff_w_out[0].astype(BF16)],
                ffn_w_gate[1].astype(BF16), ffn_w_up[1].astype(BF16), ffn_w_down[1].astype(BF16),
                final_gain=final_norm_gain.reshape(1, -1).astype(F32))
```

```python
import functools

import numpy as np
import jax
import jax.numpy as jnp
from jax import lax
from jax.experimental import pallas as pl
from jax.experimental.pallas import tpu as pltpu

F32 = jnp.float32
BF16 = jnp.bfloat16

GRID_W = 64
HEAD_DIM = 64
WIN_R = 8
WIN_C = 16
N_MOD = 6
ROPE_THETA = 10000.0
EPS = 1e-6

LANE = 128
TM = 256
NA_ROWS_PER_TILE = TM // GRID_W
NA_KEY_ROWS = 12
NA_WIN = NA_KEY_ROWS * GRID_W
ROW_BLOCK = 128
DIFF_HEADS_PER_STEP = 2
GQA_Q_TILES = 2
DIFF_Q_TILES = 4
SOFTMAX_ROWS = 32
LOG2_E = 1.4426950408889634
MXU_DEPTH = 256
NEG = -1e30
VMEM_LIMIT = 56 * 1024 * 1024

N_HEADS_NA = 8
N_HEADS_GQ = 8
N_HEADS_GKV = 2
N_HEADS_DIFF = 8
NA_W = N_HEADS_NA * HEAD_DIM
GQ_W = N_HEADS_GQ * HEAD_DIM
GKV_W = N_HEADS_GKV * HEAD_DIM
NAQ0, NAK0, NAV0, GQ0, GK0, GV0, PAR_W = (int(v) for v in np.cumsum([0, NA_W, NA_W, NA_W, GQ_W, GKV_W, GKV_W]))
DIFF_W = N_HEADS_DIFF * 2 * HEAD_DIM
ADA_TILE = 1024

_DEINT = np.concatenate([np.arange(0, HEAD_DIM, 2), np.arange(1, HEAD_DIM, 2)])
_GQA_HEAD_ORDER = (0, 4, 1, 5, 2, 6, 3, 7)


def _dot(a, b):
    return jnp.dot(a, b, preferred_element_type=F32)


def _dot_nt(a, b):
    return lax.dot_general(a, b, (((1,), (1,)), ((), ())), preferred_element_type=F32)


def _params(n_grid):
    return pltpu.CompilerParams(dimension_semantics=("arbitrary",) * n_grid, vmem_limit_bytes=VMEM_LIMIT)


def _const_spec(shape):
    return pl.BlockSpec(shape, lambda *_: (0,) * len(shape), pipeline_mode=pl.Buffered(1))


def _split_bf16(a):
    hi = a.astype(BF16)
    return hi, (a - hi.astype(F32)).astype(BF16)


def _ada_kernel(cond_ref, w_ref, b_ref, o_ref):
    c = cond_ref[...]
    a_hi, a_lo = _split_bf16(c * (1.0 / (1.0 + jnp.exp(-c))))
    w_hi, w_lo = _split_bf16(w_ref[0])
    o_ref[0] = _dot(a_hi, w_hi) + _dot(a_lo, w_hi) + _dot(a_hi, w_lo) + b_ref[0]


def _ada(cond, ada_w, ada_b):
    depth, d, n = ada_w.shape
    tn = ADA_TILE
    return pl.pallas_call(
        _ada_kernel,
        grid=(depth, n // tn),
        in_specs=[pl.BlockSpec(cond.shape, lambda l, j: (0, 0)),
                  pl.BlockSpec((1, d, tn), lambda l, j: (l, 0, j)),
                  pl.BlockSpec((1, 1, tn), lambda l, j: (l, 0, j))],
        out_specs=pl.BlockSpec((1, cond.shape[0], tn), lambda l, j: (l, 0, j)),
        out_shape=jax.ShapeDtypeStruct((depth, cond.shape[0], n), F32),
        compiler_params=_params(2),
        name="ada_modulation",
    )(cond, ada_w, ada_b.reshape(depth, 1, n))


def _modulated_norm(x, shift, scale):
    ms = jnp.mean(x * x, axis=-1, keepdims=True)
    return (x * lax.rsqrt(ms + EPS)) * (1.0 + scale) + shift


def _group_mean_sq(x, bd):
    hi, lo = _split_bf16(x * x)
    return _dot(hi, bd) + _dot(lo, bd)


def _rope(x, c, sa, sb):
    return x * c + pltpu.roll(x, 32, 1) * sa + pltpu.roll(x, 96, 1) * sb


def _ones_column(rows):
    return (lax.broadcasted_iota(jnp.int32, (rows, LANE), 1) == 0).astype(BF16)


def _store_values(v_ref, p, col0, n_blocks):
    ones = _ones_column(p.shape[0])
    for j in range(n_blocks):
        v_ref[0, :, 2 * LANE * j:2 * LANE * j + LANE] = p[:, col0 + LANE * j:col0 + LANE * (j + 1)].astype(BF16)
        v_ref[0, :, 2 * LANE * j + LANE:2 * LANE * (j + 1)] = ones


def _token_tile(ctx_ref, x_ref):
    return jnp.where(pl.program_id(1) == 0, ctx_ref[0], x_ref[0])


def _proj_par_kernel(ctx_ref, x_ref, mod_ref, w_ref, c_ref, sa_ref, sb_ref, gain_ref, bd_ref,
                     naq_ref, nak_ref, nav_ref, gq_ref, gk_ref, gv_ref):
    mod = mod_ref[0, 0]
    h = _modulated_norm(_token_tile(ctx_ref, x_ref), mod[0:1], mod[1:2]).astype(BF16)
    p = _dot(h, w_ref[...])
    naq_ref[0] = p[:, NAQ0:NAK0].astype(BF16)
    nak_ref[0] = p[:, NAK0:NAV0].astype(BF16)
    _store_values(nav_ref, p, NAV0, NA_W // LANE)
    c, sa, sb, bd = c_ref[...], sa_ref[...], sb_ref[...], bd_ref[...]

    def qk_norm_rope(g, gain):
        g = g * lax.rsqrt(_group_mean_sq(g, bd) + EPS) * gain
        return _rope(g, c, sa, sb).astype(BF16)

    for j in range(GQ_W // LANE):
        gq_ref[0, :, LANE * j:LANE * (j + 1)] = qk_norm_rope(p[:, GQ0 + LANE * j:GQ0 + LANE * (j + 1)], gain_ref[0:1, :])
    gk_ref[0] = qk_norm_rope(p[:, GK0:GV0], gain_ref[1:2, :])
    _store_values(gv_ref, p, GV0, GKV_W // LANE)


def _proj_diff_kernel(x_ref, mod_ref, w_ref, c_ref, sa_ref, sb_ref, dq_ref, dk_ref, dv_ref):
    mod = mod_ref[0, 0]
    h = _modulated_norm(x_ref[0], mod[0:1], mod[1:2]).astype(BF16)
    p = _dot(h, w_ref[...])
    c, sa, sb = c_ref[...], sa_ref[...], sb_ref[...]
    for j in range(DIFF_W // LANE):
        dq_ref[0, :, LANE * j:LANE * (j + 1)] = _rope(p[:, LANE * j:LANE * (j + 1)], c, sa, sb).astype(BF16)
        dk_ref[0, :, LANE * j:LANE * (j + 1)] = _rope(p[:, DIFF_W + LANE * j:DIFF_W + LANE * (j + 1)], c, sa, sb).astype(BF16)
    _store_values(dv_ref, p, 2 * DIFF_W, DIFF_W // LANE)


def _token_spec(width, tile_off=0):
    return pl.BlockSpec((1, TM, width), lambda b, t: (b, t + tile_off, 0))


def _split_token_specs(d):
    return [pl.BlockSpec((1, TM, d), lambda b, t: (b, 0, 0)),
            pl.BlockSpec((1, TM, d), lambda b, t: (b, jnp.maximum(t - 1, 0), 0))]


def _mod_spec(d, latent_only=False):
    if latent_only:
        return pl.BlockSpec((1, 1, N_MOD, d), lambda b, t: (b, 1, 0, 0))
    return pl.BlockSpec((1, 1, N_MOD, d), lambda b, t: (b, jnp.minimum(t, 1), 0, 0))


def _rope_spec():
    return pl.BlockSpec((TM, LANE), lambda b, t: (t, 0))


def _proj_par(ctx, x, mods, w, rope, gains, bd):
    b, seq, d = x.shape
    t_len = ctx.shape[1] + seq
    widths = (NA_W, NA_W, 2 * NA_W, GQ_W, GKV_W, 2 * GKV_W)
    return pl.pallas_call(
        _proj_par_kernel,
        grid=(b, t_len // TM),
        in_specs=_split_token_specs(d) + [_mod_spec(d), _const_spec(w.shape), _rope_spec(), _rope_spec(), _rope_spec(),
                                          _const_spec(gains.shape), _const_spec(bd.shape)],
        out_specs=[_token_spec(n) for n in widths],
        out_shape=[jax.ShapeDtypeStruct((b, t_len, n), BF16) for n in widths],
        compiler_params=_params(2),
        name="proj_parallel_mixer",
    )(ctx, x, mods, w, *rope, gains, bd)


def _proj_diff(xa, mods, w, rope):
    b, t_len, d = xa.shape
    widths = (DIFF_W, DIFF_W, 2 * DIFF_W)
    return pl.pallas_call(
        _proj_diff_kernel,
        grid=(b, t_len // TM),
        in_specs=[_token_spec(d), _mod_spec(d), _const_spec(w.shape), _rope_spec(), _rope_spec(), _rope_spec()],
        out_specs=[_token_spec(n) for n in widths],
        out_shape=[jax.ShapeDtypeStruct((b, t_len, n), BF16) for n in widths],
        compiler_params=_params(2),
        name="proj_diff_mixer",
    )(xa, mods, w, *rope)


def _split_heads(q):
    lane = lax.broadcasted_iota(jnp.int32, q.shape, 1)
    zero = jnp.zeros_like(q)
    return jnp.concatenate([jnp.where(lane < HEAD_DIM, q, zero), jnp.where(lane >= HEAD_DIM, q, zero)], axis=0)


def _softmax_update(s, m_prev):
    m_new = jnp.maximum(m_prev, jnp.max(s, axis=-1, keepdims=True))
    p = jnp.exp2(s - m_new)
    return p, jnp.exp2(m_prev - m_new), m_new


def _flash_chunk(q2_ref, k, v, m_ref, acc_ref, row0):
    for r in range(2 * TM // ROW_BLOCK):
        rows = slice(row0 + ROW_BLOCK * r, row0 + ROW_BLOCK * (r + 1))
        s = _dot_nt(q2_ref[rows], k)
        p, alpha, m_new = _softmax_update(s, m_ref[rows])
        acc_ref[rows] = alpha * acc_ref[rows] + _dot(p.astype(BF16), v)
        m_ref[rows] = m_new


def _flash_pipeline(q2_ref, k_of, v_of, n_chunks, n_blocks, mrep_ref, acc_ref, s_refs, p_refs, alpha_refs):
    assert n_blocks % 2 == 0
    r2 = 2 * TM

    def score(i, j, slot):
        s_refs[slot][...] = _dot_nt(q2_ref[r2 * j:r2 * (j + 1)], k_of(i, j))

    def softmax(j, slot):
        for r in range(r2 // SOFTMAX_ROWS):
            rows = slice(SOFTMAX_ROWS * r, SOFTMAX_ROWS * (r + 1))
            state_rows = slice(r2 * j + SOFTMAX_ROWS * r, r2 * j + SOFTMAX_ROWS * (r + 1))
            s = s_refs[slot][rows]
            m_prev = mrep_ref[state_rows]
            m_new = jnp.maximum(m_prev, jnp.max(s, axis=-1, keepdims=True))
            p_refs[slot][rows] = jnp.exp2(s - pltpu.repeat(m_new, s.shape[1] // LANE, 1)).astype(BF16)
            alpha_refs[slot][rows] = jnp.exp2(m_prev - m_new)
            mrep_ref[state_rows] = m_new

    def accumulate(i, j, slot):
        blk = slice(r2 * j, r2 * (j + 1))
        acc_ref[blk] = pltpu.repeat(alpha_refs[slot][...], 2, 1) * acc_ref[blk] + _dot(p_refs[slot][...], v_of(i, j))

    mrep_ref[...] = jnp.full(mrep_ref.shape, NEG, F32)
    last_slot = (n_blocks - 1) % 2
    p_refs[last_slot][...] = jnp.zeros(p_refs[last_slot].shape, BF16)
    alpha_refs[last_slot][...] = jnp.ones(alpha_refs[last_slot].shape, F32)
    score(0, 0, 0)

    def body(i, carry):
        for j in range(n_blocks):
            if j + 1 < n_blocks:
                score(i, j + 1, (j + 1) % 2)
            else:
                score(jnp.minimum(i + 1, n_chunks - 1), 0, 0)
            softmax(j, j % 2)
            if j > 0:
                accumulate(i, j - 1, (j - 1) % 2)
            else:
                accumulate(jnp.maximum(i - 1, 0), n_blocks - 1, last_slot)
        return carry

    lax.fori_loop(0, n_chunks, body, 0)
    accumulate(n_chunks - 1, n_blocks - 1, last_slot)


def _pipeline_scratch(tk):
    rows = 2 * TM
    return [pltpu.VMEM((rows, tk), F32)] * 2 + [pltpu.VMEM((rows, tk), BF16)] * 2 + [pltpu.VMEM((rows, LANE), F32)] * 2


def _normalize(acc):
    return acc[:, 0:LANE] / acc[:, LANE:LANE + 1]


def _merge_heads(o):
    lane = lax.broadcasted_iota(jnp.int32, (TM, LANE), 1)
    return jnp.where(lane < HEAD_DIM, o[0:TM], o[TM:2 * TM])


def _init_flash(q_refs, q2_ref, m_ref, acc_ref):
    n = 0
    for q_ref in q_refs:
        for j in range(q_ref.shape[2] // LANE):
            q2_ref[2 * TM * n:2 * TM * (n + 1)] = _split_heads(q_ref[0, :, LANE * j:LANE * (j + 1)])
            n += 1
    m_ref[...] = jnp.full(m_ref.shape, NEG, F32)
    acc_ref[...] = jnp.zeros(acc_ref.shape, F32)


def _flash_scratch(n_blocks):
    rows = 2 * TM * n_blocks
    return [pltpu.VMEM((rows, LANE), BF16), pltpu.VMEM((rows, 1), F32), pltpu.VMEM((rows, 2 * LANE), F32)]


def _key_chunk(t_len):
    return next(tk for tk in (768, 640, 512, 384, 256) if t_len % tk == 0)


def _store_heads(o_ref, acc_ref, n_tiles, finish):
    n_lane_blocks = o_ref.shape[2] // LANE
    for u in range(n_tiles):
        for j in range(n_lane_blocks):
            n = u * n_lane_blocks + j
            o = _normalize(acc_ref[2 * TM * n:2 * TM * (n + 1)])
            o_ref[0, TM * u:TM * (u + 1), LANE * j:LANE * (j + 1)] = finish(o).astype(BF16)


def _gqa_ctx_kernel(q_ref, k_ref, v_ref, o_ref, q2_ref, m_ref, acc_ref):
    _init_flash([q_ref], q2_ref, m_ref, acc_ref)
    for j in range(q_ref.shape[2] // LANE):
        _flash_chunk(q2_ref, k_ref[0], v_ref[0], m_ref, acc_ref, 2 * TM * j)
    _store_heads(o_ref, acc_ref, 1, _merge_heads)


def _gqa_kernel(*refs, tk):
    q_refs = refs[:GQA_Q_TILES]
    k_ref, v_ref, o_ref, q2_ref, m_ref, acc_ref = refs[GQA_Q_TILES:GQA_Q_TILES + 6]
    pipe_refs = refs[GQA_Q_TILES + 6:]
    _init_flash(q_refs, q2_ref, m_ref, acc_ref)

    def chunk_of(ref):
        return lambda i, n: ref[0, pl.ds(pl.multiple_of(i * tk, tk), tk), :]
    _flash_pipeline(q2_ref, chunk_of(k_ref), chunk_of(v_ref), k_ref.shape[1] // tk,
                    GQA_Q_TILES * (o_ref.shape[2] // LANE), pipe_refs[6], acc_ref,
                    pipe_refs[0:2], pipe_refs[2:4], pipe_refs[4:6])
    _store_heads(o_ref, acc_ref, GQA_Q_TILES, _merge_heads)


def _gqa(gq, gk, gv, n_ctx, tk):
    b, t_len, qw = gq.shape
    assert n_ctx == TM and (t_len - n_ctx) % (GQA_Q_TILES * TM) == 0
    n_lane_blocks = qw // LANE
    ctx_out = pl.pallas_call(
        _gqa_ctx_kernel,
        grid=(b,),
        in_specs=[pl.BlockSpec((1, TM, qw), lambda b: (b, 0, 0)),
                  pl.BlockSpec((1, TM, LANE), lambda b: (b, 0, 0)),
                  pl.BlockSpec((1, TM, 2 * LANE), lambda b: (b, 0, 0))],
        out_specs=pl.BlockSpec((1, TM, qw), lambda b: (b, 0, 0)),
        out_shape=jax.ShapeDtypeStruct((b, TM, qw), BF16),
        scratch_shapes=_flash_scratch(n_lane_blocks),
        compiler_params=_params(1),
        name="gqa_attention_ctx",
    )(gq, gk, gv)
    latent_out = pl.pallas_call(
        functools.partial(_gqa_kernel, tk=tk),
        grid=(b, (t_len - n_ctx) // (GQA_Q_TILES * TM)),
        in_specs=([pl.BlockSpec((1, TM, qw), lambda b, s, u=u: (b, 1 + GQA_Q_TILES * s + u, 0))
                   for u in range(GQA_Q_TILES)]
                  + [pl.BlockSpec((1, t_len, LANE), lambda b, s: (b, 0, 0)),
                     pl.BlockSpec((1, t_len, 2 * LANE), lambda b, s: (b, 0, 0))]),
        out_specs=pl.BlockSpec((1, GQA_Q_TILES * TM, qw), lambda b, s: (b, s, 0)),
        out_shape=jax.ShapeDtypeStruct((b, t_len - n_ctx, qw), BF16),
        scratch_shapes=_flash_scratch(GQA_Q_TILES * n_lane_blocks) + _pipeline_scratch(tk) + [pltpu.VMEM((2 * TM * GQA_Q_TILES * n_lane_blocks, LANE), F32)],
        compiler_params=_params(2),
        name="gqa_attention",
    )(*([gq] * GQA_Q_TILES), gk, gv)
    return jnp.concatenate([ctx_out, latent_out], axis=1)


def _na_kernel(q_ref, k_ref, v_ref, bias_ref, o_ref, q2_ref, m_ref, acc_ref, *pipe_refs, n_ctx, rows):
    t = pl.program_id(1)
    n_blocks = q_ref.shape[2] // LANE
    r2 = 2 * TM
    s_refs, p_refs = pipe_refs[0:2], pipe_refs[2:4]
    _init_flash([q_ref], q2_ref, m_ref, acc_ref)

    @pl.when(t == 0)
    def _():
        for j in range(n_blocks):
            _flash_chunk(q2_ref, k_ref[0, 0:n_ctx, LANE * j:LANE * (j + 1)],
                         v_ref[0, 0:n_ctx, 2 * LANE * j:2 * LANE * (j + 1)], m_ref, acc_ref, r2 * j)
            o_ref[0, :, LANE * j:LANE * (j + 1)] = _merge_heads(_normalize(acc_ref[r2 * j:r2 * (j + 1)])).astype(BF16)

    @pl.when(t > 0)
    def _():
        first_row = NA_ROWS_PER_TILE * (t - 1)
        start = jnp.clip(first_row - WIN_R // 2, 0, rows - NA_KEY_ROWS)
        off = pl.multiple_of(n_ctx + GRID_W * start, GRID_W)

        def score(j, slot):
            q2 = q2_ref[r2 * j:r2 * (j + 1)]
            s_refs[slot][:, 0:n_ctx] = _dot_nt(q2, k_ref[0, 0:n_ctx, LANE * j:LANE * (j + 1)])
            s_refs[slot][:, n_ctx:] = _dot_nt(q2, k_ref[0, pl.ds(off, NA_WIN), LANE * j:LANE * (j + 1)])

        def softmax(j, slot):
            for r in range(r2 // SOFTMAX_ROWS):
                rows_r = slice(SOFTMAX_ROWS * r, SOFTMAX_ROWS * (r + 1))
                head, row = divmod(SOFTMAX_ROWS * r, TM)
                s_ctx = s_refs[slot][rows_r, 0:n_ctx]
                s_win = s_refs[slot][rows_r, n_ctx:] + bias_ref[2 * j + head, 0, row:row + SOFTMAX_ROWS, :]
                m = jnp.maximum(jnp.max(s_ctx, axis=-1, keepdims=True), jnp.max(s_win, axis=-1, keepdims=True))
                p_refs[slot][rows_r, 0:n_ctx] = jnp.exp2(s_ctx - m).astype(BF16)
                p_refs[slot][rows_r, n_ctx:] = jnp.exp2(s_win - m).astype(BF16)

        def output(j, slot):
            acc = (_dot(p_refs[slot][:, 0:n_ctx], v_ref[0, 0:n_ctx, 2 * LANE * j:2 * LANE * (j + 1)])
                   + _dot(p_refs[slot][:, n_ctx:], v_ref[0, pl.ds(off, NA_WIN), 2 * LANE * j:2 * LANE * (j + 1)]))
            o_ref[0, :, LANE * j:LANE * (j + 1)] = _merge_heads(_normalize(acc)).astype(BF16)

        score(0, 0)
        for j in range(n_blocks):
            if j + 1 < n_blocks:
                score(j + 1, (j + 1) % 2)
            softmax(j, j % 2)
            if j > 0:
                output(j - 1, (j - 1) % 2)
        output(n_blocks - 1, (n_blocks - 1) % 2)


def _na(naq, nak, nav, bias, n_ctx):
    b, t_len, qw = naq.shape
    n_tiles = t_len // TM
    rows = (t_len - n_ctx) // GRID_W
    n_keys = n_ctx + NA_WIN

    def bias_index(b, t):
        return (0, jnp.where(t <= 1, 0, jnp.where(t == n_tiles - 1, 2, 1)), 0, 0)

    def resident(width):
        return pl.BlockSpec((1, t_len, width), lambda b, t: (b, 0, 0), pipeline_mode=pl.Buffered(1))

    return pl.pallas_call(
        functools.partial(_na_kernel, n_ctx=n_ctx, rows=rows),
        grid=(b, n_tiles),
        in_specs=[pl.BlockSpec((1, TM, qw), lambda b, t: (b, t, 0)), resident(qw), resident(2 * qw),
                  pl.BlockSpec((bias.shape[0], 1, TM, NA_WIN), bias_index, pipeline_mode=pl.Buffered(1))],
        out_specs=pl.BlockSpec((1, TM, qw), lambda b, t: (b, t, 0)),
        out_shape=jax.ShapeDtypeStruct((b, t_len, qw), BF16),
        scratch_shapes=(_flash_scratch(qw // LANE)
                        + [pltpu.VMEM((2 * TM, n_keys), F32)] * 2 + [pltpu.VMEM((2 * TM, n_keys), BF16)] * 2),
        compiler_params=_params(2),
        name="neighbourhood_attention",
    )(naq, nak, nav, bias)


def _na_bias_table(rpb, rows):
    g_of_pattern = np.array([0, 2, rows // NA_ROWS_PER_TILE - 1])
    a = np.arange(NA_ROWS_PER_TILE)
    r = NA_ROWS_PER_TILE * g_of_pattern[:, None] + a[None, :]
    start = np.clip(NA_ROWS_PER_TILE * g_of_pattern - WIN_R // 2, 0, rows - NA_KEY_ROWS)
    rs = np.clip(r - WIN_R // 2, 0, rows - WIN_R)
    key_row = start[:, None] + np.arange(NA_KEY_ROWS)[None, :]
    row_ok = (key_row[:, None, :] >= rs[:, :, None]) & (key_row[:, None, :] < rs[:, :, None] + WIN_R)
    row_off = np.clip(key_row[:, None, :] - r[:, :, None] + (WIN_R - 1), 0, 2 * WIN_R - 2)
    cols = np.arange(GRID_W)
    col_start = np.clip(cols - WIN_C // 2, 0, GRID_W - WIN_C)
    col_ok = (cols[None, :] >= col_start[:, None]) & (cols[None, :] < col_start[:, None] + WIN_C)
    col_off = np.clip(cols[None, :] - cols[:, None] + (WIN_C - 1), 0, 2 * WIN_C - 2)
    ok = row_ok[:, :, None, :, None] & col_ok[None, None, :, None, :]
    bias_rows = rpb.astype(F32)[:, row_off]
    pick_col = (col_off[:, :, None] == np.arange(2 * WIN_C - 1)).astype(np.float32)
    vals = jnp.einsum('hpakb,cjb->hpackj', bias_rows, pick_col, precision=lax.Precision.HIGHEST)
    table = jnp.where(ok[None], vals * LOG2_E, NEG)
    return table.reshape(rpb.shape[0], 3, TM, NA_WIN)


def _diff_kernel(lam_ref, gain_ref, *refs, tk, lambda_init):
    q_refs = refs[:DIFF_Q_TILES]
    k_ref, v_ref, o_ref, q2_ref, m_ref, acc_ref = refs[DIFF_Q_TILES:DIFF_Q_TILES + 6]
    pipe_refs = refs[DIFF_Q_TILES + 6:]
    n_heads = o_ref.shape[2] // LANE
    _init_flash(q_refs, q2_ref, m_ref, acc_ref)

    def chunk_of(ref, width):
        def chunk(i, n):
            h = n % n_heads
            return ref[0, pl.ds(pl.multiple_of(i * tk, tk), tk), width * h:width * (h + 1)]
        return chunk
    _flash_pipeline(q2_ref, chunk_of(k_ref, LANE), chunk_of(v_ref, 2 * LANE), k_ref.shape[1] // tk,
                    DIFF_Q_TILES * n_heads, pipe_refs[6], acc_ref, pipe_refs[0:2], pipe_refs[2:4], pipe_refs[4:6])

    lp = lam_ref[...]
    lam = (jnp.exp(jnp.sum(lp[0:1] * lp[1:2], axis=-1, keepdims=True))
           - jnp.exp(jnp.sum(lp[2:3] * lp[3:4], axis=-1, keepdims=True)) + lambda_init)

    def sub_layer_norm(o):
        d = o[0:TM] - lam * o[TM:2 * TM]
        y = d * lax.rsqrt(jnp.mean(d * d, axis=-1, keepdims=True) + EPS) * gain_ref[...]
        return y * (1.0 - lambda_init)
    _store_heads(o_ref, acc_ref, DIFF_Q_TILES, sub_layer_norm)


def _diff(lam_params, subln_gain, dq, dk, dv, n_ctx, tk, lambda_init):
    b, t_len, qw = dq.shape
    assert n_ctx == TM and (t_len - n_ctx) % (DIFF_Q_TILES * TM) == 0
    hw = LANE * DIFF_HEADS_PER_STEP
    return pl.pallas_call(
        functools.partial(_diff_kernel, tk=tk, lambda_init=lambda_init),
        grid=(b, qw // hw, (t_len - n_ctx) // (DIFF_Q_TILES * TM)),
        in_specs=([pl.BlockSpec(lam_params.shape, lambda b, h, s: (0, 0)),
                   pl.BlockSpec(subln_gain.shape, lambda b, h, s: (0, 0))]
                  + [pl.BlockSpec((1, TM, hw), lambda b, h, s, u=u: (b, 1 + DIFF_Q_TILES * s + u, h))
                     for u in range(DIFF_Q_TILES)]
                  + [pl.BlockSpec((1, t_len, hw), lambda b, h, s: (b, 0, h)),
                     pl.BlockSpec((1, t_len, 2 * hw), lambda b, h, s: (b, 0, h))]),
        out_specs=pl.BlockSpec((1, DIFF_Q_TILES * TM, hw), lambda b, h, s: (b, s, h)),
        out_shape=jax.ShapeDtypeStruct((b, t_len - n_ctx, qw), BF16),
        scratch_shapes=_flash_scratch(DIFF_Q_TILES * DIFF_HEADS_PER_STEP) + _pipeline_scratch(tk) + [pltpu.VMEM((2 * TM * DIFF_Q_TILES * DIFF_HEADS_PER_STEP, LANE), F32)],
        compiler_params=_params(3),
        name="diff_attention",
    )(lam_params, subln_gain, *([dq] * DIFF_Q_TILES), dk, dv)


def _ffn_kernel(*refs, n_att, ff_chunks, final, split_tokens):
    if split_tokens:
        x_tile = _token_tile(refs[0], refs[1])
        refs = refs[1:]
    else:
        x_tile = refs[0][0]
    mod_ref = refs[1]
    att_refs = refs[2:2 + n_att]
    wo_refs = refs[2 + n_att:2 + 2 * n_att]
    wg_ref, wu_ref, wd_ref = refs[2 + 2 * n_att:5 + 2 * n_att]
    o_ref = refs[-1]
    mod = mod_ref[0, 0]
    y = _dot(att_refs[0][0], wo_refs[0][...])
    for a_ref, w_ref in zip(att_refs[1:], wo_refs[1:]):
        y = y + _dot(a_ref[0], w_ref[...])
    x1 = x_tile + mod[2:3] * y
    h = _modulated_norm(x1, mod[3:4], mod[4:5]).astype(BF16)
    d_ff = wg_ref.shape[1]
    bounds = [0]
    for i in range(ff_chunks):
        bounds.append(min(d_ff, -(-(d_ff * (i + 1) // ff_chunks) // MXU_DEPTH) * MXU_DEPTH))
    down = None
    for lo, hi in zip(bounds[:-1], bounds[1:]):
        g = _dot(h, wg_ref[:, lo:hi])
        u = _dot(h, wu_ref[:, lo:hi])
        a = (g * (1.0 / (1.0 + jnp.exp(-g))) * u).astype(BF16)
        part = _dot(a, wd_ref[lo:hi, :])
        down = part if down is None else down + part
    x2 = x1 + mod[5:6] * down
    if final:
        gain_ref = refs[5 + 2 * n_att]
        x2 = x2 * lax.rsqrt(jnp.mean(x2 * x2, axis=-1, keepdims=True) + EPS) * gain_ref[...]
    o_ref[0] = x2


def _ffn(tokens, mods, atts, wos, wg, wu, wd, final_gain=None):
    split_tokens = isinstance(tokens, tuple)
    n_rows = atts[0].shape[1]
    b, _, d = tokens[-1].shape if split_tokens else tokens.shape
    final = final_gain is not None
    if split_tokens:
        token_specs = _split_token_specs(d)
        tokens = list(tokens)
    else:
        token_specs = [_token_spec(d, (tokens.shape[1] - n_rows) // TM)]
        tokens = [tokens]
    in_specs = (token_specs + [_mod_spec(d, latent_only=not split_tokens)]
                + [_token_spec(a.shape[2]) for a in atts]
                + [_const_spec(w.shape) for w in wos]
                + [_const_spec(wg.shape), _const_spec(wu.shape), _const_spec(wd.shape)])
    args = [*tokens, mods, *atts, *wos, wg, wu, wd]
    if final:
        in_specs.append(_const_spec(final_gain.shape))
        args.append(final_gain)
    return pl.pallas_call(
        functools.partial(_ffn_kernel, n_att=len(atts), ff_chunks=2, final=final, split_tokens=split_tokens),
        grid=(b, n_rows // TM),
        in_specs=in_specs,
        out_specs=_token_spec(d),
        out_shape=jax.ShapeDtypeStruct((b, n_rows, d), F32),
        compiler_params=_params(2),
        name="outproj_ffn_final" if final else "outproj_ffn",
    )(*args)


def _head_cols(base, heads):
    return np.concatenate([base + HEAD_DIM * h + _DEINT for h in heads])


def _rope_tables(seq, n_ctx):
    t = jnp.arange(seq)
    row = (t // GRID_W).astype(F32)
    col = (t % GRID_W).astype(F32)
    n_freq = HEAD_DIM // 4
    inv = ROPE_THETA ** (-jnp.arange(n_freq, dtype=F32) / n_freq)
    ang = jnp.concatenate([row[:, None] * inv, col[:, None] * inv], axis=-1)
    cos, sin = jnp.cos(ang), jnp.sin(ang)
    zero = jnp.zeros_like(sin)

    def table(first_half, second_half, ctx_value):
        lat = jnp.tile(jnp.concatenate([first_half, second_half], axis=-1), (1, LANE // HEAD_DIM))
        return jnp.concatenate([jnp.full((n_ctx, LANE), ctx_value, F32), lat], axis=0)

    return table(cos, cos, 1.0), table(zero, sin, 0.0), table(-sin, zero, 0.0)


def kernel(x, c, ctx, c_ctx, ada_w, ada_b, ffn_w_gate, ffn_w_up, ffn_w_down, par_w_in, par_w_out, na_rpb,
           gqa_q_gain, gqa_k_gain, diff_w_in, diff_w_out, diff_lambda_q1, diff_lambda_k1, diff_lambda_q2,
           diff_lambda_k2, diff_subln_gain, final_norm_gain):
    b, seq, d = x.shape
    n_ctx = ctx.shape[1]
    assert n_ctx == TM and seq % TM == 0 and d % LANE == 0 and b < 8 and ada_w.shape[0] == 2
    assert par_w_in.shape[-1] == PAR_W and diff_w_in.shape[-1] == 3 * DIFF_W
    rows = seq // GRID_W
    scale = HEAD_DIM ** -0.5 * LOG2_E

    cond = jnp.zeros((8, d), F32).at[:b].set(c).at[b].set(c_ctx)
    mods_all = _ada(cond, ada_w, ada_b).reshape(2, 8, N_MOD, d)

    def mods_of(layer):
        m = mods_all[layer]
        return jnp.stack([jnp.broadcast_to(m[b], (b, N_MOD, d)), m[:b]], axis=1)

    rope = _rope_tables(seq, n_ctx)

    cols0 = np.concatenate([np.arange(NAQ0, GQ0), _head_cols(GQ0, _GQA_HEAD_ORDER), _head_cols(GK0, range(N_HEADS_GKV)),
                            np.arange(GV0, PAR_W)])
    col_scale0 = np.ones((PAR_W,), np.float32)
    col_scale0[NAQ0:NAK0] = scale
    w_in0 = (par_w_in[0][:, cols0] * col_scale0).astype(BF16)
    gains = jnp.zeros((8, LANE), F32)
    gains = gains.at[0].set(jnp.tile(gqa_q_gain[0][_DEINT] * scale, 2)).at[1].set(jnp.tile(gqa_k_gain[0][_DEINT], 2))
    block_mean = jnp.asarray(np.kron(np.eye(LANE // HEAD_DIM), np.full((HEAD_DIM, HEAD_DIM), 1.0 / HEAD_DIM)), BF16)
    naq, nak, nav, gq, gk, gv = _proj_par(ctx, x, mods_of(0), w_in0, rope, gains, block_mean)
    att_na = _na(naq, nak, nav, _na_bias_table(na_rpb[0], rows), n_ctx)
    att_g = _gqa(gq, gk, gv, n_ctx, tk=_key_chunk(n_ctx + seq))
    wo_na = par_w_out[0][0:NA_W].astype(BF16)
    wo_g = par_w_out[0][NA_W + np.concatenate([HEAD_DIM * h + np.arange(HEAD_DIM) for h in _GQA_HEAD_ORDER])].astype(BF16)
    xa = _ffn((ctx, x), mods_of(0), [att_na, att_g], [wo_na, wo_g],
              ffn_w_gate[0].astype(BF16), ffn_w_up[0].astype(BF16), ffn_w_down[0].astype(BF16))

    sub_heads = range(2 * N_HEADS_DIFF)
    cols1 = np.concatenate([_head_cols(0, sub_heads), _head_cols(DIFF_W, sub_heads), np.arange(2 * DIFF_W, 3 * DIFF_W)])
    col_scale1 = np.ones((3 * DIFF_W,), np.float32)
    col_scale1[0:DIFF_W] = scale
    w_in1 = (diff_w_in[0][:, cols1] * col_scale1).astype(BF16)
    dq, dk, dv = _proj_diff(xa, mods_of(1), w_in1, rope)
    lambda_init = 0.8 - 0.6 * float(np.exp(-0.3 * 1))
    lam_params = jnp.stack([diff_lambda_q1[0], diff_lambda_k1[0], diff_lambda_q2[0], diff_lambda_k2[0]]).astype(F32)
    att_d = _diff(lam_params, diff_subln_gain[0].reshape(1, -1).astype(F32), dq, dk, dv, n_ctx,
                  tk=_key_chunk(n_ctx + seq),
                  lambda_init=lambda_init)
    return _ffn(xa, mods_of(1), [att_d], [diff_w_out[0].astype(BF16)],
                ffn_w_gate[1].astype(BF16), ffn_w_up[1].astype(BF16), ffn_w_down[1].astype(BF16),
                final_gain=final_norm_gain.reshape(1, -1).astype(F32))
```

```python
import functools

import numpy as np
import jax
import jax.numpy as jnp
from jax import lax
from jax.experimental import pallas as pl
from jax.experimental.pallas import tpu as pltpu

F32 = jnp.float32
BF16 = jnp.bfloat16

GRID_W = 64
HEAD_DIM = 64
WIN_R = 8
WIN_C = 16
N_MOD = 6
ROPE_THETA = 10000.0
EPS = 1e-6

LANE = 128
TM = 256
NA_ROWS_PER_TILE = TM // GRID_W
NA_KEY_ROWS = 12
NA_WIN = NA_KEY_ROWS * GRID_W
ROW_BLOCK = 128
DIFF_HEADS_PER_STEP = 2
GQA_Q_TILES = 2
DIFF_Q_TILES = 4
SOFTMAX_ROWS = 32
LOG2_E = 1.4426950408889634
MXU_DEPTH = 256
NEG = -1e30
VMEM_LIMIT = 56 * 1024 * 1024

N_HEADS_NA = 8
N_HEADS_GQ = 8
N_HEADS_GKV = 2
N_HEADS_DIFF = 8
NA_W = N_HEADS_NA * HEAD_DIM
GQ_W = N_HEADS_GQ * HEAD_DIM
GKV_W = N_HEADS_GKV * HEAD_DIM
NAQ0, NAK0, NAV0, GQ0, GK0, GV0, PAR_W = (int(v) for v in np.cumsum([0, NA_W, NA_W, NA_W, GQ_W, GKV_W, GKV_W]))
DIFF_W = N_HEADS_DIFF * 2 * HEAD_DIM
ADA_TILE = 1024

_DEINT = np.concatenate([np.arange(0, HEAD_DIM, 2), np.arange(1, HEAD_DIM, 2)])
_GQA_HEAD_ORDER = (0, 4, 1, 5, 2, 6, 3, 7)


def _dot(a, b):
    return jnp.dot(a, b, preferred_element_type=F32)


def _dot_nt(a, b):
    return lax.dot_general(a, b, (((1,), (1,)), ((), ())), preferred_element_type=F32)


def _params(n_grid):
    return pltpu.CompilerParams(dimension_semantics=("arbitrary",) * n_grid, vmem_limit_bytes=VMEM_LIMIT)


def _const_spec(shape):
    return pl.BlockSpec(shape, lambda *_: (0,) * len(shape), pipeline_mode=pl.Buffered(1))


def _split_bf16(a):
    hi = a.astype(BF16)
    return hi, (a - hi.astype(F32)).astype(BF16)


def _ada_kernel(cond_ref, w_ref, b_ref, o_ref):
    c = cond_ref[...]
    a_hi, a_lo = _split_bf16(c * (1.0 / (1.0 + jnp.exp(-c))))
    w_hi, w_lo = _split_bf16(w_ref[0])
    o_ref[0] = _dot(a_hi, w_hi) + _dot(a_lo, w_hi) + _dot(a_hi, w_lo) + b_ref[0]


def _ada(cond, ada_w, ada_b):
    depth, d, n = ada_w.shape
    tn = ADA_TILE
    return pl.pallas_call(
        _ada_kernel,
        grid=(depth, n // tn),
        in_specs=[pl.BlockSpec(cond.shape, lambda l, j: (0, 0)),
                  pl.BlockSpec((1, d, tn), lambda l, j: (l, 0, j)),
                  pl.BlockSpec((1, 1, tn), lambda l, j: (l, 0, j))],
        out_specs=pl.BlockSpec((1, cond.shape[0], tn), lambda l, j: (l, 0, j)),
        out_shape=jax.ShapeDtypeStruct((depth, cond.shape[0], n), F32),
        compiler_params=_params(2),
        name="ada_modulation",
    )(cond, ada_w, ada_b.reshape(depth, 1, n))


def _modulated_norm(x, shift, scale):
    ms = jnp.mean(x * x, axis=-1, keepdims=True)
    return (x * lax.rsqrt(ms + EPS)) * (1.0 + scale) + shift


def _group_mean_sq(x, bd):
    hi, lo = _split_bf16(x * x)
    return _dot(hi, bd) + _dot(lo, bd)


def _rope(x, c, sa, sb):
    return x * c + pltpu.roll(x, 32, 1) * sa + pltpu.roll(x, 96, 1) * sb


def _ones_column(rows):
    return (lax.broadcasted_iota(jnp.int32, (rows, LANE), 1) == 0).astype(BF16)


def _store_values(v_ref, p, col0, n_blocks):
    ones = _ones_column(p.shape[0])
    for j in range(n_blocks):
        v_ref[0, :, 2 * LANE * j:2 * LANE * j + LANE] = p[:, col0 + LANE * j:col0 + LANE * (j + 1)].astype(BF16)
        v_ref[0, :, 2 * LANE * j + LANE:2 * LANE * (j + 1)] = ones


def _token_tile(ctx_ref, x_ref):
    return jnp.where(pl.program_id(1) == 0, ctx_ref[0], x_ref[0])


def _proj_par_kernel(ctx_ref, x_ref, mod_ref, w_ref, c_ref, sa_ref, sb_ref, gain_ref, bd_ref,
                     naq_ref, nak_ref, nav_ref, gq_ref, gk_ref, gv_ref):
    mod = mod_ref[0, 0]
    h = _modulated_norm(_token_tile(ctx_ref, x_ref), mod[0:1], mod[1:2]).astype(BF16)
    p = _dot(h, w_ref[...])
    naq_ref[0] = p[:, NAQ0:NAK0].astype(BF16)
    nak_ref[0] = p[:, NAK0:NAV0].astype(BF16)
    _store_values(nav_ref, p, NAV0, NA_W // LANE)
    c, sa, sb, bd = c_ref[...], sa_ref[...], sb_ref[...], bd_ref[...]

    def qk_norm_rope(g, gain):
        g = g * lax.rsqrt(_group_mean_sq(g, bd) + EPS) * gain
        return _rope(g, c, sa, sb).astype(BF16)

    for j in range(GQ_W // LANE):
        gq_ref[0, :, LANE * j:LANE * (j + 1)] = qk_norm_rope(p[:, GQ0 + LANE * j:GQ0 + LANE * (j + 1)], gain_ref[0:1, :])
    gk_ref[0] = qk_norm_rope(p[:, GK0:GV0], gain_ref[1:2, :])
    _store_values(gv_ref, p, GV0, GKV_W // LANE)


def _proj_diff_kernel(x_ref, mod_ref, w_ref, c_ref, sa_ref, sb_ref, dq_ref, dk_ref, dv_ref):
    mod = mod_ref[0, 0]
    h = _modulated_norm(x_ref[0], mod[0:1], mod[1:2]).astype(BF16)
    p = _dot(h, w_ref[...])
    c, sa, sb = c_ref[...], sa_ref[...], sb_ref[...]
    for j in range(DIFF_W // LANE):
        dq_ref[0, :, LANE * j:LANE * (j + 1)] = _rope(p[:, LANE * j:LANE * (j + 1)], c, sa, sb).astype(BF16)
        dk_ref[0, :, LANE * j:LANE * (j + 1)] = _rope(p[:, DIFF_W + LANE * j:DIFF_W + LANE * (j + 1)], c, sa, sb).astype(BF16)
    _store_values(dv_ref, p, 2 * DIFF_W, DIFF_W // LANE)


def _token_spec(width, tile_off=0):
    return pl.BlockSpec((1, TM, width), lambda b, t: (b, t + tile_off, 0))


def _split_token_specs(d):
    return [pl.BlockSpec((1, TM, d), lambda b, t: (b, 0, 0)),
            pl.BlockSpec((1, TM, d), lambda b, t: (b, jnp.maximum(t - 1, 0), 0))]


def _mod_spec(d, latent_only=False):
    if latent_only:
        return pl.BlockSpec((1, 1, N_MOD, d), lambda b, t: (b, 1, 0, 0))
    return pl.BlockSpec((1, 1, N_MOD, d), lambda b, t: (b, jnp.minimum(t, 1), 0, 0))


def _rope_spec():
    return pl.BlockSpec((TM, LANE), lambda b, t: (t, 0))


def _proj_par(ctx, x, mods, w, rope, gains, bd):
    b, seq, d = x.shape
    t_len = ctx.shape[1] + seq
    widths = (NA_W, NA_W, 2 * NA_W, GQ_W, GKV_W, 2 * GKV_W)
    return pl.pallas_call(
        _proj_par_kernel,
        grid=(b, t_len // TM),
        in_specs=_split_token_specs(d) + [_mod_spec(d), _const_spec(w.shape), _rope_spec(), _rope_spec(), _rope_spec(),
                                          _const_spec(gains.shape), _const_spec(bd.shape)],
        out_specs=[_token_spec(n) for n in widths],
        out_shape=[jax.ShapeDtypeStruct((b, t_len, n), BF16) for n in widths],
        compiler_params=_params(2),
        name="proj_parallel_mixer",
    )(ctx, x, mods, w, *rope, gains, bd)


def _proj_diff(xa, mods, w, rope):
    b, t_len, d = xa.shape
    widths = (DIFF_W, DIFF_W, 2 * DIFF_W)
    return pl.pallas_call(
        _proj_diff_kernel,
        grid=(b, t_len // TM),
        in_specs=[_token_spec(d), _mod_spec(d), _const_spec(w.shape), _rope_spec(), _rope_spec(), _rope_spec()],
        out_specs=[_token_spec(n) for n in widths],
        out_shape=[jax.ShapeDtypeStruct((b, t_len, n), BF16) for n in widths],
        compiler_params=_params(2),
        name="proj_diff_mixer",
    )(xa, mods, w, *rope)


def _split_heads(q):
    lane = lax.broadcasted_iota(jnp.int32, q.shape, 1)
    zero = jnp.zeros_like(q)
    return jnp.concatenate([jnp.where(lane < HEAD_DIM, q, zero), jnp.where(lane >= HEAD_DIM, q, zero)], axis=0)


def _softmax_update(s, m_prev):
    m_new = jnp.maximum(m_prev, jnp.max(s, axis=-1, keepdims=True))
    p = jnp.exp2(s - m_new)
    return p, jnp.exp2(m_prev - m_new), m_new


def _flash_chunk(q2_ref, k, v, m_ref, acc_ref, row0):
    for r in range(2 * TM // ROW_BLOCK):
        rows = slice(row0 + ROW_BLOCK * r, row0 + ROW_BLOCK * (r + 1))
        s = _dot_nt(q2_ref[rows], k)
        p, alpha, m_new = _softmax_update(s, m_ref[rows])
        acc_ref[rows] = alpha * acc_ref[rows] + _dot(p.astype(BF16), v)
        m_ref[rows] = m_new


def _flash_pipeline(q2_ref, k_of, v_of, n_chunks, n_blocks, m_ref, acc_ref, s_refs, p_refs):
    assert n_blocks % 2 == 0
    r2 = 2 * TM

    def score(i, j, slot):
        s_refs[slot][...] = _dot_nt(q2_ref[r2 * j:r2 * (j + 1)], k_of(i, j))

    def softmax(j, slot):
        for r in range(r2 // SOFTMAX_ROWS):
            rows = slice(SOFTMAX_ROWS * r, SOFTMAX_ROWS * (r + 1))
            state_rows = slice(r2 * j + SOFTMAX_ROWS * r, r2 * j + SOFTMAX_ROWS * (r + 1))
            p, alpha, m_new = _softmax_update(s_refs[slot][rows], m_ref[state_rows])
            p_refs[slot][rows] = p.astype(BF16)
            acc_ref[state_rows] = acc_ref[state_rows] * jnp.broadcast_to(alpha, (SOFTMAX_ROWS, 2 * LANE))
            m_ref[state_rows] = m_new

    def accumulate(i, j, slot):
        blk = slice(r2 * j, r2 * (j + 1))
        acc_ref[blk] += _dot(p_refs[slot][...], v_of(i, j))

    last_slot = (n_blocks - 1) % 2
    p_refs[last_slot][...] = jnp.zeros(p_refs[last_slot].shape, BF16)
    score(0, 0, 0)

    def body(i, carry):
        for j in range(n_blocks):
            if j + 1 < n_blocks:
                score(i, j + 1, (j + 1) % 2)
            else:
                score(jnp.minimum(i + 1, n_chunks - 1), 0, 0)
            softmax(j, j % 2)
            if j > 0:
                accumulate(i, j - 1, (j - 1) % 2)
            else:
                accumulate(jnp.maximum(i - 1, 0), n_blocks - 1, last_slot)
        return carry

    lax.fori_loop(0, n_chunks, body, 0)
    accumulate(n_chunks - 1, n_blocks - 1, last_slot)


def _pipeline_scratch(tk):
    rows = 2 * TM
    return [pltpu.VMEM((rows, tk), F32)] * 2 + [pltpu.VMEM((rows, tk), BF16)] * 2


def _normalize(acc):
    return acc[:, 0:LANE] / acc[:, LANE:LANE + 1]


def _merge_heads(o):
    lane = lax.broadcasted_iota(jnp.int32, (TM, LANE), 1)
    return jnp.where(lane < HEAD_DIM, o[0:TM], o[TM:2 * TM])


def _init_flash(q_refs, q2_ref, m_ref, acc_ref):
    n = 0
    for q_ref in q_refs:
        for j in range(q_ref.shape[2] // LANE):
            q2_ref[2 * TM * n:2 * TM * (n + 1)] = _split_heads(q_ref[0, :, LANE * j:LANE * (j + 1)])
            n += 1
    m_ref[...] = jnp.full(m_ref.shape, NEG, F32)
    acc_ref[...] = jnp.zeros(acc_ref.shape, F32)


def _flash_scratch(n_blocks):
    rows = 2 * TM * n_blocks
    return [pltpu.VMEM((rows, LANE), BF16), pltpu.VMEM((rows, 1), F32), pltpu.VMEM((rows, 2 * LANE), F32)]


def _key_chunk(t_len):
    return next(tk for tk in (768, 640, 512, 384, 256) if t_len % tk == 0)


def _store_heads(o_ref, acc_ref, n_tiles, finish):
    n_lane_blocks = o_ref.shape[2] // LANE
    for u in range(n_tiles):
        for j in range(n_lane_blocks):
            n = u * n_lane_blocks + j
            o = _normalize(acc_ref[2 * TM * n:2 * TM * (n + 1)])
            o_ref[0, TM * u:TM * (u + 1), LANE * j:LANE * (j + 1)] = finish(o).astype(BF16)


def _gqa_ctx_kernel(q_ref, k_ref, v_ref, o_ref, q2_ref, m_ref, acc_ref):
    _init_flash([q_ref], q2_ref, m_ref, acc_ref)
    for j in range(q_ref.shape[2] // LANE):
        _flash_chunk(q2_ref, k_ref[0], v_ref[0], m_ref, acc_ref, 2 * TM * j)
    _store_heads(o_ref, acc_ref, 1, _merge_heads)


def _gqa_kernel(*refs, tk):
    q_refs = refs[:GQA_Q_TILES]
    k_ref, v_ref, o_ref, q2_ref, m_ref, acc_ref = refs[GQA_Q_TILES:GQA_Q_TILES + 6]
    pipe_refs = refs[GQA_Q_TILES + 6:]
    _init_flash(q_refs, q2_ref, m_ref, acc_ref)

    def chunk_of(ref):
        return lambda i, n: ref[0, pl.ds(pl.multiple_of(i * tk, tk), tk), :]
    _flash_pipeline(q2_ref, chunk_of(k_ref), chunk_of(v_ref), k_ref.shape[1] // tk,
                    GQA_Q_TILES * (o_ref.shape[2] // LANE), m_ref, acc_ref,
                    pipe_refs[0:2], pipe_refs[2:4])
    _store_heads(o_ref, acc_ref, GQA_Q_TILES, _merge_heads)


def _gqa(gq, gk, gv, n_ctx, tk):
    b, t_len, qw = gq.shape
    assert n_ctx == TM and (t_len - n_ctx) % (GQA_Q_TILES * TM) == 0
    n_lane_blocks = qw // LANE
    ctx_out = pl.pallas_call(
        _gqa_ctx_kernel,
        grid=(b,),
        in_specs=[pl.BlockSpec((1, TM, qw), lambda b: (b, 0, 0)),
                  pl.BlockSpec((1, TM, LANE), lambda b: (b, 0, 0)),
                  pl.BlockSpec((1, TM, 2 * LANE), lambda b: (b, 0, 0))],
        out_specs=pl.BlockSpec((1, TM, qw), lambda b: (b, 0, 0)),
        out_shape=jax.ShapeDtypeStruct((b, TM, qw), BF16),
        scratch_shapes=_flash_scratch(n_lane_blocks),
        compiler_params=_params(1),
        name="gqa_attention_ctx",
    )(gq, gk, gv)
    latent_out = pl.pallas_call(
        functools.partial(_gqa_kernel, tk=tk),
        grid=(b, (t_len - n_ctx) // (GQA_Q_TILES * TM)),
        in_specs=([pl.BlockSpec((1, TM, qw), lambda b, s, u=u: (b, 1 + GQA_Q_TILES * s + u, 0))
                   for u in range(GQA_Q_TILES)]
                  + [pl.BlockSpec((1, t_len, LANE), lambda b, s: (b, 0, 0)),
                     pl.BlockSpec((1, t_len, 2 * LANE), lambda b, s: (b, 0, 0))]),
        out_specs=pl.BlockSpec((1, GQA_Q_TILES * TM, qw), lambda b, s: (b, s, 0)),
        out_shape=jax.ShapeDtypeStruct((b, t_len - n_ctx, qw), BF16),
        scratch_shapes=_flash_scratch(GQA_Q_TILES * n_lane_blocks) + _pipeline_scratch(tk),
        compiler_params=_params(2),
        name="gqa_attention",
    )(*([gq] * GQA_Q_TILES), gk, gv)
    return jnp.concatenate([ctx_out, latent_out], axis=1)


def _na_kernel(q_ref, k_ref, v_ref, bias_ref, o_ref, q2_ref, m_ref, acc_ref, *pipe_refs, n_ctx, rows):
    t = pl.program_id(1)
    n_blocks = q_ref.shape[2] // LANE
    r2 = 2 * TM
    s_refs, p_refs = pipe_refs[0:2], pipe_refs[2:4]
    _init_flash([q_ref], q2_ref, m_ref, acc_ref)

    @pl.when(t == 0)
    def _():
        for j in range(n_blocks):
            _flash_chunk(q2_ref, k_ref[0, 0:n_ctx, LANE * j:LANE * (j + 1)],
                         v_ref[0, 0:n_ctx, 2 * LANE * j:2 * LANE * (j + 1)], m_ref, acc_ref, r2 * j)
            o_ref[0, :, LANE * j:LANE * (j + 1)] = _merge_heads(_normalize(acc_ref[r2 * j:r2 * (j + 1)])).astype(BF16)

    @pl.when(t > 0)
    def _():
        first_row = NA_ROWS_PER_TILE * (t - 1)
        start = jnp.clip(first_row - WIN_R // 2, 0, rows - NA_KEY_ROWS)
        off = pl.multiple_of(n_ctx + GRID_W * start, GRID_W)

        def score(j, slot):
            q2 = q2_ref[r2 * j:r2 * (j + 1)]
            s_refs[slot][:, 0:n_ctx] = _dot_nt(q2, k_ref[0, 0:n_ctx, LANE * j:LANE * (j + 1)])
            s_refs[slot][:, n_ctx:] = _dot_nt(q2, k_ref[0, pl.ds(off, NA_WIN), LANE * j:LANE * (j + 1)])

        def softmax(j, slot):
            for r in range(r2 // SOFTMAX_ROWS):
                rows_r = slice(SOFTMAX_ROWS * r, SOFTMAX_ROWS * (r + 1))
                head, row = divmod(SOFTMAX_ROWS * r, TM)
                s_ctx = s_refs[slot][rows_r, 0:n_ctx]
                s_win = s_refs[slot][rows_r, n_ctx:] + bias_ref[2 * j + head, 0, row:row + SOFTMAX_ROWS, :]
                m = jnp.maximum(jnp.max(s_ctx, axis=-1, keepdims=True), jnp.max(s_win, axis=-1, keepdims=True))
                p_refs[slot][rows_r, 0:n_ctx] = jnp.exp2(s_ctx - m).astype(BF16)
                p_refs[slot][rows_r, n_ctx:] = jnp.exp2(s_win - m).astype(BF16)

        def output(j, slot):
            acc = (_dot(p_refs[slot][:, 0:n_ctx], v_ref[0, 0:n_ctx, 2 * LANE * j:2 * LANE * (j + 1)])
                   + _dot(p_refs[slot][:, n_ctx:], v_ref[0, pl.ds(off, NA_WIN), 2 * LANE * j:2 * LANE * (j + 1)]))
            o_ref[0, :, LANE * j:LANE * (j + 1)] = _merge_heads(_normalize(acc)).astype(BF16)

        score(0, 0)
        for j in range(n_blocks):
            if j + 1 < n_blocks:
                score(j + 1, (j + 1) % 2)
            softmax(j, j % 2)
            if j > 0:
                output(j - 1, (j - 1) % 2)
        output(n_blocks - 1, (n_blocks - 1) % 2)


def _na(naq, nak, nav, bias, n_ctx):
    b, t_len, qw = naq.shape
    n_tiles = t_len // TM
    rows = (t_len - n_ctx) // GRID_W
    n_keys = n_ctx + NA_WIN

    def bias_index(b, t):
        return (0, jnp.where(t <= 1, 0, jnp.where(t == n_tiles - 1, 2, 1)), 0, 0)

    def resident(width):
        return pl.BlockSpec((1, t_len, width), lambda b, t: (b, 0, 0), pipeline_mode=pl.Buffered(1))

    return pl.pallas_call(
        functools.partial(_na_kernel, n_ctx=n_ctx, rows=rows),
        grid=(b, n_tiles),
        in_specs=[pl.BlockSpec((1, TM, qw), lambda b, t: (b, t, 0)), resident(qw), resident(2 * qw),
                  pl.BlockSpec((bias.shape[0], 1, TM, NA_WIN), bias_index, pipeline_mode=pl.Buffered(1))],
        out_specs=pl.BlockSpec((1, TM, qw), lambda b, t: (b, t, 0)),
        out_shape=jax.ShapeDtypeStruct((b, t_len, qw), BF16),
        scratch_shapes=(_flash_scratch(qw // LANE)
                        + [pltpu.VMEM((2 * TM, n_keys), F32)] * 2 + [pltpu.VMEM((2 * TM, n_keys), BF16)] * 2),
        compiler_params=_params(2),
        name="neighbourhood_attention",
    )(naq, nak, nav, bias)


def _na_bias_table(rpb, rows):
    g_of_pattern = np.array([0, 2, rows // NA_ROWS_PER_TILE - 1])
    a = np.arange(NA_ROWS_PER_TILE)
    r = NA_ROWS_PER_TILE * g_of_pattern[:, None] + a[None, :]
    start = np.clip(NA_ROWS_PER_TILE * g_of_pattern - WIN_R // 2, 0, rows - NA_KEY_ROWS)
    rs = np.clip(r - WIN_R // 2, 0, rows - WIN_R)
    key_row = start[:, None] + np.arange(NA_KEY_ROWS)[None, :]
    row_ok = (key_row[:, None, :] >= rs[:, :, None]) & (key_row[:, None, :] < rs[:, :, None] + WIN_R)
    row_off = np.clip(key_row[:, None, :] - r[:, :, None] + (WIN_R - 1), 0, 2 * WIN_R - 2)
    cols = np.arange(GRID_W)
    col_start = np.clip(cols - WIN_C // 2, 0, GRID_W - WIN_C)
    col_ok = (cols[None, :] >= col_start[:, None]) & (cols[None, :] < col_start[:, None] + WIN_C)
    col_off = np.clip(cols[None, :] - cols[:, None] + (WIN_C - 1), 0, 2 * WIN_C - 2)
    ok = row_ok[:, :, None, :, None] & col_ok[None, None, :, None, :]
    bias_rows = rpb.astype(F32)[:, row_off]
    pick_col = (col_off[:, :, None] == np.arange(2 * WIN_C - 1)).astype(np.float32)
    vals = jnp.einsum('hpakb,cjb->hpackj', bias_rows, pick_col, precision=lax.Precision.HIGHEST)
    table = jnp.where(ok[None], vals * LOG2_E, NEG)
    return table.reshape(rpb.shape[0], 3, TM, NA_WIN)


def _diff_kernel(lam_ref, gain_ref, *refs, tk, lambda_init):
    q_refs = refs[:DIFF_Q_TILES]
    k_ref, v_ref, o_ref, q2_ref, m_ref, acc_ref = refs[DIFF_Q_TILES:DIFF_Q_TILES + 6]
    pipe_refs = refs[DIFF_Q_TILES + 6:]
    n_heads = o_ref.shape[2] // LANE
    _init_flash(q_refs, q2_ref, m_ref, acc_ref)

    def chunk_of(ref, width):
        def chunk(i, n):
            h = n % n_heads
            return ref[0, pl.ds(pl.multiple_of(i * tk, tk), tk), width * h:width * (h + 1)]
        return chunk
    _flash_pipeline(q2_ref, chunk_of(k_ref, LANE), chunk_of(v_ref, 2 * LANE), k_ref.shape[1] // tk,
                    DIFF_Q_TILES * n_heads, m_ref, acc_ref, pipe_refs[0:2], pipe_refs[2:4])

    lp = lam_ref[...]
    lam = (jnp.exp(jnp.sum(lp[0:1] * lp[1:2], axis=-1, keepdims=True))
           - jnp.exp(jnp.sum(lp[2:3] * lp[3:4], axis=-1, keepdims=True)) + lambda_init)

    def sub_layer_norm(o):
        d = o[0:TM] - lam * o[TM:2 * TM]
        y = d * lax.rsqrt(jnp.mean(d * d, axis=-1, keepdims=True) + EPS) * gain_ref[...]
        return y * (1.0 - lambda_init)
    _store_heads(o_ref, acc_ref, DIFF_Q_TILES, sub_layer_norm)


def _diff(lam_params, subln_gain, dq, dk, dv, n_ctx, tk, lambda_init):
    b, t_len, qw = dq.shape
    assert n_ctx == TM and (t_len - n_ctx) % (DIFF_Q_TILES * TM) == 0
    hw = LANE * DIFF_HEADS_PER_STEP
    return pl.pallas_call(
        functools.partial(_diff_kernel, tk=tk, lambda_init=lambda_init),
        grid=(b, qw // hw, (t_len - n_ctx) // (DIFF_Q_TILES * TM)),
        in_specs=([pl.BlockSpec(lam_params.shape, lambda b, h, s: (0, 0)),
                   pl.BlockSpec(subln_gain.shape, lambda b, h, s: (0, 0))]
                  + [pl.BlockSpec((1, TM, hw), lambda b, h, s, u=u: (b, 1 + DIFF_Q_TILES * s + u, h))
                     for u in range(DIFF_Q_TILES)]
                  + [pl.BlockSpec((1, t_len, hw), lambda b, h, s: (b, 0, h)),
                     pl.BlockSpec((1, t_len, 2 * hw), lambda b, h, s: (b, 0, h))]),
        out_specs=pl.BlockSpec((1, DIFF_Q_TILES * TM, hw), lambda b, h, s: (b, s, h)),
        out_shape=jax.ShapeDtypeStruct((b, t_len - n_ctx, qw), BF16),
        scratch_shapes=_flash_scratch(DIFF_Q_TILES * DIFF_HEADS_PER_STEP) + _pipeline_scratch(tk),
        compiler_params=_params(3),
        name="diff_attention",
    )(lam_params, subln_gain, *([dq] * DIFF_Q_TILES), dk, dv)


def _ffn_kernel(*refs, n_att, ff_chunks, final, split_tokens):
    if split_tokens:
        x_tile = _token_tile(refs[0], refs[1])
        refs = refs[1:]
    else:
        x_tile = refs[0][0]
    mod_ref = refs[1]
    att_refs = refs[2:2 + n_att]
    wo_refs = refs[2 + n_att:2 + 2 * n_att]
    wg_ref, wu_ref, wd_ref = refs[2 + 2 * n_att:5 + 2 * n_att]
    o_ref = refs[-1]
    mod = mod_ref[0, 0]
    y = _dot(att_refs[0][0], wo_refs[0][...])
    for a_ref, w_ref in zip(att_refs[1:], wo_refs[1:]):
        y = y + _dot(a_ref[0], w_ref[...])
    x1 = x_tile + mod[2:3] * y
    h = _modulated_norm(x1, mod[3:4], mod[4:5]).astype(BF16)
    d_ff = wg_ref.shape[1]
    bounds = [0]
    for i in range(ff_chunks):
        bounds.append(min(d_ff, -(-(d_ff * (i + 1) // ff_chunks) // MXU_DEPTH) * MXU_DEPTH))
    down = None
    for lo, hi in zip(bounds[:-1], bounds[1:]):
        g = _dot(h, wg_ref[:, lo:hi])
        u = _dot(h, wu_ref[:, lo:hi])
        a = (g * (1.0 / (1.0 + jnp.exp(-g))) * u).astype(BF16)
        part = _dot(a, wd_ref[lo:hi, :])
        down = part if down is None else down + part
    x2 = x1 + mod[5:6] * down
    if final:
        gain_ref = refs[5 + 2 * n_att]
        x2 = x2 * lax.rsqrt(jnp.mean(x2 * x2, axis=-1, keepdims=True) + EPS) * gain_ref[...]
    o_ref[0] = x2


def _ffn(tokens, mods, atts, wos, wg, wu, wd, final_gain=None):
    split_tokens = isinstance(tokens, tuple)
    n_rows = atts[0].shape[1]
    b, _, d = tokens[-1].shape if split_tokens else tokens.shape
    final = final_gain is not None
    if split_tokens:
        token_specs = _split_token_specs(d)
        tokens = list(tokens)
    else:
        token_specs = [_token_spec(d, (tokens.shape[1] - n_rows) // TM)]
        tokens = [tokens]
    in_specs = (token_specs + [_mod_spec(d, latent_only=not split_tokens)]
                + [_token_spec(a.shape[2]) for a in atts]
                + [_const_spec(w.shape) for w in wos]
                + [_const_spec(wg.shape), _const_spec(wu.shape), _const_spec(wd.shape)])
    args = [*tokens, mods, *atts, *wos, wg, wu, wd]
    if final:
        in_specs.append(_const_spec(final_gain.shape))
        args.append(final_gain)
    return pl.pallas_call(
        functools.partial(_ffn_kernel, n_att=len(atts), ff_chunks=2, final=final, split_tokens=split_tokens),
        grid=(b, n_rows // TM),
        in_specs=in_specs,
        out_specs=_token_spec(d),
        out_shape=jax.ShapeDtypeStruct((b, n_rows, d), F32),
        compiler_params=_params(2),
        name="outproj_ffn_final" if final else "outproj_ffn",
    )(*args)


def _head_cols(base, heads):
    return np.concatenate([base + HEAD_DIM * h + _DEINT for h in heads])


def _rope_tables(seq, n_ctx):
    t = jnp.arange(seq)
    row = (t // GRID_W).astype(F32)
    col = (t % GRID_W).astype(F32)
    n_freq = HEAD_DIM // 4
    inv = ROPE_THETA ** (-jnp.arange(n_freq, dtype=F32) / n_freq)
    ang = jnp.concatenate([row[:, None] * inv, col[:, None] * inv], axis=-1)
    cos, sin = jnp.cos(ang), jnp.sin(ang)
    zero = jnp.zeros_like(sin)

    def table(first_half, second_half, ctx_value):
        lat = jnp.tile(jnp.concatenate([first_half, second_half], axis=-1), (1, LANE // HEAD_DIM))
        return jnp.concatenate([jnp.full((n_ctx, LANE), ctx_value, F32), lat], axis=0)

    return table(cos, cos, 1.0), table(zero, sin, 0.0), table(-sin, zero, 0.0)


def kernel(x, c, ctx, c_ctx, ada_w, ada_b, ffn_w_gate, ffn_w_up, ffn_w_down, par_w_in, par_w_out, na_rpb,
           gqa_q_gain, gqa_k_gain, diff_w_in, diff_w_out, diff_lambda_q1, diff_lambda_k1, diff_lambda_q2,
           diff_lambda_k2, diff_subln_gain, final_norm_gain):
    b, seq, d = x.shape
    n_ctx = ctx.shape[1]
    assert n_ctx == TM and seq % TM == 0 and d % LANE == 0 and b < 8 and ada_w.shape[0] == 2
    assert par_w_in.shape[-1] == PAR_W and diff_w_in.shape[-1] == 3 * DIFF_W
    rows = seq // GRID_W
    scale = HEAD_DIM ** -0.5 * LOG2_E

    cond = jnp.zeros((8, d), F32).at[:b].set(c).at[b].set(c_ctx)
    mods_all = _ada(cond, ada_w, ada_b).reshape(2, 8, N_MOD, d)

    def mods_of(layer):
        m = mods_all[layer]
        return jnp.stack([jnp.broadcast_to(m[b], (b, N_MOD, d)), m[:b]], axis=1)

    rope = _rope_tables(seq, n_ctx)

    cols0 = np.concatenate([np.arange(NAQ0, GQ0), _head_cols(GQ0, _GQA_HEAD_ORDER), _head_cols(GK0, range(N_HEADS_GKV)),
                            np.arange(GV0, PAR_W)])
    col_scale0 = np.ones((PAR_W,), np.float32)
    col_scale0[NAQ0:NAK0] = scale
    w_in0 = (par_w_in[0][:, cols0] * col_scale0).astype(BF16)
    gains = jnp.zeros((8, LANE), F32)
    gains = gains.at[0].set(jnp.tile(gqa_q_gain[0][_DEINT] * scale, 2)).at[1].set(jnp.tile(gqa_k_gain[0][_DEINT], 2))
    block_mean = jnp.asarray(np.kron(np.eye(LANE // HEAD_DIM), np.full((HEAD_DIM, HEAD_DIM), 1.0 / HEAD_DIM)), BF16)
    naq, nak, nav, gq, gk, gv = _proj_par(ctx, x, mods_of(0), w_in0, rope, gains, block_mean)
    att_na = _na(naq, nak, nav, _na_bias_table(na_rpb[0], rows), n_ctx)
    att_g = _gqa(gq, gk, gv, n_ctx, tk=_key_chunk(n_ctx + seq))
    wo_na = par_w_out[0][0:NA_W].astype(BF16)
    wo_g = par_w_out[0][NA_W + np.concatenate([HEAD_DIM * h + np.arange(HEAD_DIM) for h in _GQA_HEAD_ORDER])].astype(BF16)
    xa = _ffn((ctx, x), mods_of(0), [att_na, att_g], [wo_na, wo_g],
              ffn_w_gate[0].astype(BF16), ffn_w_up[0].astype(BF16), ffn_w_down[0].astype(BF16))

    sub_heads = range(2 * N_HEADS_DIFF)
    cols1 = np.concatenate([_head_cols(0, sub_heads), _head_cols(DIFF_W, sub_heads), np.arange(2 * DIFF_W, 3 * DIFF_W)])
    col_scale1 = np.ones((3 * DIFF_W,), np.float32)
    col_scale1[0:DIFF_W] = scale
    w_in1 = (diff_w_in[0][:, cols1] * col_scale1).astype(BF16)
    dq, dk, dv = _proj_diff(xa, mods_of(1), w_in1, rope)
    lambda_init = 0.8 - 0.6 * float(np.exp(-0.3 * 1))
    lam_params = jnp.stack([diff_lambda_q1[0], diff_lambda_k1[0], diff_lambda_q2[0], diff_lambda_k2[0]]).astype(F32)
    att_d = _diff(lam_params, diff_subln_gain[0].reshape(1, -1).astype(F32), dq, dk, dv, n_ctx,
                  tk=_key_chunk(n_ctx + seq),
                  lambda_init=lambda_init)
    return _ffn(xa, mods_of(1), [att_d], [diff_w_out[0].astype(BF16)],
                ffn_w_gate[1].astype(BF16), ffn_w_up[1].astype(BF16), ffn_w_down[1].astype(BF16),
                final_gain=final_norm_gain.reshape(1, -1).astype(F32))
```

```python
import functools

import numpy as np
import jax
import jax.numpy as jnp
from jax import lax
from jax.experimental import pallas as pl
from jax.experimental.pallas import tpu as pltpu

F32 = jnp.float32
BF16 = jnp.bfloat16

GRID_W = 64
HEAD_DIM = 64
WIN_R = 8
WIN_C = 16
N_MOD = 6
ROPE_THETA = 10000.0
EPS = 1e-6

LANE = 128
TM = 256
NA_ROWS_PER_TILE = TM // GRID_W
NA_KEY_ROWS = 12
NA_WIN = NA_KEY_ROWS * GRID_W
ROW_BLOCK = 128
DIFF_HEADS_PER_STEP = 2
GQA_Q_TILES = 4
DIFF_Q_TILES = 8
SOFTMAX_ROWS = 32
LOG2_E = 1.4426950408889634
MXU_DEPTH = 256
NEG = -1e30
VMEM_LIMIT = 56 * 1024 * 1024

N_HEADS_NA = 8
N_HEADS_GQ = 8
N_HEADS_GKV = 2
N_HEADS_DIFF = 8
NA_W = N_HEADS_NA * HEAD_DIM
GQ_W = N_HEADS_GQ * HEAD_DIM
GKV_W = N_HEADS_GKV * HEAD_DIM
NAQ0, NAK0, NAV0, GQ0, GK0, GV0, PAR_W = (int(v) for v in np.cumsum([0, NA_W, NA_W, NA_W, GQ_W, GKV_W, GKV_W]))
DIFF_W = N_HEADS_DIFF * 2 * HEAD_DIM
ADA_TILE = 1024

_DEINT = np.concatenate([np.arange(0, HEAD_DIM, 2), np.arange(1, HEAD_DIM, 2)])
_GQA_HEAD_ORDER = (0, 4, 1, 5, 2, 6, 3, 7)


def _dot(a, b):
    return jnp.dot(a, b, preferred_element_type=F32)


def _dot_nt(a, b):
    return lax.dot_general(a, b, (((1,), (1,)), ((), ())), preferred_element_type=F32)


def _params(n_grid):
    return pltpu.CompilerParams(dimension_semantics=("arbitrary",) * n_grid, vmem_limit_bytes=VMEM_LIMIT)


def _const_spec(shape):
    return pl.BlockSpec(shape, lambda *_: (0,) * len(shape), pipeline_mode=pl.Buffered(1))


def _split_bf16(a):
    hi = a.astype(BF16)
    return hi, (a - hi.astype(F32)).astype(BF16)


def _ada_kernel(cond_ref, w_ref, b_ref, o_ref):
    c = cond_ref[...]
    a_hi, a_lo = _split_bf16(c * (1.0 / (1.0 + jnp.exp(-c))))
    w_hi, w_lo = _split_bf16(w_ref[0])
    o_ref[0] = _dot(a_hi, w_hi) + _dot(a_lo, w_hi) + _dot(a_hi, w_lo) + b_ref[0]


def _ada(cond, ada_w, ada_b):
    depth, d, n = ada_w.shape
    tn = ADA_TILE
    return pl.pallas_call(
        _ada_kernel,
        grid=(depth, n // tn),
        in_specs=[pl.BlockSpec(cond.shape, lambda l, j: (0, 0)),
                  pl.BlockSpec((1, d, tn), lambda l, j: (l, 0, j)),
                  pl.BlockSpec((1, 1, tn), lambda l, j: (l, 0, j))],
        out_specs=pl.BlockSpec((1, cond.shape[0], tn), lambda l, j: (l, 0, j)),
        out_shape=jax.ShapeDtypeStruct((depth, cond.shape[0], n), F32),
        compiler_params=_params(2),
        name="ada_modulation",
    )(cond, ada_w, ada_b.reshape(depth, 1, n))


def _modulated_norm(x, shift, scale):
    ms = jnp.mean(x * x, axis=-1, keepdims=True)
    return (x * lax.rsqrt(ms + EPS)) * (1.0 + scale) + shift


def _group_mean_sq(x, bd):
    hi, lo = _split_bf16(x * x)
    return _dot(hi, bd) + _dot(lo, bd)


def _rope(x, c, sa, sb):
    return x * c + pltpu.roll(x, 32, 1) * sa + pltpu.roll(x, 96, 1) * sb


def _ones_column(rows):
    return (lax.broadcasted_iota(jnp.int32, (rows, LANE), 1) == 0).astype(BF16)


def _store_values(v_ref, p, col0, n_blocks):
    ones = _ones_column(p.shape[0])
    for j in range(n_blocks):
        v_ref[0, :, 2 * LANE * j:2 * LANE * j + LANE] = p[:, col0 + LANE * j:col0 + LANE * (j + 1)].astype(BF16)
        v_ref[0, :, 2 * LANE * j + LANE:2 * LANE * (j + 1)] = ones


def _token_tile(ctx_ref, x_ref):
    return jnp.where(pl.program_id(1) == 0, ctx_ref[0], x_ref[0])


def _proj_par_kernel(ctx_ref, x_ref, mod_ref, w_ref, c_ref, sa_ref, sb_ref, gain_ref, bd_ref,
                     naq_ref, nak_ref, nav_ref, gq_ref, gk_ref, gv_ref):
    mod = mod_ref[0, 0]
    h = _modulated_norm(_token_tile(ctx_ref, x_ref), mod[0:1], mod[1:2]).astype(BF16)
    p = _dot(h, w_ref[...])
    naq_ref[0] = p[:, NAQ0:NAK0].astype(BF16)
    nak_ref[0] = p[:, NAK0:NAV0].astype(BF16)
    _store_values(nav_ref, p, NAV0, NA_W // LANE)
    c, sa, sb, bd = c_ref[...], sa_ref[...], sb_ref[...], bd_ref[...]

    def qk_norm_rope(g, gain):
        g = g * lax.rsqrt(_group_mean_sq(g, bd) + EPS) * gain
        return _rope(g, c, sa, sb).astype(BF16)

    for j in range(GQ_W // LANE):
        gq_ref[0, :, LANE * j:LANE * (j + 1)] = qk_norm_rope(p[:, GQ0 + LANE * j:GQ0 + LANE * (j + 1)], gain_ref[0:1, :])
    gk_ref[0] = qk_norm_rope(p[:, GK0:GV0], gain_ref[1:2, :])
    _store_values(gv_ref, p, GV0, GKV_W // LANE)


def _proj_diff_kernel(x_ref, mod_ref, w_ref, c_ref, sa_ref, sb_ref, dq_ref, dk_ref, dv_ref):
    mod = mod_ref[0, 0]
    h = _modulated_norm(x_ref[0], mod[0:1], mod[1:2]).astype(BF16)
    p = _dot(h, w_ref[...])
    c, sa, sb = c_ref[...], sa_ref[...], sb_ref[...]
    for j in range(DIFF_W // LANE):
        dq_ref[0, :, LANE * j:LANE * (j + 1)] = _rope(p[:, LANE * j:LANE * (j + 1)], c, sa, sb).astype(BF16)
        dk_ref[0, :, LANE * j:LANE * (j + 1)] = _rope(p[:, DIFF_W + LANE * j:DIFF_W + LANE * (j + 1)], c, sa, sb).astype(BF16)
    _store_values(dv_ref, p, 2 * DIFF_W, DIFF_W // LANE)


def _token_spec(width, tile_off=0):
    return pl.BlockSpec((1, TM, width), lambda b, t: (b, t + tile_off, 0))


def _split_token_specs(d):
    return [pl.BlockSpec((1, TM, d), lambda b, t: (b, 0, 0)),
            pl.BlockSpec((1, TM, d), lambda b, t: (b, jnp.maximum(t - 1, 0), 0))]


def _mod_spec(d, latent_only=False):
    if latent_only:
        return pl.BlockSpec((1, 1, N_MOD, d), lambda b, t: (b, 1, 0, 0))
    return pl.BlockSpec((1, 1, N_MOD, d), lambda b, t: (b, jnp.minimum(t, 1), 0, 0))


def _rope_spec():
    return pl.BlockSpec((TM, LANE), lambda b, t: (t, 0))


def _proj_par(ctx, x, mods, w, rope, gains, bd):
    b, seq, d = x.shape
    t_len = ctx.shape[1] + seq
    widths = (NA_W, NA_W, 2 * NA_W, GQ_W, GKV_W, 2 * GKV_W)
    return pl.pallas_call(
        _proj_par_kernel,
        grid=(b, t_len // TM),
        in_specs=_split_token_specs(d) + [_mod_spec(d), _const_spec(w.shape), _rope_spec(), _rope_spec(), _rope_spec(),
                                          _const_spec(gains.shape), _const_spec(bd.shape)],
        out_specs=[_token_spec(n) for n in widths],
        out_shape=[jax.ShapeDtypeStruct((b, t_len, n), BF16) for n in widths],
        compiler_params=_params(2),
        name="proj_parallel_mixer",
    )(ctx, x, mods, w, *rope, gains, bd)


def _proj_diff(xa, mods, w, rope):
    b, t_len, d = xa.shape
    widths = (DIFF_W, DIFF_W, 2 * DIFF_W)
    return pl.pallas_call(
        _proj_diff_kernel,
        grid=(b, t_len // TM),
        in_specs=[_token_spec(d), _mod_spec(d), _const_spec(w.shape), _rope_spec(), _rope_spec(), _rope_spec()],
        out_specs=[_token_spec(n) for n in widths],
        out_shape=[jax.ShapeDtypeStruct((b, t_len, n), BF16) for n in widths],
        compiler_params=_params(2),
        name="proj_diff_mixer",
    )(xa, mods, w, *rope)


def _split_heads(q):
    lane = lax.broadcasted_iota(jnp.int32, q.shape, 1)
    zero = jnp.zeros_like(q)
    return jnp.concatenate([jnp.where(lane < HEAD_DIM, q, zero), jnp.where(lane >= HEAD_DIM, q, zero)], axis=0)


def _softmax_update(s, m_prev):
    m_new = jnp.maximum(m_prev, jnp.max(s, axis=-1, keepdims=True))
    p = jnp.exp2(s - m_new)
    return p, jnp.exp2(m_prev - m_new), m_new


def _flash_chunk(q2_ref, k, v, m_ref, acc_ref, row0):
    for r in range(2 * TM // ROW_BLOCK):
        rows = slice(row0 + ROW_BLOCK * r, row0 + ROW_BLOCK * (r + 1))
        s = _dot_nt(q2_ref[rows], k)
        p, alpha, m_new = _softmax_update(s, m_ref[rows])
        acc_ref[rows] = alpha * acc_ref[rows] + _dot(p.astype(BF16), v)
        m_ref[rows] = m_new


def _flash_pipeline(q2_ref, k_of, v_of, n_chunks, n_blocks, m_ref, acc_ref, s_refs, p_refs):
    assert n_blocks % 2 == 0
    r2 = 2 * TM

    def score(i, j, slot):
        s_refs[slot][...] = _dot_nt(q2_ref[r2 * j:r2 * (j + 1)], k_of(i, j))

    def softmax(j, slot):
        for r in range(r2 // SOFTMAX_ROWS):
            rows = slice(SOFTMAX_ROWS * r, SOFTMAX_ROWS * (r + 1))
            state_rows = slice(r2 * j + SOFTMAX_ROWS * r, r2 * j + SOFTMAX_ROWS * (r + 1))
            p, alpha, m_new = _softmax_update(s_refs[slot][rows], m_ref[state_rows])
            p_refs[slot][rows] = p.astype(BF16)
            acc_ref[state_rows] = acc_ref[state_rows] * jnp.broadcast_to(alpha, (SOFTMAX_ROWS, 2 * LANE))
            m_ref[state_rows] = m_new

    def accumulate(i, j, slot):
        blk = slice(r2 * j, r2 * (j + 1))
        acc_ref[blk] += _dot(p_refs[slot][...], v_of(i, j))

    last_slot = (n_blocks - 1) % 2
    p_refs[last_slot][...] = jnp.zeros(p_refs[last_slot].shape, BF16)
    score(0, 0, 0)

    def body(i, carry):
        for j in range(n_blocks):
            if j + 1 < n_blocks:
                score(i, j + 1, (j + 1) % 2)
            else:
                score(jnp.minimum(i + 1, n_chunks - 1), 0, 0)
            softmax(j, j % 2)
            if j > 0:
                accumulate(i, j - 1, (j - 1) % 2)
            else:
                accumulate(jnp.maximum(i - 1, 0), n_blocks - 1, last_slot)
        return carry

    lax.fori_loop(0, n_chunks, body, 0)
    accumulate(n_chunks - 1, n_blocks - 1, last_slot)


def _pipeline_scratch(tk):
    rows = 2 * TM
    return [pltpu.VMEM((rows, tk), F32)] * 2 + [pltpu.VMEM((rows, tk), BF16)] * 2


def _normalize(acc):
    return acc[:, 0:LANE] / acc[:, LANE:LANE + 1]


def _merge_heads(o):
    lane = lax.broadcasted_iota(jnp.int32, (TM, LANE), 1)
    return jnp.where(lane < HEAD_DIM, o[0:TM], o[TM:2 * TM])


def _init_flash(q_refs, q2_ref, m_ref, acc_ref):
    n = 0
    for q_ref in q_refs:
        for j in range(q_ref.shape[2] // LANE):
            q2_ref[2 * TM * n:2 * TM * (n + 1)] = _split_heads(q_ref[0, :, LANE * j:LANE * (j + 1)])
            n += 1
    m_ref[...] = jnp.full(m_ref.shape, NEG, F32)
    acc_ref[...] = jnp.zeros(acc_ref.shape, F32)


def _flash_scratch(n_blocks):
    rows = 2 * TM * n_blocks
    return [pltpu.VMEM((rows, LANE), BF16), pltpu.VMEM((rows, 1), F32), pltpu.VMEM((rows, 2 * LANE), F32)]


def _key_chunk(t_len):
    return next(tk for tk in (768, 640, 512, 384, 256) if t_len % tk == 0)


def _store_heads(o_ref, acc_ref, n_tiles, finish):
    n_lane_blocks = o_ref.shape[2] // LANE
    for u in range(n_tiles):
        for j in range(n_lane_blocks):
            n = u * n_lane_blocks + j
            o = _normalize(acc_ref[2 * TM * n:2 * TM * (n + 1)])
            o_ref[0, TM * u:TM * (u + 1), LANE * j:LANE * (j + 1)] = finish(o).astype(BF16)


def _gqa_ctx_kernel(q_ref, k_ref, v_ref, o_ref, q2_ref, m_ref, acc_ref):
    _init_flash([q_ref], q2_ref, m_ref, acc_ref)
    for j in range(q_ref.shape[2] // LANE):
        _flash_chunk(q2_ref, k_ref[0], v_ref[0], m_ref, acc_ref, 2 * TM * j)
    _store_heads(o_ref, acc_ref, 1, _merge_heads)


def _gqa_kernel(*refs, tk):
    q_refs = refs[:GQA_Q_TILES]
    k_ref, v_ref, o_ref, q2_ref, m_ref, acc_ref = refs[GQA_Q_TILES:GQA_Q_TILES + 6]
    pipe_refs = refs[GQA_Q_TILES + 6:]
    _init_flash(q_refs, q2_ref, m_ref, acc_ref)

    def chunk_of(ref):
        return lambda i, n: ref[0, pl.ds(pl.multiple_of(i * tk, tk), tk), :]
    _flash_pipeline(q2_ref, chunk_of(k_ref), chunk_of(v_ref), k_ref.shape[1] // tk,
                    GQA_Q_TILES * (o_ref.shape[2] // LANE), m_ref, acc_ref,
                    pipe_refs[0:2], pipe_refs[2:4])
    _store_heads(o_ref, acc_ref, GQA_Q_TILES, _merge_heads)


def _gqa(gq, gk, gv, n_ctx, tk):
    b, t_len, qw = gq.shape
    assert n_ctx == TM and (t_len - n_ctx) % (GQA_Q_TILES * TM) == 0
    n_lane_blocks = qw // LANE
    ctx_out = pl.pallas_call(
        _gqa_ctx_kernel,
        grid=(b,),
        in_specs=[pl.BlockSpec((1, TM, qw), lambda b: (b, 0, 0)),
                  pl.BlockSpec((1, TM, LANE), lambda b: (b, 0, 0)),
                  pl.BlockSpec((1, TM, 2 * LANE), lambda b: (b, 0, 0))],
        out_specs=pl.BlockSpec((1, TM, qw), lambda b: (b, 0, 0)),
        out_shape=jax.ShapeDtypeStruct((b, TM, qw), BF16),
        scratch_shapes=_flash_scratch(n_lane_blocks),
        compiler_params=_params(1),
        name="gqa_attention_ctx",
    )(gq, gk, gv)
    latent_out = pl.pallas_call(
        functools.partial(_gqa_kernel, tk=tk),
        grid=(b, (t_len - n_ctx) // (GQA_Q_TILES * TM)),
        in_specs=([pl.BlockSpec((1, TM, qw), lambda b, s, u=u: (b, 1 + GQA_Q_TILES * s + u, 0))
                   for u in range(GQA_Q_TILES)]
                  + [pl.BlockSpec((1, t_len, LANE), lambda b, s: (b, 0, 0)),
                     pl.BlockSpec((1, t_len, 2 * LANE), lambda b, s: (b, 0, 0))]),
        out_specs=pl.BlockSpec((1, GQA_Q_TILES * TM, qw), lambda b, s: (b, s, 0)),
        out_shape=jax.ShapeDtypeStruct((b, t_len - n_ctx, qw), BF16),
        scratch_shapes=_flash_scratch(GQA_Q_TILES * n_lane_blocks) + _pipeline_scratch(tk),
        compiler_params=_params(2),
        name="gqa_attention",
    )(*([gq] * GQA_Q_TILES), gk, gv)
    return jnp.concatenate([ctx_out, latent_out], axis=1)


def _na_kernel(q_ref, k_ref, v_ref, bias_ref, o_ref, q2_ref, m_ref, acc_ref, *pipe_refs, n_ctx, rows):
    t = pl.program_id(1)
    n_blocks = q_ref.shape[2] // LANE
    r2 = 2 * TM
    s_refs, p_refs = pipe_refs[0:2], pipe_refs[2:4]
    _init_flash([q_ref], q2_ref, m_ref, acc_ref)

    @pl.when(t == 0)
    def _():
        for j in range(n_blocks):
            _flash_chunk(q2_ref, k_ref[0, 0:n_ctx, LANE * j:LANE * (j + 1)],
                         v_ref[0, 0:n_ctx, 2 * LANE * j:2 * LANE * (j + 1)], m_ref, acc_ref, r2 * j)
            o_ref[0, :, LANE * j:LANE * (j + 1)] = _merge_heads(_normalize(acc_ref[r2 * j:r2 * (j + 1)])).astype(BF16)

    @pl.when(t > 0)
    def _():
        first_row = NA_ROWS_PER_TILE * (t - 1)
        start = jnp.clip(first_row - WIN_R // 2, 0, rows - NA_KEY_ROWS)
        off = pl.multiple_of(n_ctx + GRID_W * start, GRID_W)

        def score(j, slot):
            q2 = q2_ref[r2 * j:r2 * (j + 1)]
            s_refs[slot][:, 0:n_ctx] = _dot_nt(q2, k_ref[0, 0:n_ctx, LANE * j:LANE * (j + 1)])
            s_refs[slot][:, n_ctx:] = _dot_nt(q2, k_ref[0, pl.ds(off, NA_WIN), LANE * j:LANE * (j + 1)])

        def softmax(j, slot):
            for r in range(r2 // SOFTMAX_ROWS):
                rows_r = slice(SOFTMAX_ROWS * r, SOFTMAX_ROWS * (r + 1))
                head, row = divmod(SOFTMAX_ROWS * r, TM)
                s_ctx = s_refs[slot][rows_r, 0:n_ctx]
                s_win = s_refs[slot][rows_r, n_ctx:] + bias_ref[2 * j + head, 0, row:row + SOFTMAX_ROWS, :]
                m = jnp.maximum(jnp.max(s_ctx, axis=-1, keepdims=True), jnp.max(s_win, axis=-1, keepdims=True))
                p_refs[slot][rows_r, 0:n_ctx] = jnp.exp2(s_ctx - m).astype(BF16)
                p_refs[slot][rows_r, n_ctx:] = jnp.exp2(s_win - m).astype(BF16)

        def output(j, slot):
            acc = (_dot(p_refs[slot][:, 0:n_ctx], v_ref[0, 0:n_ctx, 2 * LANE * j:2 * LANE * (j + 1)])
                   + _dot(p_refs[slot][:, n_ctx:], v_ref[0, pl.ds(off, NA_WIN), 2 * LANE * j:2 * LANE * (j + 1)]))
            o_ref[0, :, LANE * j:LANE * (j + 1)] = _merge_heads(_normalize(acc)).astype(BF16)

        score(0, 0)
        for j in range(n_blocks):
            if j + 1 < n_blocks:
                score(j + 1, (j + 1) % 2)
            softmax(j, j % 2)
            if j > 0:
                output(j - 1, (j - 1) % 2)
        output(n_blocks - 1, (n_blocks - 1) % 2)


def _na(naq, nak, nav, bias, n_ctx):
    b, t_len, qw = naq.shape
    n_tiles = t_len // TM
    rows = (t_len - n_ctx) // GRID_W
    n_keys = n_ctx + NA_WIN

    def bias_index(b, t):
        return (0, jnp.where(t <= 1, 0, jnp.where(t == n_tiles - 1, 2, 1)), 0, 0)

    def resident(width):
        return pl.BlockSpec((1, t_len, width), lambda b, t: (b, 0, 0), pipeline_mode=pl.Buffered(1))

    return pl.pallas_call(
        functools.partial(_na_kernel, n_ctx=n_ctx, rows=rows),
        grid=(b, n_tiles),
        in_specs=[pl.BlockSpec((1, TM, qw), lambda b, t: (b, t, 0)), resident(qw), resident(2 * qw),
                  pl.BlockSpec((bias.shape[0], 1, TM, NA_WIN), bias_index, pipeline_mode=pl.Buffered(1))],
        out_specs=pl.BlockSpec((1, TM, qw), lambda b, t: (b, t, 0)),
        out_shape=jax.ShapeDtypeStruct((b, t_len, qw), BF16),
        scratch_shapes=(_flash_scratch(qw // LANE)
                        + [pltpu.VMEM((2 * TM, n_keys), F32)] * 2 + [pltpu.VMEM((2 * TM, n_keys), BF16)] * 2),
        compiler_params=_params(2),
        name="neighbourhood_attention",
    )(naq, nak, nav, bias)


def _na_bias_table(rpb, rows):
    g_of_pattern = np.array([0, 2, rows // NA_ROWS_PER_TILE - 1])
    a = np.arange(NA_ROWS_PER_TILE)
    r = NA_ROWS_PER_TILE * g_of_pattern[:, None] + a[None, :]
    start = np.clip(NA_ROWS_PER_TILE * g_of_pattern - WIN_R // 2, 0, rows - NA_KEY_ROWS)
    rs = np.clip(r - WIN_R // 2, 0, rows - WIN_R)
    key_row = start[:, None] + np.arange(NA_KEY_ROWS)[None, :]
    row_ok = (key_row[:, None, :] >= rs[:, :, None]) & (key_row[:, None, :] < rs[:, :, None] + WIN_R)
    row_off = np.clip(key_row[:, None, :] - r[:, :, None] + (WIN_R - 1), 0, 2 * WIN_R - 2)
    cols = np.arange(GRID_W)
    col_start = np.clip(cols - WIN_C // 2, 0, GRID_W - WIN_C)
    col_ok = (cols[None, :] >= col_start[:, None]) & (cols[None, :] < col_start[:, None] + WIN_C)
    col_off = np.clip(cols[None, :] - cols[:, None] + (WIN_C - 1), 0, 2 * WIN_C - 2)
    ok = row_ok[:, :, None, :, None] & col_ok[None, None, :, None, :]
    bias_rows = rpb.astype(F32)[:, row_off]
    pick_col = (col_off[:, :, None] == np.arange(2 * WIN_C - 1)).astype(np.float32)
    vals = jnp.einsum('hpakb,cjb->hpackj', bias_rows, pick_col, precision=lax.Precision.HIGHEST)
    table = jnp.where(ok[None], vals * LOG2_E, NEG)
    return table.reshape(rpb.shape[0], 3, TM, NA_WIN)


def _diff_kernel(lam_ref, gain_ref, *refs, tk, lambda_init):
    q_refs = refs[:DIFF_Q_TILES]
    k_ref, v_ref, o_ref, q2_ref, m_ref, acc_ref = refs[DIFF_Q_TILES:DIFF_Q_TILES + 6]
    pipe_refs = refs[DIFF_Q_TILES + 6:]
    n_heads = o_ref.shape[2] // LANE
    _init_flash(q_refs, q2_ref, m_ref, acc_ref)

    def chunk_of(ref, width):
        def chunk(i, n):
            h = n % n_heads
            return ref[0, pl.ds(pl.multiple_of(i * tk, tk), tk), width * h:width * (h + 1)]
        return chunk
    _flash_pipeline(q2_ref, chunk_of(k_ref, LANE), chunk_of(v_ref, 2 * LANE), k_ref.shape[1] // tk,
                    DIFF_Q_TILES * n_heads, m_ref, acc_ref, pipe_refs[0:2], pipe_refs[2:4])

    lp = lam_ref[...]
    lam = (jnp.exp(jnp.sum(lp[0:1] * lp[1:2], axis=-1, keepdims=True))
           - jnp.exp(jnp.sum(lp[2:3] * lp[3:4], axis=-1, keepdims=True)) + lambda_init)

    def sub_layer_norm(o):
        d = o[0:TM] - lam * o[TM:2 * TM]
        y = d * lax.rsqrt(jnp.mean(d * d, axis=-1, keepdims=True) + EPS) * gain_ref[...]
        return y * (1.0 - lambda_init)
    _store_heads(o_ref, acc_ref, DIFF_Q_TILES, sub_layer_norm)


def _diff(lam_params, subln_gain, dq, dk, dv, n_ctx, tk, lambda_init):
    b, t_len, qw = dq.shape
    assert n_ctx == TM and (t_len - n_ctx) % (DIFF_Q_TILES * TM) == 0
    hw = LANE * DIFF_HEADS_PER_STEP
    return pl.pallas_call(
        functools.partial(_diff_kernel, tk=tk, lambda_init=lambda_init),
        grid=(b, qw // hw, (t_len - n_ctx) // (DIFF_Q_TILES * TM)),
        in_specs=([pl.BlockSpec(lam_params.shape, lambda b, h, s: (0, 0)),
                   pl.BlockSpec(subln_gain.shape, lambda b, h, s: (0, 0))]
                  + [pl.BlockSpec((1, TM, hw), lambda b, h, s, u=u: (b, 1 + DIFF_Q_TILES * s + u, h))
                     for u in range(DIFF_Q_TILES)]
                  + [pl.BlockSpec((1, t_len, hw), lambda b, h, s: (b, 0, h)),
                     pl.BlockSpec((1, t_len, 2 * hw), lambda b, h, s: (b, 0, h))]),
        out_specs=pl.BlockSpec((1, DIFF_Q_TILES * TM, hw), lambda b, h, s: (b, s, h)),
        out_shape=jax.ShapeDtypeStruct((b, t_len - n_ctx, qw), BF16),
        scratch_shapes=_flash_scratch(DIFF_Q_TILES * DIFF_HEADS_PER_STEP) + _pipeline_scratch(tk),
        compiler_params=_params(3),
        name="diff_attention",
    )(lam_params, subln_gain, *([dq] * DIFF_Q_TILES), dk, dv)


def _ffn_kernel(*refs, n_att, ff_chunks, final, split_tokens):
    if split_tokens:
        x_tile = _token_tile(refs[0], refs[1])
        refs = refs[1:]
    else:
        x_tile = refs[0][0]
    mod_ref = refs[1]
    att_refs = refs[2:2 + n_att]
    wo_refs = refs[2 + n_att:2 + 2 * n_att]
    wg_ref, wu_ref, wd_ref = refs[2 + 2 * n_att:5 + 2 * n_att]
    o_ref = refs[-1]
    mod = mod_ref[0, 0]
    y = _dot(att_refs[0][0], wo_refs[0][...])
    for a_ref, w_ref in zip(att_refs[1:], wo_refs[1:]):
        y = y + _dot(a_ref[0], w_ref[...])
    x1 = x_tile + mod[2:3] * y
    h = _modulated_norm(x1, mod[3:4], mod[4:5]).astype(BF16)
    d_ff = wg_ref.shape[1]
    bounds = [0]
    for i in range(ff_chunks):
        bounds.append(min(d_ff, -(-(d_ff * (i + 1) // ff_chunks) // MXU_DEPTH) * MXU_DEPTH))
    down = None
    for lo, hi in zip(bounds[:-1], bounds[1:]):
        g = _dot(h, wg_ref[:, lo:hi])
        u = _dot(h, wu_ref[:, lo:hi])
        a = (g * (1.0 / (1.0 + jnp.exp(-g))) * u).astype(BF16)
        part = _dot(a, wd_ref[lo:hi, :])
        down = part if down is None else down + part
    x2 = x1 + mod[5:6] * down
    if final:
        gain_ref = refs[5 + 2 * n_att]
        x2 = x2 * lax.rsqrt(jnp.mean(x2 * x2, axis=-1, keepdims=True) + EPS) * gain_ref[...]
    o_ref[0] = x2


def _ffn(tokens, mods, atts, wos, wg, wu, wd, final_gain=None):
    split_tokens = isinstance(tokens, tuple)
    n_rows = atts[0].shape[1]
    b, _, d = tokens[-1].shape if split_tokens else tokens.shape
    final = final_gain is not None
    if split_tokens:
        token_specs = _split_token_specs(d)
        tokens = list(tokens)
    else:
        token_specs = [_token_spec(d, (tokens.shape[1] - n_rows) // TM)]
        tokens = [tokens]
    in_specs = (token_specs + [_mod_spec(d, latent_only=not split_tokens)]
                + [_token_spec(a.shape[2]) for a in atts]
                + [_const_spec(w.shape) for w in wos]
                + [_const_spec(wg.shape), _const_spec(wu.shape), _const_spec(wd.shape)])
    args = [*tokens, mods, *atts, *wos, wg, wu, wd]
    if final:
        in_specs.append(_const_spec(final_gain.shape))
        args.append(final_gain)
    return pl.pallas_call(
        functools.partial(_ffn_kernel, n_att=len(atts), ff_chunks=2, final=final, split_tokens=split_tokens),
        grid=(b, n_rows // TM),
        in_specs=in_specs,
        out_specs=_token_spec(d),
        out_shape=jax.ShapeDtypeStruct((b, n_rows, d), F32),
        compiler_params=_params(2),
        name="outproj_ffn_final" if final else "outproj_ffn",
    )(*args)


def _head_cols(base, heads):
    return np.concatenate([base + HEAD_DIM * h + _DEINT for h in heads])


def _rope_tables(seq, n_ctx):
    t = jnp.arange(seq)
    row = (t // GRID_W).astype(F32)
    col = (t % GRID_W).astype(F32)
    n_freq = HEAD_DIM // 4
    inv = ROPE_THETA ** (-jnp.arange(n_freq, dtype=F32) / n_freq)
    ang = jnp.concatenate([row[:, None] * inv, col[:, None] * inv], axis=-1)
    cos, sin = jnp.cos(ang), jnp.sin(ang)
    zero = jnp.zeros_like(sin)

    def table(first_half, second_half, ctx_value):
        lat = jnp.tile(jnp.concatenate([first_half, second_half], axis=-1), (1, LANE // HEAD_DIM))
        return jnp.concatenate([jnp.full((n_ctx, LANE), ctx_value, F32), lat], axis=0)

    return table(cos, cos, 1.0), table(zero, sin, 0.0), table(-sin, zero, 0.0)


def kernel(x, c, ctx, c_ctx, ada_w, ada_b, ffn_w_gate, ffn_w_up, ffn_w_down, par_w_in, par_w_out, na_rpb,
           gqa_q_gain, gqa_k_gain, diff_w_in, diff_w_out, diff_lambda_q1, diff_lambda_k1, diff_lambda_q2,
           diff_lambda_k2, diff_subln_gain, final_norm_gain):
    b, seq, d = x.shape
    n_ctx = ctx.shape[1]
    assert n_ctx == TM and seq % TM == 0 and d % LANE == 0 and b < 8 and ada_w.shape[0] == 2
    assert par_w_in.shape[-1] == PAR_W and diff_w_in.shape[-1] == 3 * DIFF_W
    rows = seq // GRID_W
    scale = HEAD_DIM ** -0.5 * LOG2_E

    cond = jnp.zeros((8, d), F32).at[:b].set(c).at[b].set(c_ctx)
    mods_all = _ada(cond, ada_w, ada_b).reshape(2, 8, N_MOD, d)

    def mods_of(layer):
        m = mods_all[layer]
        return jnp.stack([jnp.broadcast_to(m[b], (b, N_MOD, d)), m[:b]], axis=1)

    rope = _rope_tables(seq, n_ctx)

    cols0 = np.concatenate([np.arange(NAQ0, GQ0), _head_cols(GQ0, _GQA_HEAD_ORDER), _head_cols(GK0, range(N_HEADS_GKV)),
                            np.arange(GV0, PAR_W)])
    col_scale0 = np.ones((PAR_W,), np.float32)
    col_scale0[NAQ0:NAK0] = scale
    w_in0 = (par_w_in[0][:, cols0] * col_scale0).astype(BF16)
    gains = jnp.zeros((8, LANE), F32)
    gains = gains.at[0].set(jnp.tile(gqa_q_gain[0][_DEINT] * scale, 2)).at[1].set(jnp.tile(gqa_k_gain[0][_DEINT], 2))
    block_mean = jnp.asarray(np.kron(np.eye(LANE // HEAD_DIM), np.full((HEAD_DIM, HEAD_DIM), 1.0 / HEAD_DIM)), BF16)
    naq, nak, nav, gq, gk, gv = _proj_par(ctx, x, mods_of(0), w_in0, rope, gains, block_mean)
    att_na = _na(naq, nak, nav, _na_bias_table(na_rpb[0], rows), n_ctx)
    att_g = _gqa(gq, gk, gv, n_ctx, tk=_key_chunk(n_ctx + seq))
    wo_na = par_w_out[0][0:NA_W].astype(BF16)
    wo_g = par_w_out[0][NA_W + np.concatenate([HEAD_DIM * h + np.arange(HEAD_DIM) for h in _GQA_HEAD_ORDER])].astype(BF16)
    xa = _ffn((ctx, x), mods_of(0), [att_na, att_g], [wo_na, wo_g],
              ffn_w_gate[0].astype(BF16), ffn_w_up[0].astype(BF16), ffn_w_down[0].astype(BF16))

    sub_heads = range(2 * N_HEADS_DIFF)
    cols1 = np.concatenate([_head_cols(0, sub_heads), _head_cols(DIFF_W, sub_heads), np.arange(2 * DIFF_W, 3 * DIFF_W)])
    col_scale1 = np.ones((3 * DIFF_W,), np.float32)
    col_scale1[0:DIFF_W] = scale
    w_in1 = (diff_w_in[0][:, cols1] * col_scale1).astype(BF16)
    dq, dk, dv = _proj_diff(xa, mods_of(1), w_in1, rope)
    lambda_init = 0.8 - 0.6 * float(np.exp(-0.3 * 1))
    lam_params = jnp.stack([diff_lambda_q1[0], diff_lambda_k1[0], diff_lambda_q2[0], diff_lambda_k2[0]]).astype(F32)
    att_d = _diff(lam_params, diff_subln_gain[0].reshape(1, -1).astype(F32), dq, dk, dv, n_ctx,
                  tk=_key_chunk(n_ctx + seq),
                  lambda_init=lambda_init)
    return _ffn(xa, mods_of(1), [att_d], [diff_w_out[0].astype(BF16)],
                ffn_w_gate[1].astype(BF16), ffn_w_up[1].astype(BF16), ffn_w_down[1].astype(BF16),
                final_gain=final_norm_gain.reshape(1, -1).astype(F32))
```

```python
import functools

import numpy as np
import jax
import jax.numpy as jnp
from jax import lax
from jax.experimental import pallas as pl
from jax.experimental.pallas import tpu as pltpu

F32 = jnp.float32
BF16 = jnp.bfloat16

GRID_W = 64
HEAD_DIM = 64
WIN_R = 8
WIN_C = 16
N_MOD = 6
ROPE_THETA = 10000.0
EPS = 1e-6

LANE = 128
TM = 256
NA_ROWS_PER_TILE = TM // GRID_W
NA_KEY_ROWS = 12
NA_WIN = NA_KEY_ROWS * GRID_W
ROW_BLOCK = 128
DIFF_HEADS_PER_STEP = 2
GQA_Q_TILES = 4
DIFF_Q_TILES = 8
SOFTMAX_ROWS = 32
LOG2_E = 1.4426950408889634
MXU_DEPTH = 256
NEG = -1e30
VMEM_LIMIT = 56 * 1024 * 1024

N_HEADS_NA = 8
N_HEADS_GQ = 8
N_HEADS_GKV = 2
N_HEADS_DIFF = 8
NA_W = N_HEADS_NA * HEAD_DIM
GQ_W = N_HEADS_GQ * HEAD_DIM
GKV_W = N_HEADS_GKV * HEAD_DIM
NAQ0, NAK0, NAV0, GQ0, GK0, GV0, PAR_W = (int(v) for v in np.cumsum([0, NA_W, NA_W, NA_W, GQ_W, GKV_W, GKV_W]))
DIFF_W = N_HEADS_DIFF * 2 * HEAD_DIM
ADA_TILE = 1024

_DEINT = np.concatenate([np.arange(0, HEAD_DIM, 2), np.arange(1, HEAD_DIM, 2)])
_GQA_HEAD_ORDER = (0, 4, 1, 5, 2, 6, 3, 7)


def _dot(a, b):
    return jnp.dot(a, b, preferred_element_type=F32)


def _dot_nt(a, b):
    return lax.dot_general(a, b, (((1,), (1,)), ((), ())), preferred_element_type=F32)


def _params(n_grid):
    return pltpu.CompilerParams(dimension_semantics=("arbitrary",) * n_grid, vmem_limit_bytes=VMEM_LIMIT)


def _const_spec(shape):
    return pl.BlockSpec(shape, lambda *_: (0,) * len(shape), pipeline_mode=pl.Buffered(1))


def _split_bf16(a):
    hi = a.astype(BF16)
    return hi, (a - hi.astype(F32)).astype(BF16)


def _ada_kernel(cond_ref, w_ref, b_ref, o_ref):
    c = cond_ref[...]
    a_hi, a_lo = _split_bf16(c * (1.0 / (1.0 + jnp.exp(-c))))
    w_hi, w_lo = _split_bf16(w_ref[0])
    o_ref[0] = _dot(a_hi, w_hi) + _dot(a_lo, w_hi) + _dot(a_hi, w_lo) + b_ref[0]


def _ada(cond, ada_w, ada_b):
    depth, d, n = ada_w.shape
    tn = ADA_TILE
    return pl.pallas_call(
        _ada_kernel,
        grid=(depth, n // tn),
        in_specs=[pl.BlockSpec(cond.shape, lambda l, j: (0, 0)),
                  pl.BlockSpec((1, d, tn), lambda l, j: (l, 0, j)),
                  pl.BlockSpec((1, 1, tn), lambda l, j: (l, 0, j))],
        out_specs=pl.BlockSpec((1, cond.shape[0], tn), lambda l, j: (l, 0, j)),
        out_shape=jax.ShapeDtypeStruct((depth, cond.shape[0], n), F32),
        compiler_params=_params(2),
        name="ada_modulation",
    )(cond, ada_w, ada_b.reshape(depth, 1, n))


def _modulated_norm(x, shift, scale):
    ms = jnp.mean(x * x, axis=-1, keepdims=True)
    return (x * lax.rsqrt(ms + EPS)) * (1.0 + scale) + shift


def _group_mean_sq(x, bd):
    hi, lo = _split_bf16(x * x)
    return _dot(hi, bd) + _dot(lo, bd)


def _rope(x, c, sa, sb):
    return x * c + pltpu.roll(x, 32, 1) * sa + pltpu.roll(x, 96, 1) * sb


def _ones_column(rows):
    return (lax.broadcasted_iota(jnp.int32, (rows, LANE), 1) == 0).astype(BF16)


def _store_values(v_ref, p, col0, n_blocks):
    ones = _ones_column(p.shape[0])
    for j in range(n_blocks):
        v_ref[0, :, 2 * LANE * j:2 * LANE * j + LANE] = p[:, col0 + LANE * j:col0 + LANE * (j + 1)].astype(BF16)
        v_ref[0, :, 2 * LANE * j + LANE:2 * LANE * (j + 1)] = ones


def _token_tile(ctx_ref, x_ref):
    return jnp.where(pl.program_id(1) == 0, ctx_ref[0], x_ref[0])


def _proj_par_kernel(ctx_ref, x_ref, mod_ref, w_ref, c_ref, sa_ref, sb_ref, gain_ref, bd_ref,
                     naq_ref, nak_ref, nav_ref, gq_ref, gk_ref, gv_ref):
    mod = mod_ref[0, 0]
    h = _modulated_norm(_token_tile(ctx_ref, x_ref), mod[0:1], mod[1:2]).astype(BF16)
    p = _dot(h, w_ref[...])
    naq_ref[0] = p[:, NAQ0:NAK0].astype(BF16)
    nak_ref[0] = p[:, NAK0:NAV0].astype(BF16)
    _store_values(nav_ref, p, NAV0, NA_W // LANE)
    c, sa, sb, bd = c_ref[...], sa_ref[...], sb_ref[...], bd_ref[...]

    def qk_norm_rope(g, gain):
        g = g * lax.rsqrt(_group_mean_sq(g, bd) + EPS) * gain
        return _rope(g, c, sa, sb).astype(BF16)

    for j in range(GQ_W // LANE):
        gq_ref[0, :, LANE * j:LANE * (j + 1)] = qk_norm_rope(p[:, GQ0 + LANE * j:GQ0 + LANE * (j + 1)], gain_ref[0:1, :])
    gk_ref[0] = qk_norm_rope(p[:, GK0:GV0], gain_ref[1:2, :])
    _store_values(gv_ref, p, GV0, GKV_W // LANE)


def _diff_projection(x, mod, w_ref, c_ref, sa_ref, sb_ref, dq_ref, dk_ref, dv_ref):
    h = _modulated_norm(x, mod[0:1], mod[1:2]).astype(BF16)
    p = _dot(h, w_ref[...])
    c, sa, sb = c_ref[...], sa_ref[...], sb_ref[...]
    for j in range(DIFF_W // LANE):
        dq_ref[0, :, LANE * j:LANE * (j + 1)] = _rope(p[:, LANE * j:LANE * (j + 1)], c, sa, sb).astype(BF16)
        dk_ref[0, :, LANE * j:LANE * (j + 1)] = _rope(p[:, DIFF_W + LANE * j:DIFF_W + LANE * (j + 1)], c, sa, sb).astype(BF16)
    _store_values(dv_ref, p, 2 * DIFF_W, DIFF_W // LANE)


def _token_spec(width, tile_off=0):
    return pl.BlockSpec((1, TM, width), lambda b, t: (b, t + tile_off, 0))


def _split_token_specs(d):
    return [pl.BlockSpec((1, TM, d), lambda b, t: (b, 0, 0)),
            pl.BlockSpec((1, TM, d), lambda b, t: (b, jnp.maximum(t - 1, 0), 0))]


def _mod_spec(d, latent_only=False):
    if latent_only:
        return pl.BlockSpec((1, 1, N_MOD, d), lambda b, t: (b, 1, 0, 0))
    return pl.BlockSpec((1, 1, N_MOD, d), lambda b, t: (b, jnp.minimum(t, 1), 0, 0))


def _rope_spec():
    return pl.BlockSpec((TM, LANE), lambda b, t: (t, 0))


def _proj_par(ctx, x, mods, w, rope, gains, bd):
    b, seq, d = x.shape
    t_len = ctx.shape[1] + seq
    widths = (NA_W, NA_W, 2 * NA_W, GQ_W, GKV_W, 2 * GKV_W)
    return pl.pallas_call(
        _proj_par_kernel,
        grid=(b, t_len // TM),
        in_specs=_split_token_specs(d) + [_mod_spec(d), _const_spec(w.shape), _rope_spec(), _rope_spec(), _rope_spec(),
                                          _const_spec(gains.shape), _const_spec(bd.shape)],
        out_specs=[_token_spec(n) for n in widths],
        out_shape=[jax.ShapeDtypeStruct((b, t_len, n), BF16) for n in widths],
        compiler_params=_params(2),
        name="proj_parallel_mixer",
    )(ctx, x, mods, w, *rope, gains, bd)


def _split_heads(q):
    lane = lax.broadcasted_iota(jnp.int32, q.shape, 1)
    zero = jnp.zeros_like(q)
    return jnp.concatenate([jnp.where(lane < HEAD_DIM, q, zero), jnp.where(lane >= HEAD_DIM, q, zero)], axis=0)


def _softmax_update(s, m_prev):
    m_new = jnp.maximum(m_prev, jnp.max(s, axis=-1, keepdims=True))
    p = jnp.exp2(s - m_new)
    return p, jnp.exp2(m_prev - m_new), m_new


def _flash_chunk(q2_ref, k, v, m_ref, acc_ref, row0):
    for r in range(2 * TM // ROW_BLOCK):
        rows = slice(row0 + ROW_BLOCK * r, row0 + ROW_BLOCK * (r + 1))
        s = _dot_nt(q2_ref[rows], k)
        p, alpha, m_new = _softmax_update(s, m_ref[rows])
        acc_ref[rows] = alpha * acc_ref[rows] + _dot(p.astype(BF16), v)
        m_ref[rows] = m_new


def _flash_pipeline(q2_ref, k_of, v_of, n_chunks, n_blocks, m_ref, acc_ref, s_refs, p_refs):
    assert n_blocks % 2 == 0
    r2 = 2 * TM

    def score(i, j, slot):
        s_refs[slot][...] = _dot_nt(q2_ref[r2 * j:r2 * (j + 1)], k_of(i, j))

    def softmax(j, slot):
        for r in range(r2 // SOFTMAX_ROWS):
            rows = slice(SOFTMAX_ROWS * r, SOFTMAX_ROWS * (r + 1))
            state_rows = slice(r2 * j + SOFTMAX_ROWS * r, r2 * j + SOFTMAX_ROWS * (r + 1))
            p, alpha, m_new = _softmax_update(s_refs[slot][rows], m_ref[state_rows])
            p_refs[slot][rows] = p.astype(BF16)
            acc_ref[state_rows] = acc_ref[state_rows] * jnp.broadcast_to(alpha, (SOFTMAX_ROWS, 2 * LANE))
            m_ref[state_rows] = m_new

    def accumulate(i, j, slot):
        blk = slice(r2 * j, r2 * (j + 1))
        acc_ref[blk] += _dot(p_refs[slot][...], v_of(i, j))

    last_slot = (n_blocks - 1) % 2
    p_refs[last_slot][...] = jnp.zeros(p_refs[last_slot].shape, BF16)
    score(0, 0, 0)

    def body(i, carry):
        for j in range(n_blocks):
            if j + 1 < n_blocks:
                score(i, j + 1, (j + 1) % 2)
            else:
                score(jnp.minimum(i + 1, n_chunks - 1), 0, 0)
            softmax(j, j % 2)
            if j > 0:
                accumulate(i, j - 1, (j - 1) % 2)
            else:
                accumulate(jnp.maximum(i - 1, 0), n_blocks - 1, last_slot)
        return carry

    lax.fori_loop(0, n_chunks, body, 0)
    accumulate(n_chunks - 1, n_blocks - 1, last_slot)


def _pipeline_scratch(tk):
    rows = 2 * TM
    return [pltpu.VMEM((rows, tk), F32)] * 2 + [pltpu.VMEM((rows, tk), BF16)] * 2


def _normalize(acc):
    return acc[:, 0:LANE] / acc[:, LANE:LANE + 1]


def _merge_heads(o):
    lane = lax.broadcasted_iota(jnp.int32, (TM, LANE), 1)
    return jnp.where(lane < HEAD_DIM, o[0:TM], o[TM:2 * TM])


def _init_flash(q_refs, q2_ref, m_ref, acc_ref):
    n = 0
    for q_ref in q_refs:
        for j in range(q_ref.shape[2] // LANE):
            q2_ref[2 * TM * n:2 * TM * (n + 1)] = _split_heads(q_ref[0, :, LANE * j:LANE * (j + 1)])
            n += 1
    m_ref[...] = jnp.full(m_ref.shape, NEG, F32)
    acc_ref[...] = jnp.zeros(acc_ref.shape, F32)


def _flash_scratch(n_blocks):
    rows = 2 * TM * n_blocks
    return [pltpu.VMEM((rows, LANE), BF16), pltpu.VMEM((rows, 1), F32), pltpu.VMEM((rows, 2 * LANE), F32)]


def _key_chunk(t_len):
    return next(tk for tk in (768, 640, 512, 384, 256) if t_len % tk == 0)


def _store_heads(o_ref, acc_ref, n_tiles, finish):
    n_lane_blocks = o_ref.shape[2] // LANE
    for u in range(n_tiles):
        for j in range(n_lane_blocks):
            n = u * n_lane_blocks + j
            o = _normalize(acc_ref[2 * TM * n:2 * TM * (n + 1)])
            o_ref[0, TM * u:TM * (u + 1), LANE * j:LANE * (j + 1)] = finish(o).astype(BF16)


def _gqa_ctx_kernel(q_ref, k_ref, v_ref, o_ref, q2_ref, m_ref, acc_ref):
    _init_flash([q_ref], q2_ref, m_ref, acc_ref)
    for j in range(q_ref.shape[2] // LANE):
        _flash_chunk(q2_ref, k_ref[0], v_ref[0], m_ref, acc_ref, 2 * TM * j)
    _store_heads(o_ref, acc_ref, 1, _merge_heads)


def _gqa_kernel(*refs, tk):
    q_refs = refs[:GQA_Q_TILES]
    k_ref, v_ref, o_ref, q2_ref, m_ref, acc_ref = refs[GQA_Q_TILES:GQA_Q_TILES + 6]
    pipe_refs = refs[GQA_Q_TILES + 6:]
    _init_flash(q_refs, q2_ref, m_ref, acc_ref)

    def chunk_of(ref):
        return lambda i, n: ref[0, pl.ds(pl.multiple_of(i * tk, tk), tk), :]
    _flash_pipeline(q2_ref, chunk_of(k_ref), chunk_of(v_ref), k_ref.shape[1] // tk,
                    GQA_Q_TILES * (o_ref.shape[2] // LANE), m_ref, acc_ref,
                    pipe_refs[0:2], pipe_refs[2:4])
    _store_heads(o_ref, acc_ref, GQA_Q_TILES, _merge_heads)


def _gqa(gq, gk, gv, n_ctx, tk):
    b, t_len, qw = gq.shape
    assert n_ctx == TM and (t_len - n_ctx) % (GQA_Q_TILES * TM) == 0
    n_lane_blocks = qw // LANE
    ctx_out = pl.pallas_call(
        _gqa_ctx_kernel,
        grid=(b,),
        in_specs=[pl.BlockSpec((1, TM, qw), lambda b: (b, 0, 0)),
                  pl.BlockSpec((1, TM, LANE), lambda b: (b, 0, 0)),
                  pl.BlockSpec((1, TM, 2 * LANE), lambda b: (b, 0, 0))],
        out_specs=pl.BlockSpec((1, TM, qw), lambda b: (b, 0, 0)),
        out_shape=jax.ShapeDtypeStruct((b, TM, qw), BF16),
        scratch_shapes=_flash_scratch(n_lane_blocks),
        compiler_params=_params(1),
        name="gqa_attention_ctx",
    )(gq, gk, gv)
    latent_out = pl.pallas_call(
        functools.partial(_gqa_kernel, tk=tk),
        grid=(b, (t_len - n_ctx) // (GQA_Q_TILES * TM)),
        in_specs=([pl.BlockSpec((1, TM, qw), lambda b, s, u=u: (b, 1 + GQA_Q_TILES * s + u, 0))
                   for u in range(GQA_Q_TILES)]
                  + [pl.BlockSpec((1, t_len, LANE), lambda b, s: (b, 0, 0)),
                     pl.BlockSpec((1, t_len, 2 * LANE), lambda b, s: (b, 0, 0))]),
        out_specs=pl.BlockSpec((1, GQA_Q_TILES * TM, qw), lambda b, s: (b, s, 0)),
        out_shape=jax.ShapeDtypeStruct((b, t_len - n_ctx, qw), BF16),
        scratch_shapes=_flash_scratch(GQA_Q_TILES * n_lane_blocks) + _pipeline_scratch(tk),
        compiler_params=_params(2),
        name="gqa_attention",
    )(*([gq] * GQA_Q_TILES), gk, gv)
    return jnp.concatenate([ctx_out, latent_out], axis=1)


def _na_kernel(q_ref, k_ref, v_ref, bias_ref, o_ref, q2_ref, m_ref, acc_ref, *pipe_refs, n_ctx, rows):
    t = pl.program_id(1)
    n_blocks = q_ref.shape[2] // LANE
    r2 = 2 * TM
    s_refs, p_refs = pipe_refs[0:2], pipe_refs[2:4]
    _init_flash([q_ref], q2_ref, m_ref, acc_ref)

    @pl.when(t == 0)
    def _():
        for j in range(n_blocks):
            _flash_chunk(q2_ref, k_ref[0, 0:n_ctx, LANE * j:LANE * (j + 1)],
                         v_ref[0, 0:n_ctx, 2 * LANE * j:2 * LANE * (j + 1)], m_ref, acc_ref, r2 * j)
            o_ref[0, :, LANE * j:LANE * (j + 1)] = _merge_heads(_normalize(acc_ref[r2 * j:r2 * (j + 1)])).astype(BF16)

    @pl.when(t > 0)
    def _():
        first_row = NA_ROWS_PER_TILE * (t - 1)
        start = jnp.clip(first_row - WIN_R // 2, 0, rows - NA_KEY_ROWS)
        off = pl.multiple_of(n_ctx + GRID_W * start, GRID_W)

        def score(j, slot):
            q2 = q2_ref[r2 * j:r2 * (j + 1)]
            s_refs[slot][:, 0:n_ctx] = _dot_nt(q2, k_ref[0, 0:n_ctx, LANE * j:LANE * (j + 1)])
            s_refs[slot][:, n_ctx:] = _dot_nt(q2, k_ref[0, pl.ds(off, NA_WIN), LANE * j:LANE * (j + 1)])

        def softmax(j, slot):
            for r in range(r2 // SOFTMAX_ROWS):
                rows_r = slice(SOFTMAX_ROWS * r, SOFTMAX_ROWS * (r + 1))
                head, row = divmod(SOFTMAX_ROWS * r, TM)
                s_ctx = s_refs[slot][rows_r, 0:n_ctx]
                s_win = s_refs[slot][rows_r, n_ctx:] + bias_ref[2 * j + head, 0, row:row + SOFTMAX_ROWS, :]
                m = jnp.maximum(jnp.max(s_ctx, axis=-1, keepdims=True), jnp.max(s_win, axis=-1, keepdims=True))
                p_refs[slot][rows_r, 0:n_ctx] = jnp.exp2(s_ctx - m).astype(BF16)
                p_refs[slot][rows_r, n_ctx:] = jnp.exp2(s_win - m).astype(BF16)

        def output(j, slot):
            acc = (_dot(p_refs[slot][:, 0:n_ctx], v_ref[0, 0:n_ctx, 2 * LANE * j:2 * LANE * (j + 1)])
                   + _dot(p_refs[slot][:, n_ctx:], v_ref[0, pl.ds(off, NA_WIN), 2 * LANE * j:2 * LANE * (j + 1)]))
            o_ref[0, :, LANE * j:LANE * (j + 1)] = _merge_heads(_normalize(acc)).astype(BF16)

        score(0, 0)
        for j in range(n_blocks):
            if j + 1 < n_blocks:
                score(j + 1, (j + 1) % 2)
            softmax(j, j % 2)
            if j > 0:
                output(j - 1, (j - 1) % 2)
        output(n_blocks - 1, (n_blocks - 1) % 2)


def _na(naq, nak, nav, bias, n_ctx):
    b, t_len, qw = naq.shape
    n_tiles = t_len // TM
    rows = (t_len - n_ctx) // GRID_W
    n_keys = n_ctx + NA_WIN

    def bias_index(b, t):
        return (0, jnp.where(t <= 1, 0, jnp.where(t == n_tiles - 1, 2, 1)), 0, 0)

    def resident(width):
        return pl.BlockSpec((1, t_len, width), lambda b, t: (b, 0, 0), pipeline_mode=pl.Buffered(1))

    return pl.pallas_call(
        functools.partial(_na_kernel, n_ctx=n_ctx, rows=rows),
        grid=(b, n_tiles),
        in_specs=[pl.BlockSpec((1, TM, qw), lambda b, t: (b, t, 0)), resident(qw), resident(2 * qw),
                  pl.BlockSpec((bias.shape[0], 1, TM, NA_WIN), bias_index, pipeline_mode=pl.Buffered(1))],
        out_specs=pl.BlockSpec((1, TM, qw), lambda b, t: (b, t, 0)),
        out_shape=jax.ShapeDtypeStruct((b, t_len, qw), BF16),
        scratch_shapes=(_flash_scratch(qw // LANE)
                        + [pltpu.VMEM((2 * TM, n_keys), F32)] * 2 + [pltpu.VMEM((2 * TM, n_keys), BF16)] * 2),
        compiler_params=_params(2),
        name="neighbourhood_attention",
    )(naq, nak, nav, bias)


def _na_bias_table(rpb, rows):
    g_of_pattern = np.array([0, 2, rows // NA_ROWS_PER_TILE - 1])
    a = np.arange(NA_ROWS_PER_TILE)
    r = NA_ROWS_PER_TILE * g_of_pattern[:, None] + a[None, :]
    start = np.clip(NA_ROWS_PER_TILE * g_of_pattern - WIN_R // 2, 0, rows - NA_KEY_ROWS)
    rs = np.clip(r - WIN_R // 2, 0, rows - WIN_R)
    key_row = start[:, None] + np.arange(NA_KEY_ROWS)[None, :]
    row_ok = (key_row[:, None, :] >= rs[:, :, None]) & (key_row[:, None, :] < rs[:, :, None] + WIN_R)
    row_off = np.clip(key_row[:, None, :] - r[:, :, None] + (WIN_R - 1), 0, 2 * WIN_R - 2)
    cols = np.arange(GRID_W)
    col_start = np.clip(cols - WIN_C // 2, 0, GRID_W - WIN_C)
    col_ok = (cols[None, :] >= col_start[:, None]) & (cols[None, :] < col_start[:, None] + WIN_C)
    col_off = np.clip(cols[None, :] - cols[:, None] + (WIN_C - 1), 0, 2 * WIN_C - 2)
    ok = row_ok[:, :, None, :, None] & col_ok[None, None, :, None, :]
    bias_rows = rpb.astype(F32)[:, row_off]
    pick_col = (col_off[:, :, None] == np.arange(2 * WIN_C - 1)).astype(np.float32)
    vals = jnp.einsum('hpakb,cjb->hpackj', bias_rows, pick_col, precision=lax.Precision.HIGHEST)
    table = jnp.where(ok[None], vals * LOG2_E, NEG)
    return table.reshape(rpb.shape[0], 3, TM, NA_WIN)


def _diff_kernel(lam_ref, gain_ref, *refs, tk, lambda_init):
    q_refs = refs[:DIFF_Q_TILES]
    k_ref, v_ref, o_ref, q2_ref, m_ref, acc_ref = refs[DIFF_Q_TILES:DIFF_Q_TILES + 6]
    pipe_refs = refs[DIFF_Q_TILES + 6:]
    n_heads = o_ref.shape[2] // LANE
    _init_flash(q_refs, q2_ref, m_ref, acc_ref)

    def chunk_of(ref, width):
        def chunk(i, n):
            h = n % n_heads
            return ref[0, pl.ds(pl.multiple_of(i * tk, tk), tk), width * h:width * (h + 1)]
        return chunk
    _flash_pipeline(q2_ref, chunk_of(k_ref, LANE), chunk_of(v_ref, 2 * LANE), k_ref.shape[1] // tk,
                    DIFF_Q_TILES * n_heads, m_ref, acc_ref, pipe_refs[0:2], pipe_refs[2:4])

    lp = lam_ref[...]
    lam = (jnp.exp(jnp.sum(lp[0:1] * lp[1:2], axis=-1, keepdims=True))
           - jnp.exp(jnp.sum(lp[2:3] * lp[3:4], axis=-1, keepdims=True)) + lambda_init)

    def sub_layer_norm(o):
        d = o[0:TM] - lam * o[TM:2 * TM]
        y = d * lax.rsqrt(jnp.mean(d * d, axis=-1, keepdims=True) + EPS) * gain_ref[...]
        return y * (1.0 - lambda_init)
    _store_heads(o_ref, acc_ref, DIFF_Q_TILES, sub_layer_norm)


def _diff(lam_params, subln_gain, dq, dk, dv, n_ctx, tk, lambda_init):
    b, t_len, qw = dq.shape
    assert n_ctx == TM and (t_len - n_ctx) % (DIFF_Q_TILES * TM) == 0
    hw = LANE * DIFF_HEADS_PER_STEP
    return pl.pallas_call(
        functools.partial(_diff_kernel, tk=tk, lambda_init=lambda_init),
        grid=(b, qw // hw, (t_len - n_ctx) // (DIFF_Q_TILES * TM)),
        in_specs=([pl.BlockSpec(lam_params.shape, lambda b, h, s: (0, 0)),
                   pl.BlockSpec(subln_gain.shape, lambda b, h, s: (0, 0))]
                  + [pl.BlockSpec((1, TM, hw), lambda b, h, s, u=u: (b, 1 + DIFF_Q_TILES * s + u, h))
                     for u in range(DIFF_Q_TILES)]
                  + [pl.BlockSpec((1, t_len, hw), lambda b, h, s: (b, 0, h)),
                     pl.BlockSpec((1, t_len, 2 * hw), lambda b, h, s: (b, 0, h))]),
        out_specs=pl.BlockSpec((1, DIFF_Q_TILES * TM, hw), lambda b, h, s: (b, s, h)),
        out_shape=jax.ShapeDtypeStruct((b, t_len - n_ctx, qw), BF16),
        scratch_shapes=_flash_scratch(DIFF_Q_TILES * DIFF_HEADS_PER_STEP) + _pipeline_scratch(tk),
        compiler_params=_params(3),
        name="diff_attention",
    )(lam_params, subln_gain, *([dq] * DIFF_Q_TILES), dk, dv)


def _ffn_kernel(*refs, n_att, ff_chunks, final, split_tokens, fused_proj):
    if split_tokens:
        x_tile = _token_tile(refs[0], refs[1])
        refs = refs[1:]
    else:
        x_tile = refs[0][0]
    mod_ref = refs[1]
    att_refs = refs[2:2 + n_att]
    wo_refs = refs[2 + n_att:2 + 2 * n_att]
    wg_ref, wu_ref, wd_ref = refs[2 + 2 * n_att:5 + 2 * n_att]
    o_ref = refs[-4] if fused_proj else refs[-1]
    mod = mod_ref[0, 0]
    y = _dot(att_refs[0][0], wo_refs[0][...])
    for a_ref, w_ref in zip(att_refs[1:], wo_refs[1:]):
        y = y + _dot(a_ref[0], w_ref[...])
    x1 = x_tile + mod[2:3] * y
    h = _modulated_norm(x1, mod[3:4], mod[4:5]).astype(BF16)
    d_ff = wg_ref.shape[1]
    bounds = [0]
    for i in range(ff_chunks):
        bounds.append(min(d_ff, -(-(d_ff * (i + 1) // ff_chunks) // MXU_DEPTH) * MXU_DEPTH))
    down = None
    for lo, hi in zip(bounds[:-1], bounds[1:]):
        g = _dot(h, wg_ref[:, lo:hi])
        u = _dot(h, wu_ref[:, lo:hi])
        a = (g * (1.0 / (1.0 + jnp.exp(-g))) * u).astype(BF16)
        part = _dot(a, wd_ref[lo:hi, :])
        down = part if down is None else down + part
    x2 = x1 + mod[5:6] * down
    if final:
        gain_ref = refs[5 + 2 * n_att]
        x2 = x2 * lax.rsqrt(jnp.mean(x2 * x2, axis=-1, keepdims=True) + EPS) * gain_ref[...]
    o_ref[0] = x2
    if fused_proj:
        mod1_ref, w1_ref, c_ref, sa_ref, sb_ref = refs[5 + 2 * n_att:10 + 2 * n_att]
        _diff_projection(x2, mod1_ref[0, 0], w1_ref, c_ref, sa_ref, sb_ref, *refs[-3:])


def _ffn(tokens, mods, atts, wos, wg, wu, wd, final_gain=None, next_proj=None):
    split_tokens = isinstance(tokens, tuple)
    n_rows = atts[0].shape[1]
    b, _, d = tokens[-1].shape if split_tokens else tokens.shape
    final = final_gain is not None
    if split_tokens:
        token_specs = _split_token_specs(d)
        tokens = list(tokens)
    else:
        token_specs = [_token_spec(d, (tokens.shape[1] - n_rows) // TM)]
        tokens = [tokens]
    in_specs = (token_specs + [_mod_spec(d, latent_only=not split_tokens)]
                + [_token_spec(a.shape[2]) for a in atts]
                + [_const_spec(w.shape) for w in wos]
                + [_const_spec(wg.shape), _const_spec(wu.shape), _const_spec(wd.shape)])
    args = [*tokens, mods, *atts, *wos, wg, wu, wd]
    if final:
        in_specs.append(_const_spec(final_gain.shape))
        args.append(final_gain)
    out_specs = _token_spec(d)
    out_shape = jax.ShapeDtypeStruct((b, n_rows, d), F32)
    if next_proj is not None:
        assert split_tokens and not final
        mods1, w1, rope = next_proj
        in_specs += [_mod_spec(d), _const_spec(w1.shape), _rope_spec(), _rope_spec(), _rope_spec()]
        args += [mods1, w1, *rope]
        widths = (DIFF_W, DIFF_W, 2 * DIFF_W)
        out_specs = [out_specs] + [_token_spec(n) for n in widths]
        out_shape = [out_shape] + [jax.ShapeDtypeStruct((b, n_rows, n), BF16) for n in widths]
    return pl.pallas_call(
        functools.partial(_ffn_kernel, n_att=len(atts), ff_chunks=2, final=final, split_tokens=split_tokens,
                          fused_proj=next_proj is not None),
        grid=(b, n_rows // TM),
        in_specs=in_specs,
        out_specs=out_specs,
        out_shape=out_shape,
        compiler_params=_params(2),
        name="outproj_ffn_final" if final else "outproj_ffn",
    )(*args)


def _head_cols(base, heads):
    return np.concatenate([base + HEAD_DIM * h + _DEINT for h in heads])


def _rope_tables(seq, n_ctx):
    t = jnp.arange(seq)
    row = (t // GRID_W).astype(F32)
    col = (t % GRID_W).astype(F32)
    n_freq = HEAD_DIM // 4
    inv = ROPE_THETA ** (-jnp.arange(n_freq, dtype=F32) / n_freq)
    ang = jnp.concatenate([row[:, None] * inv, col[:, None] * inv], axis=-1)
    cos, sin = jnp.cos(ang), jnp.sin(ang)
    zero = jnp.zeros_like(sin)

    def table(first_half, second_half, ctx_value):
        lat = jnp.tile(jnp.concatenate([first_half, second_half], axis=-1), (1, LANE // HEAD_DIM))
        return jnp.concatenate([jnp.full((n_ctx, LANE), ctx_value, F32), lat], axis=0)

    return table(cos, cos, 1.0), table(zero, sin, 0.0), table(-sin, zero, 0.0)


def kernel(x, c, ctx, c_ctx, ada_w, ada_b, ffn_w_gate, ffn_w_up, ffn_w_down, par_w_in, par_w_out, na_rpb,
           gqa_q_gain, gqa_k_gain, diff_w_in, diff_w_out, diff_lambda_q1, diff_lambda_k1, diff_lambda_q2,
           diff_lambda_k2, diff_subln_gain, final_norm_gain):
    b, seq, d = x.shape
    n_ctx = ctx.shape[1]
    assert n_ctx == TM and seq % TM == 0 and d % LANE == 0 and b < 8 and ada_w.shape[0] == 2
    assert par_w_in.shape[-1] == PAR_W and diff_w_in.shape[-1] == 3 * DIFF_W
    rows = seq // GRID_W
    scale = HEAD_DIM ** -0.5 * LOG2_E

    cond = jnp.zeros((8, d), F32).at[:b].set(c).at[b].set(c_ctx)
    mods_all = _ada(cond, ada_w, ada_b).reshape(2, 8, N_MOD, d)

    def mods_of(layer):
        m = mods_all[layer]
        return jnp.stack([jnp.broadcast_to(m[b], (b, N_MOD, d)), m[:b]], axis=1)

    rope = _rope_tables(seq, n_ctx)

    cols0 = np.concatenate([np.arange(NAQ0, GQ0), _head_cols(GQ0, _GQA_HEAD_ORDER), _head_cols(GK0, range(N_HEADS_GKV)),
                            np.arange(GV0, PAR_W)])
    col_scale0 = np.ones((PAR_W,), np.float32)
    col_scale0[NAQ0:NAK0] = scale
    w_in0 = (par_w_in[0][:, cols0] * col_scale0).astype(BF16)
    gains = jnp.zeros((8, LANE), F32)
    gains = gains.at[0].set(jnp.tile(gqa_q_gain[0][_DEINT] * scale, 2)).at[1].set(jnp.tile(gqa_k_gain[0][_DEINT], 2))
    block_mean = jnp.asarray(np.kron(np.eye(LANE // HEAD_DIM), np.full((HEAD_DIM, HEAD_DIM), 1.0 / HEAD_DIM)), BF16)
    naq, nak, nav, gq, gk, gv = _proj_par(ctx, x, mods_of(0), w_in0, rope, gains, block_mean)
    att_na = _na(naq, nak, nav, _na_bias_table(na_rpb[0], rows), n_ctx)
    att_g = _gqa(gq, gk, gv, n_ctx, tk=_key_chunk(n_ctx + seq))
    wo_na = par_w_out[0][0:NA_W].astype(BF16)
    wo_g = par_w_out[0][NA_W + np.concatenate([HEAD_DIM * h + np.arange(HEAD_DIM) for h in _GQA_HEAD_ORDER])].astype(BF16)
    sub_heads = range(2 * N_HEADS_DIFF)
    cols1 = np.concatenate([_head_cols(0, sub_heads), _head_cols(DIFF_W, sub_heads), np.arange(2 * DIFF_W, 3 * DIFF_W)])
    col_scale1 = np.ones((3 * DIFF_W,), np.float32)
    col_scale1[0:DIFF_W] = scale
    w_in1 = (diff_w_in[0][:, cols1] * col_scale1).astype(BF16)
    xa, dq, dk, dv = _ffn((ctx, x), mods_of(0), [att_na, att_g], [wo_na, wo_g],
                          ffn_w_gate[0].astype(BF16), ffn_w_up[0].astype(BF16), ffn_w_down[0].astype(BF16),
                          next_proj=(mods_of(1), w_in1, rope))
    lambda_init = 0.8 - 0.6 * float(np.exp(-0.3 * 1))
    lam_params = jnp.stack([diff_lambda_q1[0], diff_lambda_k1[0], diff_lambda_q2[0], diff_lambda_k2[0]]).astype(F32)
    att_d = _diff(lam_params, diff_subln_gain[0].reshape(1, -1).astype(F32), dq, dk, dv, n_ctx,
                  tk=_key_chunk(n_ctx + seq),
                  lambda_init=lambda_init)
    return _ffn(xa, mods_of(1), [att_d], [diff_w_out[0].astype(BF16)],
                ffn_w_gate[1].astype(BF16), ffn_w_up[1].astype(BF16), ffn_w_down[1].astype(BF16),
                final_gain=final_norm_gain.reshape(1, -1).astype(F32))
```

```python
import functools

import numpy as np
import jax
import jax.numpy as jnp
from jax import lax
from jax.experimental import pallas as pl
from jax.experimental.pallas import tpu as pltpu

F32 = jnp.float32
BF16 = jnp.bfloat16

GRID_W = 64
HEAD_DIM = 64
WIN_R = 8
WIN_C = 16
N_MOD = 6
ROPE_THETA = 10000.0
EPS = 1e-6

LANE = 128
TM = 256
NA_ROWS_PER_TILE = TM // GRID_W
NA_KEY_ROWS = 12
NA_WIN = NA_KEY_ROWS * GRID_W
ROW_BLOCK = 128
DIFF_HEADS_PER_STEP = 2
GQA_Q_TILES = 4
DIFF_Q_TILES = 8
SOFTMAX_ROWS = 32
LOG2_E = 1.4426950408889634
FFN_TILES = 2
MXU_DEPTH = 256
NEG = -1e30
VMEM_LIMIT = 56 * 1024 * 1024

N_HEADS_NA = 8
N_HEADS_GQ = 8
N_HEADS_GKV = 2
N_HEADS_DIFF = 8
NA_W = N_HEADS_NA * HEAD_DIM
GQ_W = N_HEADS_GQ * HEAD_DIM
GKV_W = N_HEADS_GKV * HEAD_DIM
NAQ0, NAK0, NAV0, GQ0, GK0, GV0, PAR_W = (int(v) for v in np.cumsum([0, NA_W, NA_W, NA_W, GQ_W, GKV_W, GKV_W]))
DIFF_W = N_HEADS_DIFF * 2 * HEAD_DIM
ADA_TILE = 1024

_DEINT = np.concatenate([np.arange(0, HEAD_DIM, 2), np.arange(1, HEAD_DIM, 2)])
_GQA_HEAD_ORDER = (0, 4, 1, 5, 2, 6, 3, 7)


def _dot(a, b):
    return jnp.dot(a, b, preferred_element_type=F32)


def _dot_nt(a, b):
    return lax.dot_general(a, b, (((1,), (1,)), ((), ())), preferred_element_type=F32)


def _params(n_grid):
    return pltpu.CompilerParams(dimension_semantics=("arbitrary",) * n_grid, vmem_limit_bytes=VMEM_LIMIT)


def _const_spec(shape):
    return pl.BlockSpec(shape, lambda *_: (0,) * len(shape), pipeline_mode=pl.Buffered(1))


def _split_bf16(a):
    hi = a.astype(BF16)
    return hi, (a - hi.astype(F32)).astype(BF16)


def _ada_kernel(cond_ref, w_ref, b_ref, o_ref):
    c = cond_ref[...]
    a_hi, a_lo = _split_bf16(c * (1.0 / (1.0 + jnp.exp(-c))))
    w_hi, w_lo = _split_bf16(w_ref[0])
    o_ref[0] = _dot(a_hi, w_hi) + _dot(a_lo, w_hi) + _dot(a_hi, w_lo) + b_ref[0]


def _ada(cond, ada_w, ada_b):
    depth, d, n = ada_w.shape
    tn = ADA_TILE
    return pl.pallas_call(
        _ada_kernel,
        grid=(depth, n // tn),
        in_specs=[pl.BlockSpec(cond.shape, lambda l, j: (0, 0)),
                  pl.BlockSpec((1, d, tn), lambda l, j: (l, 0, j)),
                  pl.BlockSpec((1, 1, tn), lambda l, j: (l, 0, j))],
        out_specs=pl.BlockSpec((1, cond.shape[0], tn), lambda l, j: (l, 0, j)),
        out_shape=jax.ShapeDtypeStruct((depth, cond.shape[0], n), F32),
        compiler_params=_params(2),
        name="ada_modulation",
    )(cond, ada_w, ada_b.reshape(depth, 1, n))


def _modulated_norm(x, shift, scale):
    ms = jnp.mean(x * x, axis=-1, keepdims=True)
    return (x * lax.rsqrt(ms + EPS)) * (1.0 + scale) + shift


def _group_mean_sq(x, bd):
    hi, lo = _split_bf16(x * x)
    return _dot(hi, bd) + _dot(lo, bd)


def _rope(x, c, sa, sb):
    return x * c + pltpu.roll(x, 32, 1) * sa + pltpu.roll(x, 96, 1) * sb


def _ones_column(rows):
    return (lax.broadcasted_iota(jnp.int32, (rows, LANE), 1) == 0).astype(BF16)


def _store_values(v_ref, p, col0, n_blocks):
    ones = _ones_column(p.shape[0])
    for j in range(n_blocks):
        v_ref[0, :, 2 * LANE * j:2 * LANE * j + LANE] = p[:, col0 + LANE * j:col0 + LANE * (j + 1)].astype(BF16)
        v_ref[0, :, 2 * LANE * j + LANE:2 * LANE * (j + 1)] = ones


def _token_tile(ctx_ref, x_ref):
    return jnp.where(pl.program_id(1) == 0, ctx_ref[0], x_ref[0])


def _proj_par_kernel(ctx_ref, x_ref, mod_ref, w_ref, c_ref, sa_ref, sb_ref, gain_ref, bd_ref,
                     naq_ref, nak_ref, nav_ref, gq_ref, gk_ref, gv_ref):
    mod = mod_ref[0, 0]
    h = _modulated_norm(_token_tile(ctx_ref, x_ref), mod[0:1], mod[1:2]).astype(BF16)
    p = _dot(h, w_ref[...])
    naq_ref[0] = p[:, NAQ0:NAK0].astype(BF16)
    nak_ref[0] = p[:, NAK0:NAV0].astype(BF16)
    _store_values(nav_ref, p, NAV0, NA_W // LANE)
    c, sa, sb, bd = c_ref[...], sa_ref[...], sb_ref[...], bd_ref[...]

    def qk_norm_rope(g, gain):
        g = g * lax.rsqrt(_group_mean_sq(g, bd) + EPS) * gain
        return _rope(g, c, sa, sb).astype(BF16)

    for j in range(GQ_W // LANE):
        gq_ref[0, :, LANE * j:LANE * (j + 1)] = qk_norm_rope(p[:, GQ0 + LANE * j:GQ0 + LANE * (j + 1)], gain_ref[0:1, :])
    gk_ref[0] = qk_norm_rope(p[:, GK0:GV0], gain_ref[1:2, :])
    _store_values(gv_ref, p, GV0, GKV_W // LANE)


def _diff_projection(x, mod, w_ref, c_ref, sa_ref, sb_ref, dq_ref, dk_ref, dv_ref):
    h = _modulated_norm(x, mod[0:1], mod[1:2]).astype(BF16)
    p = _dot(h, w_ref[...])
    c, sa, sb = c_ref[...], sa_ref[...], sb_ref[...]
    for j in range(DIFF_W // LANE):
        dq_ref[0, :, LANE * j:LANE * (j + 1)] = _rope(p[:, LANE * j:LANE * (j + 1)], c, sa, sb).astype(BF16)
        dk_ref[0, :, LANE * j:LANE * (j + 1)] = _rope(p[:, DIFF_W + LANE * j:DIFF_W + LANE * (j + 1)], c, sa, sb).astype(BF16)
    _store_values(dv_ref, p, 2 * DIFF_W, DIFF_W // LANE)


def _token_spec(width, tile_off=0):
    return pl.BlockSpec((1, TM, width), lambda b, t: (b, t + tile_off, 0))


def _split_token_specs(d):
    return [pl.BlockSpec((1, TM, d), lambda b, t: (b, 0, 0)),
            pl.BlockSpec((1, TM, d), lambda b, t: (b, jnp.maximum(t - 1, 0), 0))]


def _mod_spec(d, latent_only=False):
    if latent_only:
        return pl.BlockSpec((1, 1, N_MOD, d), lambda b, t: (b, 1, 0, 0))
    return pl.BlockSpec((1, 1, N_MOD, d), lambda b, t: (b, jnp.minimum(t, 1), 0, 0))


def _rope_spec():
    return pl.BlockSpec((TM, LANE), lambda b, t: (t, 0))


def _proj_par(ctx, x, mods, w, rope, gains, bd):
    b, seq, d = x.shape
    t_len = ctx.shape[1] + seq
    widths = (NA_W, NA_W, 2 * NA_W, GQ_W, GKV_W, 2 * GKV_W)
    return pl.pallas_call(
        _proj_par_kernel,
        grid=(b, t_len // TM),
        in_specs=_split_token_specs(d) + [_mod_spec(d), _const_spec(w.shape), _rope_spec(), _rope_spec(), _rope_spec(),
                                          _const_spec(gains.shape), _const_spec(bd.shape)],
        out_specs=[_token_spec(n) for n in widths],
        out_shape=[jax.ShapeDtypeStruct((b, t_len, n), BF16) for n in widths],
        compiler_params=_params(2),
        name="proj_parallel_mixer",
    )(ctx, x, mods, w, *rope, gains, bd)


def _split_heads(q):
    lane = lax.broadcasted_iota(jnp.int32, q.shape, 1)
    zero = jnp.zeros_like(q)
    return jnp.concatenate([jnp.where(lane < HEAD_DIM, q, zero), jnp.where(lane >= HEAD_DIM, q, zero)], axis=0)


def _softmax_update(s, m_prev):
    m_new = jnp.maximum(m_prev, jnp.max(s, axis=-1, keepdims=True))
    p = jnp.exp2(s - m_new)
    return p, jnp.exp2(m_prev - m_new), m_new


def _flash_chunk(q2_ref, k, v, m_ref, acc_ref, row0):
    for r in range(2 * TM // ROW_BLOCK):
        rows = slice(row0 + ROW_BLOCK * r, row0 + ROW_BLOCK * (r + 1))
        s = _dot_nt(q2_ref[rows], k)
        p, alpha, m_new = _softmax_update(s, m_ref[rows])
        acc_ref[rows] = alpha * acc_ref[rows] + _dot(p.astype(BF16), v)
        m_ref[rows] = m_new


def _flash_pipeline(q2_ref, k_of, v_of, n_chunks, n_blocks, m_ref, acc_ref, s_refs, p_refs):
    assert n_blocks % 2 == 0
    r2 = 2 * TM

    def score(i, j, slot):
        s_refs[slot][...] = _dot_nt(q2_ref[r2 * j:r2 * (j + 1)], k_of(i, j))

    def softmax(j, slot):
        for r in range(r2 // SOFTMAX_ROWS):
            rows = slice(SOFTMAX_ROWS * r, SOFTMAX_ROWS * (r + 1))
            state_rows = slice(r2 * j + SOFTMAX_ROWS * r, r2 * j + SOFTMAX_ROWS * (r + 1))
            p, alpha, m_new = _softmax_update(s_refs[slot][rows], m_ref[state_rows])
            p_refs[slot][rows] = p.astype(BF16)
            acc_ref[state_rows] = acc_ref[state_rows] * jnp.broadcast_to(alpha, (SOFTMAX_ROWS, 2 * LANE))
            m_ref[state_rows] = m_new

    def accumulate(i, j, slot):
        blk = slice(r2 * j, r2 * (j + 1))
        acc_ref[blk] += _dot(p_refs[slot][...], v_of(i, j))

    last_slot = (n_blocks - 1) % 2
    p_refs[last_slot][...] = jnp.zeros(p_refs[last_slot].shape, BF16)
    score(0, 0, 0)

    def body(i, carry):
        for j in range(n_blocks):
            if j + 1 < n_blocks:
                score(i, j + 1, (j + 1) % 2)
            else:
                score(jnp.minimum(i + 1, n_chunks - 1), 0, 0)
            softmax(j, j % 2)
            if j > 0:
                accumulate(i, j - 1, (j - 1) % 2)
            else:
                accumulate(jnp.maximum(i - 1, 0), n_blocks - 1, last_slot)
        return carry

    lax.fori_loop(0, n_chunks, body, 0)
    accumulate(n_chunks - 1, n_blocks - 1, last_slot)


def _pipeline_scratch(tk):
    rows = 2 * TM
    return [pltpu.VMEM((rows, tk), F32)] * 2 + [pltpu.VMEM((rows, tk), BF16)] * 2


def _normalize(acc):
    return acc[:, 0:LANE] / acc[:, LANE:LANE + 1]


def _merge_heads(o):
    lane = lax.broadcasted_iota(jnp.int32, (TM, LANE), 1)
    return jnp.where(lane < HEAD_DIM, o[0:TM], o[TM:2 * TM])


def _init_flash(q_refs, q2_ref, m_ref, acc_ref):
    n = 0
    for q_ref in q_refs:
        for j in range(q_ref.shape[2] // LANE):
            q2_ref[2 * TM * n:2 * TM * (n + 1)] = _split_heads(q_ref[0, :, LANE * j:LANE * (j + 1)])
            n += 1
    m_ref[...] = jnp.full(m_ref.shape, NEG, F32)
    acc_ref[...] = jnp.zeros(acc_ref.shape, F32)


def _flash_scratch(n_blocks):
    rows = 2 * TM * n_blocks
    return [pltpu.VMEM((rows, LANE), BF16), pltpu.VMEM((rows, 1), F32), pltpu.VMEM((rows, 2 * LANE), F32)]


def _key_chunk(t_len):
    return next(tk for tk in (768, 640, 512, 384, 256) if t_len % tk == 0)


def _store_heads(o_ref, acc_ref, n_tiles, finish):
    n_lane_blocks = o_ref.shape[2] // LANE
    for u in range(n_tiles):
        for j in range(n_lane_blocks):
            n = u * n_lane_blocks + j
            o = _normalize(acc_ref[2 * TM * n:2 * TM * (n + 1)])
            o_ref[0, TM * u:TM * (u + 1), LANE * j:LANE * (j + 1)] = finish(o).astype(BF16)


def _gqa_ctx_kernel(q_ref, k_ref, v_ref, o_ref, q2_ref, m_ref, acc_ref):
    _init_flash([q_ref], q2_ref, m_ref, acc_ref)
    for j in range(q_ref.shape[2] // LANE):
        _flash_chunk(q2_ref, k_ref[0], v_ref[0], m_ref, acc_ref, 2 * TM * j)
    _store_heads(o_ref, acc_ref, 1, _merge_heads)


def _gqa_kernel(*refs, tk):
    q_refs = refs[:GQA_Q_TILES]
    k_ref, v_ref, o_ref, q2_ref, m_ref, acc_ref = refs[GQA_Q_TILES:GQA_Q_TILES + 6]
    pipe_refs = refs[GQA_Q_TILES + 6:]
    _init_flash(q_refs, q2_ref, m_ref, acc_ref)

    def chunk_of(ref):
        return lambda i, n: ref[0, pl.ds(pl.multiple_of(i * tk, tk), tk), :]
    _flash_pipeline(q2_ref, chunk_of(k_ref), chunk_of(v_ref), k_ref.shape[1] // tk,
                    GQA_Q_TILES * (o_ref.shape[2] // LANE), m_ref, acc_ref,
                    pipe_refs[0:2], pipe_refs[2:4])
    _store_heads(o_ref, acc_ref, GQA_Q_TILES, _merge_heads)


def _gqa(gq, gk, gv, n_ctx, tk):
    b, t_len, qw = gq.shape
    assert n_ctx == TM and (t_len - n_ctx) % (GQA_Q_TILES * TM) == 0
    n_lane_blocks = qw // LANE
    ctx_out = pl.pallas_call(
        _gqa_ctx_kernel,
        grid=(b,),
        in_specs=[pl.BlockSpec((1, TM, qw), lambda b: (b, 0, 0)),
                  pl.BlockSpec((1, TM, LANE), lambda b: (b, 0, 0)),
                  pl.BlockSpec((1, TM, 2 * LANE), lambda b: (b, 0, 0))],
        out_specs=pl.BlockSpec((1, TM, qw), lambda b: (b, 0, 0)),
        out_shape=jax.ShapeDtypeStruct((b, TM, qw), BF16),
        scratch_shapes=_flash_scratch(n_lane_blocks),
        compiler_params=_params(1),
        name="gqa_attention_ctx",
    )(gq, gk, gv)
    latent_out = pl.pallas_call(
        functools.partial(_gqa_kernel, tk=tk),
        grid=(b, (t_len - n_ctx) // (GQA_Q_TILES * TM)),
        in_specs=([pl.BlockSpec((1, TM, qw), lambda b, s, u=u: (b, 1 + GQA_Q_TILES * s + u, 0))
                   for u in range(GQA_Q_TILES)]
                  + [pl.BlockSpec((1, t_len, LANE), lambda b, s: (b, 0, 0)),
                     pl.BlockSpec((1, t_len, 2 * LANE), lambda b, s: (b, 0, 0))]),
        out_specs=pl.BlockSpec((1, GQA_Q_TILES * TM, qw), lambda b, s: (b, s, 0)),
        out_shape=jax.ShapeDtypeStruct((b, t_len - n_ctx, qw), BF16),
        scratch_shapes=_flash_scratch(GQA_Q_TILES * n_lane_blocks) + _pipeline_scratch(tk),
        compiler_params=_params(2),
        name="gqa_attention",
    )(*([gq] * GQA_Q_TILES), gk, gv)
    return jnp.concatenate([ctx_out, latent_out], axis=1)


def _na_kernel(q_ref, k_ref, v_ref, bias_ref, o_ref, q2_ref, m_ref, acc_ref, *pipe_refs, n_ctx, rows):
    t = pl.program_id(1)
    n_blocks = q_ref.shape[2] // LANE
    r2 = 2 * TM
    s_refs, p_refs = pipe_refs[0:2], pipe_refs[2:4]
    _init_flash([q_ref], q2_ref, m_ref, acc_ref)

    @pl.when(t == 0)
    def _():
        for j in range(n_blocks):
            _flash_chunk(q2_ref, k_ref[0, 0:n_ctx, LANE * j:LANE * (j + 1)],
                         v_ref[0, 0:n_ctx, 2 * LANE * j:2 * LANE * (j + 1)], m_ref, acc_ref, r2 * j)
            o_ref[0, :, LANE * j:LANE * (j + 1)] = _merge_heads(_normalize(acc_ref[r2 * j:r2 * (j + 1)])).astype(BF16)

    @pl.when(t > 0)
    def _():
        first_row = NA_ROWS_PER_TILE * (t - 1)
        start = jnp.clip(first_row - WIN_R // 2, 0, rows - NA_KEY_ROWS)
        off = pl.multiple_of(n_ctx + GRID_W * start, GRID_W)

        def score(j, slot):
            q2 = q2_ref[r2 * j:r2 * (j + 1)]
            s_refs[slot][:, 0:n_ctx] = _dot_nt(q2, k_ref[0, 0:n_ctx, LANE * j:LANE * (j + 1)])
            s_refs[slot][:, n_ctx:] = _dot_nt(q2, k_ref[0, pl.ds(off, NA_WIN), LANE * j:LANE * (j + 1)])

        def softmax(j, slot):
            for r in range(r2 // SOFTMAX_ROWS):
                rows_r = slice(SOFTMAX_ROWS * r, SOFTMAX_ROWS * (r + 1))
                head, row = divmod(SOFTMAX_ROWS * r, TM)
                s_ctx = s_refs[slot][rows_r, 0:n_ctx]
                s_win = s_refs[slot][rows_r, n_ctx:] + bias_ref[2 * j + head, 0, row:row + SOFTMAX_ROWS, :]
                m = jnp.maximum(jnp.max(s_ctx, axis=-1, keepdims=True), jnp.max(s_win, axis=-1, keepdims=True))
                p_refs[slot][rows_r, 0:n_ctx] = jnp.exp2(s_ctx - m).astype(BF16)
                p_refs[slot][rows_r, n_ctx:] = jnp.exp2(s_win - m).astype(BF16)

        def output(j, slot):
            acc = (_dot(p_refs[slot][:, 0:n_ctx], v_ref[0, 0:n_ctx, 2 * LANE * j:2 * LANE * (j + 1)])
                   + _dot(p_refs[slot][:, n_ctx:], v_ref[0, pl.ds(off, NA_WIN), 2 * LANE * j:2 * LANE * (j + 1)]))
            o_ref[0, :, LANE * j:LANE * (j + 1)] = _merge_heads(_normalize(acc)).astype(BF16)

        score(0, 0)
        for j in range(n_blocks):
            if j + 1 < n_blocks:
                score(j + 1, (j + 1) % 2)
            softmax(j, j % 2)
            if j > 0:
                output(j - 1, (j - 1) % 2)
        output(n_blocks - 1, (n_blocks - 1) % 2)


def _na(naq, nak, nav, bias, n_ctx):
    b, t_len, qw = naq.shape
    n_tiles = t_len // TM
    rows = (t_len - n_ctx) // GRID_W
    n_keys = n_ctx + NA_WIN

    def bias_index(b, t):
        return (0, jnp.where(t <= 1, 0, jnp.where(t == n_tiles - 1, 2, 1)), 0, 0)

    def resident(width):
        return pl.BlockSpec((1, t_len, width), lambda b, t: (b, 0, 0), pipeline_mode=pl.Buffered(1))

    return pl.pallas_call(
        functools.partial(_na_kernel, n_ctx=n_ctx, rows=rows),
        grid=(b, n_tiles),
        in_specs=[pl.BlockSpec((1, TM, qw), lambda b, t: (b, t, 0)), resident(qw), resident(2 * qw),
                  pl.BlockSpec((bias.shape[0], 1, TM, NA_WIN), bias_index, pipeline_mode=pl.Buffered(1))],
        out_specs=pl.BlockSpec((1, TM, qw), lambda b, t: (b, t, 0)),
        out_shape=jax.ShapeDtypeStruct((b, t_len, qw), BF16),
        scratch_shapes=(_flash_scratch(qw // LANE)
                        + [pltpu.VMEM((2 * TM, n_keys), F32)] * 2 + [pltpu.VMEM((2 * TM, n_keys), BF16)] * 2),
        compiler_params=_params(2),
        name="neighbourhood_attention",
    )(naq, nak, nav, bias)


def _na_bias_table(rpb, rows):
    g_of_pattern = np.array([0, 2, rows // NA_ROWS_PER_TILE - 1])
    a = np.arange(NA_ROWS_PER_TILE)
    r = NA_ROWS_PER_TILE * g_of_pattern[:, None] + a[None, :]
    start = np.clip(NA_ROWS_PER_TILE * g_of_pattern - WIN_R // 2, 0, rows - NA_KEY_ROWS)
    rs = np.clip(r - WIN_R // 2, 0, rows - WIN_R)
    key_row = start[:, None] + np.arange(NA_KEY_ROWS)[None, :]
    row_ok = (key_row[:, None, :] >= rs[:, :, None]) & (key_row[:, None, :] < rs[:, :, None] + WIN_R)
    row_off = np.clip(key_row[:, None, :] - r[:, :, None] + (WIN_R - 1), 0, 2 * WIN_R - 2)
    cols = np.arange(GRID_W)
    col_start = np.clip(cols - WIN_C // 2, 0, GRID_W - WIN_C)
    col_ok = (cols[None, :] >= col_start[:, None]) & (cols[None, :] < col_start[:, None] + WIN_C)
    col_off = np.clip(cols[None, :] - cols[:, None] + (WIN_C - 1), 0, 2 * WIN_C - 2)
    ok = row_ok[:, :, None, :, None] & col_ok[None, None, :, None, :]
    bias_rows = rpb.astype(F32)[:, row_off]
    pick_col = (col_off[:, :, None] == np.arange(2 * WIN_C - 1)).astype(np.float32)
    vals = jnp.einsum('hpakb,cjb->hpackj', bias_rows, pick_col, precision=lax.Precision.HIGHEST)
    table = jnp.where(ok[None], vals * LOG2_E, NEG)
    return table.reshape(rpb.shape[0], 3, TM, NA_WIN)


def _diff_kernel(lam_ref, gain_ref, *refs, tk, lambda_init):
    q_refs = refs[:DIFF_Q_TILES]
    k_ref, v_ref, o_ref, q2_ref, m_ref, acc_ref = refs[DIFF_Q_TILES:DIFF_Q_TILES + 6]
    pipe_refs = refs[DIFF_Q_TILES + 6:]
    n_heads = o_ref.shape[2] // LANE
    _init_flash(q_refs, q2_ref, m_ref, acc_ref)

    def chunk_of(ref, width):
        def chunk(i, n):
            h = n % n_heads
            return ref[0, pl.ds(pl.multiple_of(i * tk, tk), tk), width * h:width * (h + 1)]
        return chunk
    _flash_pipeline(q2_ref, chunk_of(k_ref, LANE), chunk_of(v_ref, 2 * LANE), k_ref.shape[1] // tk,
                    DIFF_Q_TILES * n_heads, m_ref, acc_ref, pipe_refs[0:2], pipe_refs[2:4])

    lp = lam_ref[...]
    lam = (jnp.exp(jnp.sum(lp[0:1] * lp[1:2], axis=-1, keepdims=True))
           - jnp.exp(jnp.sum(lp[2:3] * lp[3:4], axis=-1, keepdims=True)) + lambda_init)

    def sub_layer_norm(o):
        d = o[0:TM] - lam * o[TM:2 * TM]
        y = d * lax.rsqrt(jnp.mean(d * d, axis=-1, keepdims=True) + EPS) * gain_ref[...]
        return y * (1.0 - lambda_init)
    _store_heads(o_ref, acc_ref, DIFF_Q_TILES, sub_layer_norm)


def _diff(lam_params, subln_gain, dq, dk, dv, n_ctx, tk, lambda_init):
    b, t_len, qw = dq.shape
    assert n_ctx == TM and (t_len - n_ctx) % (DIFF_Q_TILES * TM) == 0
    hw = LANE * DIFF_HEADS_PER_STEP
    return pl.pallas_call(
        functools.partial(_diff_kernel, tk=tk, lambda_init=lambda_init),
        grid=(b, qw // hw, (t_len - n_ctx) // (DIFF_Q_TILES * TM)),
        in_specs=([pl.BlockSpec(lam_params.shape, lambda b, h, s: (0, 0)),
                   pl.BlockSpec(subln_gain.shape, lambda b, h, s: (0, 0))]
                  + [pl.BlockSpec((1, TM, hw), lambda b, h, s, u=u: (b, 1 + DIFF_Q_TILES * s + u, h))
                     for u in range(DIFF_Q_TILES)]
                  + [pl.BlockSpec((1, t_len, hw), lambda b, h, s: (b, 0, h)),
                     pl.BlockSpec((1, t_len, 2 * hw), lambda b, h, s: (b, 0, h))]),
        out_specs=pl.BlockSpec((1, DIFF_Q_TILES * TM, hw), lambda b, h, s: (b, s, h)),
        out_shape=jax.ShapeDtypeStruct((b, t_len - n_ctx, qw), BF16),
        scratch_shapes=_flash_scratch(DIFF_Q_TILES * DIFF_HEADS_PER_STEP) + _pipeline_scratch(tk),
        compiler_params=_params(3),
        name="diff_attention",
    )(lam_params, subln_gain, *([dq] * DIFF_Q_TILES), dk, dv)


def _ffn_kernel(*refs, n_att, ff_chunks, final, split_tokens, fused_proj):
    if split_tokens:
        x_tile = _token_tile(refs[0], refs[1])
        refs = refs[1:]
    else:
        x_tile = jnp.concatenate([r[0] for r in refs[:FFN_TILES]], axis=0)
        refs = refs[FFN_TILES - 1:]
    mod_ref = refs[1]
    att_refs = refs[2:2 + n_att]
    wo_refs = refs[2 + n_att:2 + 2 * n_att]
    wg_ref, wu_ref, wd_ref = refs[2 + 2 * n_att:5 + 2 * n_att]
    o_ref = refs[-4] if fused_proj else refs[-1]
    mod = mod_ref[0, 0]
    y = _dot(att_refs[0][0], wo_refs[0][...])
    for a_ref, w_ref in zip(att_refs[1:], wo_refs[1:]):
        y = y + _dot(a_ref[0], w_ref[...])
    x1 = x_tile + mod[2:3] * y
    h = _modulated_norm(x1, mod[3:4], mod[4:5]).astype(BF16)
    d_ff = wg_ref.shape[1]
    bounds = [0]
    for i in range(ff_chunks):
        bounds.append(min(d_ff, -(-(d_ff * (i + 1) // ff_chunks) // MXU_DEPTH) * MXU_DEPTH))
    down = None
    for lo, hi in zip(bounds[:-1], bounds[1:]):
        g = _dot(h, wg_ref[:, lo:hi])
        u = _dot(h, wu_ref[:, lo:hi])
        a = (g * (1.0 / (1.0 + jnp.exp(-g))) * u).astype(BF16)
        part = _dot(a, wd_ref[lo:hi, :])
        down = part if down is None else down + part
    x2 = x1 + mod[5:6] * down
    if final:
        gain_ref = refs[5 + 2 * n_att]
        x2 = x2 * lax.rsqrt(jnp.mean(x2 * x2, axis=-1, keepdims=True) + EPS) * gain_ref[...]
    o_ref[0] = x2
    if fused_proj:
        mod1_ref, w1_ref, c_ref, sa_ref, sb_ref = refs[5 + 2 * n_att:10 + 2 * n_att]
        _diff_projection(x2, mod1_ref[0, 0], w1_ref, c_ref, sa_ref, sb_ref, *refs[-3:])


def _ffn(tokens, mods, atts, wos, wg, wu, wd, final_gain=None, next_proj=None):
    split_tokens = isinstance(tokens, tuple)
    n_rows = atts[0].shape[1]
    b, _, d = tokens[-1].shape if split_tokens else tokens.shape
    final = final_gain is not None
    if split_tokens:
        token_specs = _split_token_specs(d)
        tokens = list(tokens)
    else:
        tile_off = (tokens.shape[1] - n_rows) // TM
        token_specs = [pl.BlockSpec((1, TM, d), lambda b, t, u=u: (b, tile_off + FFN_TILES * t + u, 0))
                       for u in range(FFN_TILES)]
        tokens = [tokens] * FFN_TILES
    step_rows = TM if split_tokens else FFN_TILES * TM
    assert n_rows % step_rows == 0

    def row_spec(width):
        return pl.BlockSpec((1, step_rows, width), lambda b, t: (b, t, 0))
    in_specs = (token_specs + [_mod_spec(d, latent_only=not split_tokens)]
                + [row_spec(a.shape[2]) for a in atts]
                + [_const_spec(w.shape) for w in wos]
                + [_const_spec(wg.shape), _const_spec(wu.shape), _const_spec(wd.shape)])
    args = [*tokens, mods, *atts, *wos, wg, wu, wd]
    if final:
        in_specs.append(_const_spec(final_gain.shape))
        args.append(final_gain)
    out_specs = row_spec(d)
    out_shape = jax.ShapeDtypeStruct((b, n_rows, d), F32)
    if next_proj is not None:
        assert split_tokens and not final
        mods1, w1, rope = next_proj
        in_specs += [_mod_spec(d), _const_spec(w1.shape), _rope_spec(), _rope_spec(), _rope_spec()]
        args += [mods1, w1, *rope]
        widths = (DIFF_W, DIFF_W, 2 * DIFF_W)
        out_specs = [out_specs] + [_token_spec(n) for n in widths]
        out_shape = [out_shape] + [jax.ShapeDtypeStruct((b, n_rows, n), BF16) for n in widths]
    return pl.pallas_call(
        functools.partial(_ffn_kernel, n_att=len(atts), ff_chunks=2, final=final, split_tokens=split_tokens,
                          fused_proj=next_proj is not None),
        grid=(b, n_rows // step_rows),
        in_specs=in_specs,
        out_specs=out_specs,
        out_shape=out_shape,
        compiler_params=_params(2),
        name="outproj_ffn_final" if final else "outproj_ffn",
    )(*args)


def _head_cols(base, heads):
    return np.concatenate([base + HEAD_DIM * h + _DEINT for h in heads])


def _rope_tables(seq, n_ctx):
    t = jnp.arange(seq)
    row = (t // GRID_W).astype(F32)
    col = (t % GRID_W).astype(F32)
    n_freq = HEAD_DIM // 4
    inv = ROPE_THETA ** (-jnp.arange(n_freq, dtype=F32) / n_freq)
    ang = jnp.concatenate([row[:, None] * inv, col[:, None] * inv], axis=-1)
    cos, sin = jnp.cos(ang), jnp.sin(ang)
    zero = jnp.zeros_like(sin)

    def table(first_half, second_half, ctx_value):
        lat = jnp.tile(jnp.concatenate([first_half, second_half], axis=-1), (1, LANE // HEAD_DIM))
        return jnp.concatenate([jnp.full((n_ctx, LANE), ctx_value, F32), lat], axis=0)

    return table(cos, cos, 1.0), table(zero, sin, 0.0), table(-sin, zero, 0.0)


def kernel(x, c, ctx, c_ctx, ada_w, ada_b, ffn_w_gate, ffn_w_up, ffn_w_down, par_w_in, par_w_out, na_rpb,
           gqa_q_gain, gqa_k_gain, diff_w_in, diff_w_out, diff_lambda_q1, diff_lambda_k1, diff_lambda_q2,
           diff_lambda_k2, diff_subln_gain, final_norm_gain):
    b, seq, d = x.shape
    n_ctx = ctx.shape[1]
    assert n_ctx == TM and seq % TM == 0 and d % LANE == 0 and b < 8 and ada_w.shape[0] == 2
    assert par_w_in.shape[-1] == PAR_W and diff_w_in.shape[-1] == 3 * DIFF_W
    rows = seq // GRID_W
    scale = HEAD_DIM ** -0.5 * LOG2_E

    cond = jnp.zeros((8, d), F32).at[:b].set(c).at[b].set(c_ctx)
    mods_all = _ada(cond, ada_w, ada_b).reshape(2, 8, N_MOD, d)

    def mods_of(layer):
        m = mods_all[layer]
        return jnp.stack([jnp.broadcast_to(m[b], (b, N_MOD, d)), m[:b]], axis=1)

    rope = _rope_tables(seq, n_ctx)

    cols0 = np.concatenate([np.arange(NAQ0, GQ0), _head_cols(GQ0, _GQA_HEAD_ORDER), _head_cols(GK0, range(N_HEADS_GKV)),
                            np.arange(GV0, PAR_W)])
    col_scale0 = np.ones((PAR_W,), np.float32)
    col_scale0[NAQ0:NAK0] = scale
    w_in0 = (par_w_in[0][:, cols0] * col_scale0).astype(BF16)
    gains = jnp.zeros((8, LANE), F32)
    gains = gains.at[0].set(jnp.tile(gqa_q_gain[0][_DEINT] * scale, 2)).at[1].set(jnp.tile(gqa_k_gain[0][_DEINT], 2))
    block_mean = jnp.asarray(np.kron(np.eye(LANE // HEAD_DIM), np.full((HEAD_DIM, HEAD_DIM), 1.0 / HEAD_DIM)), BF16)
    naq, nak, nav, gq, gk, gv = _proj_par(ctx, x, mods_of(0), w_in0, rope, gains, block_mean)
    att_na = _na(naq, nak, nav, _na_bias_table(na_rpb[0], rows), n_ctx)
    att_g = _gqa(gq, gk, gv, n_ctx, tk=_key_chunk(n_ctx + seq))
    wo_na = par_w_out[0][0:NA_W].astype(BF16)
    wo_g = par_w_out[0][NA_W + np.concatenate([HEAD_DIM * h + np.arange(HEAD_DIM) for h in _GQA_HEAD_ORDER])].astype(BF16)
    sub_heads = range(2 * N_HEADS_DIFF)
    cols1 = np.concatenate([_head_cols(0, sub_heads), _head_cols(DIFF_W, sub_heads), np.arange(2 * DIFF_W, 3 * DIFF_W)])
    col_scale1 = np.ones((3 * DIFF_W,), np.float32)
    col_scale1[0:DIFF_W] = scale
    w_in1 = (diff_w_in[0][:, cols1] * col_scale1).astype(BF16)
    xa, dq, dk, dv = _ffn((ctx, x), mods_of(0), [att_na, att_g], [wo_na, wo_g],
                          ffn_w_gate[0].astype(BF16), ffn_w_up[0].astype(BF16), ffn_w_down[0].astype(BF16),
                          next_proj=(mods_of(1), w_in1, rope))
    lambda_init = 0.8 - 0.6 * float(np.exp(-0.3 * 1))
    lam_params = jnp.stack([diff_lambda_q1[0], diff_lambda_k1[0], diff_lambda_q2[0], diff_lambda_k2[0]]).astype(F32)
    att_d = _diff(lam_params, diff_subln_gain[0].reshape(1, -1).astype(F32), dq, dk, dv, n_ctx,
                  tk=_key_chunk(n_ctx + seq),
                  lambda_init=lambda_init)
    return _ffn(xa, mods_of(1), [att_d], [diff_w_out[0].astype(BF16)],
                ffn_w_gate[1].astype(BF16), ffn_w_up[1].astype(BF16), ffn_w_down[1].astype(BF16),
                final_gain=final_norm_gain.reshape(1, -1).astype(F32))
```
